```python
import math
import jax
import jax.numpy as jnp
from jax import lax
import numpy as np

D_MODEL = 1024
BATCH = 4
SEQ = 4096
DEPTH = 2
DEC_BATCH = 128
DEC_SEQ = 8
PAST_LEN = 2048
PAGE_SIZE = 128

D_MIX = D_MODEL
W_GROUP = D_MIX // 4
CONV_W = 3
POOL_WINDOWS = (2, 4, 8, 16)
POOL_GROUPS = 4
POOL_CH = W_GROUP // POOL_GROUPS
POOL_BUF = 15
CHUNK = 128
CHUNK_HEADS = 4
CHUNK_CH = W_GROUP // CHUNK_HEADS
ATT_HEADS = 4
HEAD_DIM = W_GROUP // ATT_HEADS
IDX_HEADS = 8
IDX_DIM = 64
IDX_W_SCALE = (IDX_HEADS ** -0.5) * (IDX_DIM ** -0.5)
TOPK_MAX = 256
Q_BLOCK = 128
ROPE_THETA = 10000.0
MEM_HEADS = 4
MEM_HEAD_DIM = D_MODEL // MEM_HEADS
D_FF = 3584
N_EXPERTS = 8
TOP_K_EXPERTS = 2
MOE_BLOCK = 128
EPS = 1e-6

kernel_name = 'hymba_style_conv_pool_gmlp_dsa_decoder_step'


def _in_sizes():
    return (W_GROUP,) * 9 + (IDX_HEADS * IDX_DIM, IDX_DIM, IDX_HEADS)


def _split_points():
    cuts, acc = [], 0
    for s in _in_sizes()[:-1]:
        acc += s
        cuts.append(acc)
    return cuts


def rmsnorm(x, g):
    xf = x.astype(jnp.float32)
    y = xf * lax.rsqrt(jnp.mean(xf * xf, axis=-1, keepdims=True) + EPS)
    return (y * g.astype(jnp.float32)).astype(x.dtype)


def rope(x, pos):
    half = x.shape[-1] // 2
    inv = ROPE_THETA ** (-jnp.arange(half, dtype=jnp.float32) / half)
    ang = pos.astype(jnp.float32)[:, None] * inv[None, :]
    cos = jnp.cos(ang)[:, None, :]
    sin = jnp.sin(ang)[:, None, :]
    xf = x.astype(jnp.float32)
    x1, x2 = xf[..., :half], xf[..., half:]
    return jnp.concatenate([x1 * cos - x2 * sin, x1 * sin + x2 * cos], axis=-1).astype(x.dtype)


def short_conv(a_in, a_b, a_c, w, prev):
    T = a_in.shape[1]
    y = a_c * a_in
    ypad = jnp.concatenate([prev, y], axis=1)
    z = w[0] * ypad[:, 0:T]
    for j in range(1, CONV_W):
        z = z + w[j] * ypad[:, j:j + T]
    return a_b * z, ypad[:, -(CONV_W - 1):]


def multiscale_pool(xb, prev, pos, w_pool, scale):
    B, T, W = xb.shape
    xx = jnp.concatenate([prev, xb], axis=1)
    cs = jnp.concatenate([jnp.zeros((B, 1, W), jnp.float32), jnp.cumsum(xx.astype(jnp.float32), axis=1)], axis=1)
    means = []
    for g, win in enumerate(POOL_WINDOWS):
        lo, hi = g * POOL_CH, (g + 1) * POOL_CH
        s = cs[:, POOL_BUF + 1:POOL_BUF + 1 + T, lo:hi] - cs[:, POOL_BUF + 1 - win:POOL_BUF + 1 - win + T, lo:hi]
        cnt = jnp.minimum(pos + 1, win).astype(jnp.float32)[None, :, None]
        means.append(s / cnt)
    mean = jnp.stack(means, axis=2)
    d = (mean - xb.reshape(B, T, POOL_GROUPS, POOL_CH).astype(jnp.float32)).astype(xb.dtype)
    y = jnp.einsum('btgc,gcd->btgd', d, w_pool).reshape(B, T, W) * scale
    return y, xx[:, -POOL_BUF:]


def chunk_mix(u, v, ws, bias):
    B, T, W = u.shape
    L = min(T, CHUNK)
    n = T // L
    mask = jnp.tril(jnp.ones((L, L), dtype=bool))
    wm = jnp.where(mask[None], ws[:, :L, :L], 0)
    vc = v.reshape(B, n, L, CHUNK_HEADS, CHUNK_CH)
    s = jnp.einsum('gij,bnjgc->bnigc', wm, vc) + bias[:, :L].T[None, None, :, :, None]
    return u * s.reshape(B, T, W)


def dsa_block(q, qi, wi, q_pos, k, v, ki32, k_pos, topk):
    dots = jnp.einsum('bthd,bsd->bths', qi.astype(jnp.float32), ki32)
    score = jnp.einsum('bth,bths->bts', wi.astype(jnp.float32), jax.nn.relu(dots))
    causal = k_pos[None, :] <= q_pos[:, None]
    score = jnp.where(causal[None], score, -jnp.inf)
    _, idx = lax.top_k(score, topk)
    valid = jnp.take(k_pos, idx) <= q_pos[None, :, None]
    gather = jax.vmap(lambda arr, ix: arr[ix])
    k_sel = gather(k, idx)
    v_sel = gather(v, idx)
    logits = jnp.einsum('bthd,btkhd->bthk', q, k_sel).astype(jnp.float32) * (HEAD_DIM ** -0.5)
    logits = jnp.where(valid[:, :, None, :], logits, -jnp.inf)
    prob = jax.nn.softmax(logits, axis=-1).astype(v.dtype)
    return jnp.einsum('bthk,btkhd->bthd', prob, v_sel)


def dsa_attend(q, qi, wi, q_pos, k, v, ki, k_pos):
    B, T = q.shape[0], q.shape[1]
    topk = min(TOPK_MAX, k.shape[1] // 4)
    nb = max(T // Q_BLOCK, 1)
    blk = T // nb
    ki32 = ki.astype(jnp.float32)

    def to_blocks(a):
        return a.reshape((B, nb, blk) + a.shape[2:]).swapaxes(0, 1)

    def one_block(args):
        qb, qib, wib, pb = args
        return dsa_block(qb, qib, wib, pb, k, v, ki32, k_pos, topk)

    out = lax.map(one_block, (to_blocks(q), to_blocks(qi), to_blocks(wi), q_pos.reshape(nb, blk)))
    return out.swapaxes(0, 1).reshape(B, T, ATT_HEADS * HEAD_DIM)


def mem_attend(h, wq, mk, mv, wo):
    B, T, _ = h.shape
    q = jnp.einsum('btd,de->bte', h, wq).reshape(B, T, MEM_HEADS, MEM_HEAD_DIM)
    logits = jnp.einsum('bthd,bmhd->bhtm', q, mk).astype(jnp.float32) * (MEM_HEAD_DIM ** -0.5)
    prob = jax.nn.softmax(logits, axis=-1).astype(mv.dtype)
    o = jnp.einsum('bhtm,bmhd->bthd', prob, mv).reshape(B, T, MEM_HEADS * MEM_HEAD_DIM)
    return jnp.einsum('bte,ed->btd', o, wo)


def swiglu(x, w1, w3, w2):
    return jnp.matmul(jax.nn.silu(jnp.matmul(x, w1)) * jnp.matmul(x, w3), w2)


def moe_swiglu(x, router, w1, w3, w2):
    n = x.shape[0]
    logits = jnp.einsum('nd,de->ne', x, router).astype(jnp.float32)
    top_logit, top_e = lax.top_k(logits, TOP_K_EXPERTS)
    gate = jax.nn.softmax(top_logit, axis=-1)
    e_flat = top_e.reshape(-1)
    tok_flat = jnp.repeat(jnp.arange(n, dtype=jnp.int32), TOP_K_EXPERTS)
    g_flat = gate.reshape(-1)
    n_assign = e_flat.shape[0]
    order = jnp.argsort(e_flat)
    e_sorted = e_flat[order]
    counts = jnp.bincount(e_flat, length=N_EXPERTS)
    padded = (counts + MOE_BLOCK - 1) // MOE_BLOCK * MOE_BLOCK
    start = jnp.cumsum(counts) - counts
    pend = jnp.cumsum(padded)
    pstart = pend - padded
    slot = pstart[e_sorted] + (jnp.arange(n_assign) - start[e_sorted])
    n_blocks = -(-n_assign // MOE_BLOCK) + N_EXPERTS
    n_slots = n_blocks * MOE_BLOCK
    slot_tok = jnp.zeros((n_slots,), jnp.int32).at[slot].set(tok_flat[order])
    slot_gate = jnp.zeros((n_slots,), jnp.float32).at[slot].set(g_flat[order])
    block_e = jnp.minimum(jnp.searchsorted(pend, jnp.arange(n_blocks) * MOE_BLOCK, side='right'), N_EXPERTS - 1)

    def one_block(args):
        tok, g, e = args
        yb = swiglu(x[tok], w1[e], w3[e], w2[e])
        return yb * g[:, None].astype(yb.dtype)

    y = lax.map(one_block, (slot_tok.reshape(n_blocks, MOE_BLOCK), slot_gate.reshape(n_blocks, MOE_BLOCK), block_e))
    return jnp.zeros_like(x).at[slot_tok].add(y.reshape(n_slots, x.shape[1]))


def run_trunk(x, pos, conv_prev, pool_prev, past_fn, mem_k, mem_v, p):
    B, T, _ = x.shape
    cuts = _split_points()
    h = x
    ks, vs, kis, convs, pools, cvs = [], [], [], [], [], []
    for l in range(DEPTH):
        hn = rmsnorm(h, p['norm_mix'][l])
        proj = jnp.einsum('btd,dp->btp', hn, p['w_in'][l])
        a_in, a_b, a_c, b_in, c_u, c_v, q, k, v, qi, ki, wi = jnp.split(proj, cuts, axis=-1)
        ya, conv_buf = short_conv(a_in, a_b, a_c, p['conv_w'][l], conv_prev[:, l])
        yb, pool_buf = multiscale_pool(b_in, pool_prev[:, l], pos, p['pool_w'][l], p['pool_scale'][l])
        vn = rmsnorm(c_v, p['chunk_norm'][l])
        yc = chunk_mix(c_u, vn, p['chunk_ws'][l], p['chunk_b'][l])
        q = rope(q.reshape(B, T, ATT_HEADS, HEAD_DIM), pos)
        k = rope(k.reshape(B, T, ATT_HEADS, HEAD_DIM), pos)
        v = v.reshape(B, T, ATT_HEADS, HEAD_DIM)
        qi = rope(qi.reshape(B, T, IDX_HEADS, IDX_DIM), pos)
        ki = rope(ki[:, :, None, :], pos)[:, :, 0]
        wi = wi * IDX_W_SCALE
        if past_fn is None:
            k_all, v_all, ki_all, k_pos = k, v, ki, pos
        else:
            pk, pv, pki = past_fn(l)
            k_all = jnp.concatenate([pk, k], axis=1)
            v_all = jnp.concatenate([pv, v], axis=1)
            ki_all = jnp.concatenate([pki, ki], axis=1)
            k_pos = jnp.concatenate([jnp.arange(pk.shape[1], dtype=jnp.int32), pos])
        yd = dsa_attend(q, qi, wi, pos, k_all, v_all, ki_all, k_pos)
        mix = jnp.concatenate([ya, yb, yc, yd], axis=-1)
        h = h + jnp.einsum('btm,md->btd', mix, p['w_out'][l])
        hn = rmsnorm(h, p['norm_mem'][l])
        h = h + mem_attend(hn, p['wq_mem'][l], mem_k[:, l], mem_v[:, l], p['wo_mem'][l])
        hn = rmsnorm(h, p['norm_ffn'][l])
        j = l // 2
        if l % 2 == 0:
            f = swiglu(hn, p['w1_dense'][j], p['w3_dense'][j], p['w2_dense'][j])
        else:
            f = moe_swiglu(hn.reshape(B * T, D_MODEL), p['router'][j], p['w1_moe'][j], p['w3_moe'][j], p['w2_moe'][j]).reshape(B, T, D_MODEL)
        h = h + f
        ks.append(k)
        vs.append(v)
        kis.append(ki)
        convs.append(conv_buf)
        pools.append(pool_buf)
        cvs.append(vn)
    y = rmsnorm(h, p['norm_final'])
    st = lambda xs: jnp.stack(xs, axis=1)
    return y, st(ks), st(vs), st(kis), st(convs), st(pools), st(cvs)


def setup_inputs(seed: int = 0) -> dict:
    key = jax.random.key(seed)
    keys = iter(jax.random.split(key, 48))
    nrm = lambda shape, scale=1.0: scale * jax.random.normal(next(keys), shape, jnp.float32)
    gain = lambda shape: 1.0 + 0.02 * jax.random.normal(next(keys), shape, jnp.float32)
    n_pages = PAST_LEN // PAGE_SIZE
    n_used = DEC_BATCH * n_pages
    n_phys = n_used + (n_used + 3) // 4
    n_dense = (DEPTH + 1) // 2
    n_moe = DEPTH // 2
    p_in = sum(_in_sizes())
    x_prompt = nrm((BATCH, SEQ, D_MODEL))
    x_sample = nrm((DEC_BATCH, DEC_SEQ, D_MODEL))
    mem_prompt = nrm((BATCH, 256, D_MODEL))
    cache_k = nrm((n_phys, DEPTH, PAGE_SIZE, ATT_HEADS, HEAD_DIM))
    cache_v = nrm((n_phys, DEPTH, PAGE_SIZE, ATT_HEADS, HEAD_DIM))
    cache_kidx = nrm((n_phys, DEPTH, PAGE_SIZE, IDX_DIM))
    cache_mem_k = nrm((DEC_BATCH, DEPTH, 256, MEM_HEADS, MEM_HEAD_DIM))
    cache_mem_v = nrm((DEC_BATCH, DEPTH, 256, MEM_HEADS, MEM_HEAD_DIM))
    state_conv = nrm((DEC_BATCH, DEPTH, CONV_W - 1, W_GROUP))
    state_pool = nrm((DEC_BATCH, DEPTH, POOL_BUF, W_GROUP))
    perm = jax.random.permutation(next(keys), n_phys).astype(jnp.int32)
    page_table = perm[:n_used].reshape(DEC_BATCH, n_pages)
    return {
        'x_prompt': x_prompt,
        'x_sample': x_sample,
        'mem_prompt': mem_prompt,
        'cache_k': cache_k,
        'cache_v': cache_v,
        'cache_kidx': cache_kidx,
        'cache_mem_k': cache_mem_k,
        'cache_mem_v': cache_mem_v,
        'state_conv': state_conv,
        'state_pool': state_pool,
        'page_table': page_table,
        'norm_mix': gain((DEPTH, D_MODEL)),
        'w_in': nrm((DEPTH, D_MODEL, p_in), D_MODEL ** -0.5),
        'conv_w': nrm((DEPTH, CONV_W, W_GROUP), CONV_W ** -0.5),
        'pool_w': nrm((DEPTH, POOL_GROUPS, POOL_CH, POOL_CH), POOL_CH ** -0.5),
        'pool_scale': gain((DEPTH, W_GROUP)),
        'chunk_norm': gain((DEPTH, W_GROUP)),
        'chunk_ws': nrm((DEPTH, CHUNK_HEADS, CHUNK, CHUNK), CHUNK ** -0.5),
        'chunk_b': gain((DEPTH, CHUNK_HEADS, CHUNK)),
        'w_out': nrm((DEPTH, D_MIX, D_MODEL), D_MIX ** -0.5),
        'norm_mem': gain((DEPTH, D_MODEL)),
        'wq_mem': nrm((DEPTH, D_MODEL, MEM_HEADS * MEM_HEAD_DIM), D_MODEL ** -0.5),
        'wk_mem': nrm((DEPTH, D_MODEL, MEM_HEADS * MEM_HEAD_DIM), D_MODEL ** -0.5),
        'wv_mem': nrm((DEPTH, D_MODEL, MEM_HEADS * MEM_HEAD_DIM), D_MODEL ** -0.5),
        'wo_mem': nrm((DEPTH, MEM_HEADS * MEM_HEAD_DIM, D_MODEL), (MEM_HEADS * MEM_HEAD_DIM) ** -0.5),
        'norm_ffn': gain((DEPTH, D_MODEL)),
        'w1_dense': nrm((n_dense, D_MODEL, D_FF), D_MODEL ** -0.5),
        'w3_dense': nrm((n_dense, D_MODEL, D_FF), D_MODEL ** -0.5),
        'w2_dense': nrm((n_dense, D_FF, D_MODEL), D_FF ** -0.5),
        'router': nrm((n_moe, D_MODEL, N_EXPERTS), D_MODEL ** -0.5),
        'w1_moe': nrm((n_moe, N_EXPERTS, D_MODEL, D_FF), D_MODEL ** -0.5),
        'w3_moe': nrm((n_moe, N_EXPERTS, D_MODEL, D_FF), D_MODEL ** -0.5),
        'w2_moe': nrm((n_moe, N_EXPERTS, D_FF, D_MODEL), D_FF ** -0.5),
        'norm_final': gain((D_MODEL,)),
    }


def reference(x_prompt, x_sample, mem_prompt, cache_k, cache_v, cache_kidx, cache_mem_k, cache_mem_v,
              state_conv, state_pool, page_table, norm_mix, w_in, conv_w, pool_w, pool_scale, chunk_norm,
              chunk_ws, chunk_b, w_out, norm_mem, wq_mem, wk_mem, wv_mem, wo_mem, norm_ffn, w1_dense,
              w3_dense, w2_dense, router, w1_moe, w3_moe, w2_moe, norm_final):
    p = {
        'norm_mix': norm_mix, 'w_in': w_in, 'conv_w': conv_w, 'pool_w': pool_w,
        'pool_scale': pool_scale, 'chunk_norm': chunk_norm, 'chunk_ws': chunk_ws, 'chunk_b': chunk_b,
        'w_out': w_out, 'norm_mem': norm_mem, 'wq_mem': wq_mem, 'wo_mem': wo_mem, 'norm_ffn': norm_ffn,
        'w1_dense': w1_dense, 'w3_dense': w3_dense, 'w2_dense': w2_dense, 'router': router,
        'w1_moe': w1_moe, 'w3_moe': w3_moe, 'w2_moe': w2_moe, 'norm_final': norm_final,
    }
    Bp, Tp = x_prompt.shape[0], x_prompt.shape[1]
    n_mem = mem_prompt.shape[1]
    mem_k_p = jnp.einsum('bmd,lde->blme', mem_prompt, wk_mem).reshape(Bp, DEPTH, n_mem, MEM_HEADS, MEM_HEAD_DIM)
    mem_v_p = jnp.einsum('bmd,lde->blme', mem_prompt, wv_mem).reshape(Bp, DEPTH, n_mem, MEM_HEADS, MEM_HEAD_DIM)
    pos_p = jnp.arange(Tp, dtype=jnp.int32)
    conv0 = jnp.zeros((Bp, DEPTH, CONV_W - 1, W_GROUP), x_prompt.dtype)
    pool0 = jnp.zeros((Bp, DEPTH, POOL_BUF, W_GROUP), x_prompt.dtype)
    y_p, k_p, v_p, ki_p, conv_p, pool_p, _ = run_trunk(x_prompt, pos_p, conv0, pool0, None, mem_k_p, mem_v_p, p)
    Bs, Ts = x_sample.shape[0], x_sample.shape[1]
    past_len = page_table.shape[1] * cache_k.shape[2]

    def past_fn(l):
        pk = cache_k[page_table, l].reshape(Bs, past_len, ATT_HEADS, HEAD_DIM)
        pv = cache_v[page_table, l].reshape(Bs, past_len, ATT_HEADS, HEAD_DIM)
        pki = cache_kidx[page_table, l].reshape(Bs, past_len, IDX_DIM)
        return pk, pv, pki

    pos_s = past_len + jnp.arange(Ts, dtype=jnp.int32)
    y_s, k_s, v_s, ki_s, conv_s, pool_s, cv_s = run_trunk(x_sample, pos_s, state_conv, state_pool, past_fn,
                                                          cache_mem_k, cache_mem_v, p)
    return (y_p, y_s, k_p, v_p, ki_p, mem_k_p, mem_v_p, conv_p, pool_p, k_s, v_s, ki_s, conv_s, pool_s, cv_s)
```

```python
import functools
import math

import jax
import jax.numpy as jnp
from jax import lax
from jax.experimental import pallas as pl
from jax.experimental.pallas import tpu as pltpu

F32 = jnp.float32
BF16 = jnp.bfloat16
I32 = jnp.int32

D_MODEL = 1024
W_GROUP = 256
CONV_W = 3
POOL_WINDOWS = (2, 4, 8, 16)
POOL_CH = 64
POOL_BUF = 15
CHUNK = 128
CHUNK_HEADS = 4
ATT_HEADS = 4
HEAD_DIM = 64
IDX_HEADS = 8
IDX_DIM = 64
IDX_W_SCALE = (IDX_HEADS ** -0.5) * (IDX_DIM ** -0.5)
TOPK_MAX = 256
ROPE_THETA = 10000.0
MEM_HEADS = 4
MEM_HEAD_DIM = 256
D_FF = 3584
N_EXPERTS = 8
EPS = 1e-6

LANES = 128
PREV_ROWS = 16
P_PAD = 3072
HALF = P_PAD // 2
OFF_Q, OFF_K, OFF_QI, OFF_V, OFF_KI, OFF_WI = 1536, 1792, 2048, 2560, 2816, 2944
VMEM_LIMIT = 56 * 1024 * 1024

INT_MIN = -2 ** 31


def _cparams(sem):
    return pltpu.CompilerParams(dimension_semantics=sem, vmem_limit_bytes=VMEM_LIMIT)


def _rms(x, g):
    ms = jnp.mean(x * x, axis=-1, keepdims=True)
    return x * lax.rsqrt(ms + EPS) * g


def _dot(a, b):
    return jnp.dot(a, b, preferred_element_type=F32)


def _dot_nt(a, b):
    return lax.dot_general(a, b, (((1,), (1,)), ((), ())), preferred_element_type=F32)


def _norm_matmul_kernel(x_ref, g_ref, w_ref, o_ref, xn_ref, *, norm):
    @pl.when(pl.program_id(1) == 0)
    def _():
        x = x_ref[...]
        if norm:
            x = _rms(x, g_ref[...])
        xn_ref[...] = x.astype(BF16)

    o_ref[...] = _dot(xn_ref[...], w_ref[...]).astype(o_ref.dtype)


def norm_matmul(x, g, w, *, norm=True, out_dtype=F32, tm=512, tn=1024):
    m, k = x.shape
    n = w.shape[1]
    tm = min(tm, m)
    tn = min(tn, n)
    return pl.pallas_call(
        functools.partial(_norm_matmul_kernel, norm=norm),
        grid=(m // tm, n // tn),
        in_specs=[pl.BlockSpec((tm, k), lambda i, j: (i, 0)),
                  pl.BlockSpec((1, k), lambda i, j: (0, 0)),
                  pl.BlockSpec((k, tn), lambda i, j: (0, j))],
        out_specs=pl.BlockSpec((tm, tn), lambda i, j: (i, j)),
        out_shape=jax.ShapeDtypeStruct((m, n), out_dtype),
        scratch_shapes=[pltpu.VMEM((tm, k), BF16)],
        compiler_params=_cparams(("parallel", "arbitrary")),
        name="norm_matmul",
    )(x, g.reshape(1, k), w)


def _matmul_res_kernel(*refs, n_in):
    a_refs, w_refs = refs[:n_in], refs[n_in:2 * n_in]
    r_ref, o_ref = refs[2 * n_in], refs[2 * n_in + 1]
    acc = r_ref[...]
    for a, w in zip(a_refs, w_refs):
        acc = acc + _dot(a[...].astype(BF16), w[...])
    o_ref[...] = acc


def matmul_res(a_list, w_list, res, *, tm=512):
    m, n = res.shape
    tm = min(tm, m)
    n_in = len(a_list)
    in_specs = [pl.BlockSpec((tm, a.shape[1]), lambda i: (i, 0)) for a in a_list]
    in_specs += [pl.BlockSpec(w.shape, lambda i: (0, 0)) for w in w_list]
    in_specs += [pl.BlockSpec((tm, n), lambda i: (i, 0))]
    return pl.pallas_call(
        functools.partial(_matmul_res_kernel, n_in=n_in),
        grid=(m // tm,),
        in_specs=in_specs,
        out_specs=pl.BlockSpec((tm, n), lambda i: (i, 0)),
        out_shape=jax.ShapeDtypeStruct((m, n), F32),
        compiler_params=_cparams(("parallel",)),
        name="matmul_res",
    )(*a_list, *w_list, res)


def _ffn_kernel(x_ref, g_ref, w1_ref, w3_ref, w2_ref, gf_ref, o_ref, xn_ref, acc_ref, *, final_norm):
    f = pl.program_id(1)

    @pl.when(f == 0)
    def _():
        xn_ref[...] = _rms(x_ref[...], g_ref[...]).astype(BF16)
        acc_ref[...] = jnp.zeros_like(acc_ref)

    xn = xn_ref[...]
    a = _dot(xn, w1_ref[...])
    b = _dot(xn, w3_ref[...])
    hidden = (a * jax.nn.sigmoid(a) * b).astype(BF16)
    acc_ref[...] += _dot(hidden, w2_ref[...])

    @pl.when(f == pl.num_programs(1) - 1)
    def _():
        y = x_ref[...] + acc_ref[...]
        if final_norm:
            y = _rms(y, gf_ref[...])
        o_ref[...] = y


def ffn_dense(x, g, w1, w3, w2, g_final, *, final_norm, tm=512, tf=512):
    m, d = x.shape
    ff = w1.shape[1]
    tm = min(tm, m)
    return pl.pallas_call(
        functools.partial(_ffn_kernel, final_norm=final_norm),
        grid=(m // tm, ff // tf),
        in_specs=[pl.BlockSpec((tm, d), lambda i, f: (i, 0)),
                  pl.BlockSpec((1, d), lambda i, f: (0, 0)),
                  pl.BlockSpec((d, tf), lambda i, f: (0, f)),
                  pl.BlockSpec((d, tf), lambda i, f: (0, f)),
                  pl.BlockSpec((tf, d), lambda i, f: (f, 0)),
                  pl.BlockSpec((1, d), lambda i, f: (0, 0))],
        out_specs=pl.BlockSpec((tm, d), lambda i, f: (i, 0)),
        out_shape=jax.ShapeDtypeStruct((m, d), F32),
        scratch_shapes=[pltpu.VMEM((tm, d), BF16), pltpu.VMEM((tm, d), F32)],
        compiler_params=_cparams(("parallel", "arbitrary")),
        name="ffn_dense",
    )(x, g.reshape(1, d), w1, w3, w2, g_final.reshape(1, d))


def _router_kernel(x_ref, g_ref, r_ref, hn_ref, e_ref, gate_ref):
    hn = _rms(x_ref[...], g_ref[...])
    hn_ref[...] = hn.astype(BF16)
    logits = jnp.dot(hn, r_ref[...], preferred_element_type=F32, precision=lax.Precision.HIGHEST)
    lane = lax.broadcasted_iota(I32, logits.shape, 1)
    neg = jnp.float32(-jnp.inf)
    l1 = jnp.where(lane < N_EXPERTS, logits, neg)
    m1 = jnp.max(l1, axis=1, keepdims=True)
    i1 = jnp.min(jnp.where(l1 == m1, lane, LANES), axis=1, keepdims=True)
    l2 = jnp.where(lane == i1, neg, l1)
    m2 = jnp.max(l2, axis=1, keepdims=True)
    i2 = jnp.min(jnp.where(l2 == m2, lane, LANES), axis=1, keepdims=True)
    t = jnp.exp(m2 - m1)
    denom = 1.0 + t
    e_ref[...] = jnp.where(lane == 0, i1, jnp.where(lane == 1, i2, 0))
    gate_ref[...] = jnp.where(lane == 0, 1.0 / denom, jnp.where(lane == 1, t / denom, 0.0))


def moe_router(x, g, router_pad, *, tm=512):
    m, d = x.shape
    tm = min(tm, m)
    return pl.pallas_call(
        _router_kernel,
        grid=(m // tm,),
        in_specs=[pl.BlockSpec((tm, d), lambda i: (i, 0)),
                  pl.BlockSpec((1, d), lambda i: (0, 0)),
                  pl.BlockSpec((d, LANES), lambda i: (0, 0))],
        out_specs=[pl.BlockSpec((tm, d), lambda i: (i, 0)),
                   pl.BlockSpec((tm, LANES), lambda i: (i, 0)),
                   pl.BlockSpec((tm, LANES), lambda i: (i, 0))],
        out_shape=[jax.ShapeDtypeStruct((m, d), BF16),
                   jax.ShapeDtypeStruct((m, LANES), I32),
                   jax.ShapeDtypeStruct((m, LANES), F32)],
        compiler_params=_cparams(("parallel",)),
        name="moe_router",
    )(x, g.reshape(1, d), router_pad)


def _expert_kernel(be_ref, nb_ref, xs_ref, gate_ref, w1_ref, w3_ref, w2_ref, o_ref, acc_ref):
    j, f = pl.program_id(0), pl.program_id(1)
    live = j < nb_ref[0]

    @pl.when(f == 0)
    def _():
        acc_ref[...] = jnp.zeros_like(acc_ref)

    @pl.when(live)
    def _():
        xs = xs_ref[...]
        a = _dot(xs, w1_ref[0])
        b = _dot(xs, w3_ref[0])
        hidden = (a * jax.nn.sigmoid(a) * b).astype(BF16)
        acc_ref[...] += _dot(hidden, w2_ref[0])

    @pl.when(f == pl.num_programs(1) - 1)
    def _():
        o_ref[...] = acc_ref[...] * gate_ref[...]


def moe_experts(block_e, n_live, xs, slot_gate, w1, w3, w2, *, bm, tf=512):
    n_slots, d = xs.shape
    ff = w1.shape[2]
    n_blocks = n_slots // bm
    grid_spec = pltpu.PrefetchScalarGridSpec(
        num_scalar_prefetch=2,
        grid=(n_blocks, ff // tf),
        in_specs=[pl.BlockSpec((bm, d), lambda j, f, be, nb: (j, 0)),
                  pl.BlockSpec((bm, 1), lambda j, f, be, nb: (j, 0)),
                  pl.BlockSpec((1, d, tf), lambda j, f, be, nb: (be[j], 0, f)),
                  pl.BlockSpec((1, d, tf), lambda j, f, be, nb: (be[j], 0, f)),
                  pl.BlockSpec((1, tf, d), lambda j, f, be, nb: (be[j], f, 0))],
        out_specs=pl.BlockSpec((bm, d), lambda j, f, be, nb: (j, 0)),
        scratch_shapes=[pltpu.VMEM((bm, d), F32)],
    )
    return pl.pallas_call(
        _expert_kernel,
        grid_spec=grid_spec,
        out_shape=jax.ShapeDtypeStruct((n_slots, d), F32),
        compiler_params=_cparams(("parallel", "arbitrary")),
        name="moe_experts",
    )(block_e, n_live, xs, slot_gate, w1, w3, w2)


def _combine_kernel(x_ref, a_ref, b_ref, gf_ref, o_ref, *, final_norm):
    y = x_ref[...] + (a_ref[...] + b_ref[...])
    if final_norm:
        y = _rms(y, gf_ref[...])
    o_ref[...] = y


def moe_combine(x, ya, yb, g_final, *, final_norm, tm=512):
    m, d = x.shape
    tm = min(tm, m)
    spec = pl.BlockSpec((tm, d), lambda i: (i, 0))
    return pl.pallas_call(
        functools.partial(_combine_kernel, final_norm=final_norm),
        grid=(m // tm,),
        in_specs=[spec, spec, spec, pl.BlockSpec((1, d), lambda i: (0, 0))],
        out_specs=spec,
        out_shape=jax.ShapeDtypeStruct((m, d), F32),
        compiler_params=_cparams(("parallel",)),
        name="moe_combine",
    )(x, ya, yb, g_final.reshape(1, d))


def moe_ffn(x, g, router_pad, w1, w3, w2, g_final, *, final_norm):
    n, d = x.shape
    bm = 512 if n >= 4096 else 128
    hn, e12, g12 = moe_router(x, g, router_pad)
    e_flat = e12[:, :2].reshape(-1)
    g_flat = g12[:, :2].reshape(-1)
    n_assign = 2 * n
    order = jnp.argsort(e_flat, stable=True).astype(I32)
    e_sorted = e_flat[order]
    counts = jnp.sum(e_flat[:, None] == jnp.arange(N_EXPERTS, dtype=I32)[None, :], axis=0).astype(I32)
    padded = (counts + bm - 1) // bm * bm
    start = jnp.cumsum(counts) - counts
    pend = jnp.cumsum(padded)
    pstart = pend - padded
    slot = pstart[e_sorted] + (jnp.arange(n_assign, dtype=I32) - start[e_sorted])
    n_blocks = -(-n_assign // bm) + N_EXPERTS
    n_slots = n_blocks * bm
    slot_tok = jnp.zeros((n_slots,), I32).at[slot].set(order // 2)
    slot_gate = jnp.zeros((n_slots,), F32).at[slot].set(g_flat[order])
    assign_slot = jnp.zeros((n_assign,), I32).at[order].set(slot)
    block_e = jnp.minimum(
        jnp.searchsorted(pend, jnp.arange(n_blocks, dtype=I32) * bm, side='right'), N_EXPERTS - 1).astype(I32)
    n_live = (pend[-1] // bm).astype(I32).reshape(1)
    xs = hn[slot_tok]
    ys = moe_experts(block_e, n_live, xs, slot_gate.reshape(n_slots, 1), w1, w3, w2, bm=bm)
    pair = assign_slot.reshape(n, 2)
    return moe_combine(x, ys[pair[:, 0]], ys[pair[:, 1]], g_final, final_norm=final_norm)


def _swap_half_heads(x):
    n = x.shape[-1]
    lane = lax.broadcasted_iota(I32, x.shape, x.ndim - 1)
    fwd = pltpu.roll(x, n - HEAD_DIM // 2, x.ndim - 1)
    bwd = pltpu.roll(x, HEAD_DIM // 2, x.ndim - 1)
    return jnp.where(lane % HEAD_DIM < HEAD_DIM // 2, fwd, bwd)


def _rope(x, cos, sin):
    reps = x.shape[-1] // LANES
    if reps > 1:
        cos = jnp.concatenate([cos] * reps, axis=-1)
        sin = jnp.concatenate([sin] * reps, axis=-1)
    return x * cos + _swap_half_heads(x) * sin


def _mixer_kernel(*refs, tt, chunk, pos0, from_state, emit_cv):
    it = iter(refs)
    xa_ref, xb_ref = next(it), next(it)
    if from_state:
        convp_ref, poolp_ref = next(it), next(it)
    else:
        prev_ref = next(it)
    cos_ref, sin_ref = next(it), next(it)
    convw_ref, poolw_ref, pscale_ref, cnorm_ref, wm_ref, cbias_ref = (next(it) for _ in range(6))
    mix_ref, q_ref, k_ref, kb_ref, v_ref, vb_ref, qi_ref, ki_ref, ki2_ref, wi_ref, convs_ref, pools_ref = (
        next(it) for _ in range(12))
    cv_ref = next(it) if emit_cv else None
    ybuf, xbuf = next(it), next(it)

    t = pl.program_id(1)
    xa = xa_ref[0]
    a_in, a_b, a_c = xa[:, 0:256], xa[:, 256:512], xa[:, 512:768]
    b_in, c_u, c_v = xa[:, 768:1024], xa[:, 1024:1280], xa[:, 1280:1536]

    y = a_c * a_in
    if from_state:
        ybuf[PREV_ROWS - 2:PREV_ROWS, :] = convp_ref[0]
        xbuf[0:1, :] = jnp.zeros((1, W_GROUP), F32)
        xbuf[1:PREV_ROWS, :] = poolp_ref[0]
    else:
        first = (t == 0).astype(F32)
        prev = prev_ref[0] * (1.0 - first)
        ybuf[0:PREV_ROWS, :] = prev[:, 512:768] * prev[:, 0:256]
        xbuf[0:PREV_ROWS, :] = prev[:, 768:1024]
    ybuf[PREV_ROWS:PREV_ROWS + tt, :] = y
    xbuf[PREV_ROWS:PREV_ROWS + tt, :] = b_in

    cw = convw_ref[...]
    z = cw[0:1, :] * ybuf[PREV_ROWS - 2:PREV_ROWS - 2 + tt, :]
    z = z + cw[1:2, :] * ybuf[PREV_ROWS - 1:PREV_ROWS - 1 + tt, :]
    z = z + cw[2:3, :] * y
    ya = a_b * z
    convs_ref[0] = ybuf[PREV_ROWS + tt - 2:PREV_ROWS + tt, :]

    lane = lax.broadcasted_iota(I32, (tt, W_GROUP), 1)
    row = lax.broadcasted_iota(I32, (tt, W_GROUP), 0)
    pos = (pos0 + t * tt + row).astype(F32)
    run = b_in
    mean = jnp.zeros((tt, W_GROUP), F32)
    for j in range(1, POOL_WINDOWS[-1] + 1):
        if j > 1:
            run = run + xbuf[PREV_ROWS - (j - 1):PREV_ROWS - (j - 1) + tt, :]
        if j in POOL_WINDOWS:
            grp = POOL_WINDOWS.index(j)
            cnt = jnp.minimum(pos + 1.0, float(j))
            mean = jnp.where(lane // POOL_CH == grp, run / cnt, mean)
    dlt = (mean - b_in).astype(BF16)
    yb = _dot(dlt, poolw_ref[...]) * pscale_ref[...]
    pools_ref[0] = xbuf[PREV_ROWS + tt - POOL_BUF:PREV_ROWS + tt, :]

    vn = _rms(c_v, cnorm_ref[...])
    if emit_cv:
        cv_ref[0] = vn
    vnb = vn.astype(BF16)
    lane_c = lax.broadcasted_iota(I32, (chunk, W_GROUP), 1)
    parts = []
    for c in range(tt // chunk):
        vc = vnb[c * chunk:(c + 1) * chunk, :]
        s = cbias_ref[...]
        for hd in range(CHUNK_HEADS):
            sh = _dot(wm_ref[hd], vc)
            s = s + jnp.where(lane_c // (W_GROUP // CHUNK_HEADS) == hd, sh, 0.0)
        parts.append(s)
    s_all = parts[0] if len(parts) == 1 else jnp.concatenate(parts, axis=0)
    yc = c_u * s_all

    mix_ref[0] = jnp.concatenate([ya, yb, yc], axis=-1).astype(BF16)

    xb = xb_ref[0]
    cos, sin = cos_ref[...], sin_ref[...]
    q = _rope(xb[:, 0:256], cos, sin)
    k = _rope(xb[:, 256:512], cos, sin)
    qi = _rope(xb[:, 512:1024], cos, sin)
    v = xb[:, 1024:1280]
    ki = _rope(xb[:, 1280:1408], cos, sin)
    wi = xb[:, 1408:1536] * IDX_W_SCALE
    q_ref[0] = (q * (HEAD_DIM ** -0.5)).astype(BF16)
    k_ref[0] = k
    kb_ref[0] = k.astype(BF16)
    v_ref[0] = v
    vb_ref[0] = v.astype(BF16)
    qi_ref[0] = qi.astype(BF16)
    ki_ref[0] = ki
    lane128 = lax.broadcasted_iota(I32, ki.shape, 1)
    ki2_ref[0] = jnp.where(lane128 < IDX_DIM, ki, pltpu.roll(ki, IDX_DIM, 1)).astype(BF16)
    wi_ref[0] = wi


def mixers(proj, conv_prev, pool_prev, cos, sin, lw, *, pos0, from_state, emit_cv):
    b, t_len, _ = proj.shape
    tt = min(t_len, 256)
    chunk = min(t_len, CHUNK)
    nt = t_len // tt
    in_specs = [pl.BlockSpec((1, tt, HALF), lambda i, t: (i, t, 0)),
                pl.BlockSpec((1, tt, HALF), lambda i, t: (i, t, 1))]
    args = [proj, proj]
    if from_state:
        in_specs += [pl.BlockSpec((1, CONV_W - 1, W_GROUP), lambda i, t: (i, 0, 0)),
                     pl.BlockSpec((1, POOL_BUF, W_GROUP), lambda i, t: (i, 0, 0))]
        args += [conv_prev, pool_prev]
    else:
        per = tt // PREV_ROWS
        in_specs += [pl.BlockSpec((1, PREV_ROWS, HALF), lambda i, t: (i, jnp.maximum(t * per - 1, 0), 0))]
        args += [proj]
    in_specs += [pl.BlockSpec((tt, LANES), lambda i, t: (t, 0)),
                 pl.BlockSpec((tt, LANES), lambda i, t: (t, 0))]
    args += [cos, sin]
    consts = [lw['conv_w'], lw['pool_w_bd'], lw['pool_scale'], lw['chunk_norm'], lw['chunk_wm'], lw['chunk_bias']]
    for c in consts:
        in_specs.append(pl.BlockSpec(c.shape, lambda i, t, nd=c.ndim: (0,) * nd))
    args += consts

    def tok(width, dtype):
        return (pl.BlockSpec((1, tt, width), lambda i, t: (i, t, 0)), jax.ShapeDtypeStruct((b, t_len, width), dtype))

    outs = [tok(768, BF16),
            tok(256, BF16),
            tok(256, F32), tok(256, BF16),
            tok(256, F32), tok(256, BF16),
            tok(512, BF16),
            tok(LANES, F32),
            tok(LANES, BF16),
            tok(LANES, F32),
            (pl.BlockSpec((1, CONV_W - 1, W_GROUP), lambda i, t: (i, 0, 0)),
             jax.ShapeDtypeStruct((b, CONV_W - 1, W_GROUP), F32)),
            (pl.BlockSpec((1, POOL_BUF, W_GROUP), lambda i, t: (i, 0, 0)),
             jax.ShapeDtypeStruct((b, POOL_BUF, W_GROUP), F32))]
    if emit_cv:
        outs.append(tok(256, F32))
    return pl.pallas_call(
        functools.partial(_mixer_kernel, tt=tt, chunk=chunk, pos0=pos0, from_state=from_state, emit_cv=emit_cv),
        grid=(b, nt),
        in_specs=in_specs,
        out_specs=[o[0] for o in outs],
        out_shape=[o[1] for o in outs],
        scratch_shapes=[pltpu.VMEM((PREV_ROWS + tt, W_GROUP), F32), pltpu.VMEM((PREV_ROWS + tt, W_GROUP), F32)],
        compiler_params=_cparams(("parallel", "arbitrary")),
        name="mixers",
    )(*args)


def _ordered_key(x):
    x = jnp.where(x == 0.0, 0.0, x)
    bits = pltpu.bitcast(x, I32)
    return bits ^ ((bits >> 31) & jnp.int32(0x7FFFFFFF))


def _dsa_kernel(q_ref, qi_ref, wi_ref, k_ref, v_ref, ki2_ref, o_ref, key_ref, *, tq, kc, nc, q_off, topk):
    qb = pl.program_id(1)
    q_pos = q_off + qb * tq + lax.broadcasted_iota(I32, (tq, kc), 0)
    col0 = lax.broadcasted_iota(I32, (tq, kc), 1)
    lane = lax.broadcasted_iota(I32, (tq, LANES), 1)
    neg_inf = jnp.float32(-jnp.inf)

    qi = qi_ref[0]
    wi = wi_ref[0]
    zero_b = jnp.zeros((tq, LANES), BF16)
    qm = []
    for h in range(IDX_HEADS):
        tile = qi[:, (h // 2) * LANES:(h // 2 + 1) * LANES]
        qm.append(jnp.where((lane // IDX_DIM) == (h % 2), tile, zero_b))

    def score_chunk(c, carry):
        ki2 = ki2_ref[0, pl.ds(pl.multiple_of(c * kc, kc), kc), :]
        acc = jnp.zeros((tq, kc), F32)
        for h in range(IDX_HEADS):
            d = _dot_nt(qm[h], ki2)
            acc = acc + wi[:, h:h + 1] * jnp.maximum(d, 0.0)
        score = jnp.where(c * kc + col0 <= q_pos, acc, neg_inf)
        key_ref[c] = _ordered_key(score)
        return carry

    lax.fori_loop(0, nc, score_chunk, 0)

    def count_ge(cand):
        cnt = jnp.zeros((tq, 1), F32)
        for c in range(nc):
            cnt = cnt + jnp.sum(jnp.where(key_ref[c] >= cand, 1.0, 0.0), axis=1, keepdims=True)
        return cnt

    thr0 = jnp.where(count_ge(jnp.zeros((tq, 1), I32)) >= topk, 0, INT_MIN).astype(I32)

    def thr_bit(i, thr):
        cand = thr | (jnp.int32(1) << (30 - i))
        return jnp.where(count_ge(cand) >= topk, cand, thr)

    thr = lax.fori_loop(0, 31, thr_bit, thr0)

    cnt_gt = jnp.zeros((tq, 1), F32)
    for c in range(nc):
        cnt_gt = cnt_gt + jnp.sum(jnp.where(key_ref[c] > thr, 1.0, 0.0), axis=1, keepdims=True)
    need = topk - cnt_gt

    def count_eq_below(bound):
        cnt = jnp.zeros((tq, 1), F32)
        for c in range(nc):
            hit = (key_ref[c] == thr) & (c * kc + col0 < bound)
            cnt = cnt + jnp.sum(jnp.where(hit, 1.0, 0.0), axis=1, keepdims=True)
        return cnt

    idx_bits = max(1, (nc * kc).bit_length())

    def bound_bit(i, bound):
        cand = bound | (jnp.int32(1) << (idx_bits - 1 - i))
        return jnp.where(count_eq_below(cand) <= need, cand, bound)

    bound = lax.fori_loop(0, idx_bits, bound_bit, jnp.zeros((tq, 1), I32))

    q = q_ref[0]
    qh = []
    for h in range(ATT_HEADS):
        tile = q[:, (h // 2) * LANES:(h // 2 + 1) * LANES]
        qh.append(jnp.where((lane // HEAD_DIM) == (h % 2), tile, zero_b))
    lane_o = lax.broadcasted_iota(I32, (tq, W_GROUP), 1)

    def attend_chunk(c, carry):
        m_run, l_run, acc = carry
        off = pl.multiple_of(c * kc, kc)
        key = key_ref[c]
        col = c * kc + col0
        sel = ((key > thr) | ((key == thr) & (col < bound))) & (col <= q_pos)
        vb = v_ref[0, pl.ds(off, kc), :]
        m_new, l_new, scale_full, add_full = [], [], jnp.zeros((tq, W_GROUP), F32), jnp.zeros((tq, W_GROUP), F32)
        for h in range(ATT_HEADS):
            kh = k_ref[0, pl.ds(off, kc), (h // 2) * LANES:(h // 2 + 1) * LANES]
            logit = jnp.where(sel, _dot_nt(qh[h], kh), neg_inf)
            m_h = jnp.maximum(m_run[h], jnp.max(logit, axis=1, keepdims=True))
            m_safe = jnp.where(m_h == neg_inf, 0.0, m_h)
            p = jnp.exp(logit - m_safe)
            alpha = jnp.exp(m_run[h] - m_safe)
            l_new.append(alpha * l_run[h] + jnp.sum(p, axis=1, keepdims=True))
            m_new.append(m_h)
            pv = _dot(p.astype(BF16), vb)
            in_head = (lane_o // HEAD_DIM) == h
            scale_full = jnp.where(in_head, alpha, scale_full)
            add_full = jnp.where(in_head, pv, add_full)
        return tuple(m_new), tuple(l_new), acc * scale_full + add_full

    init = (tuple(jnp.full((tq, 1), neg_inf, F32) for _ in range(ATT_HEADS)),
            tuple(jnp.zeros((tq, 1), F32) for _ in range(ATT_HEADS)),
            jnp.zeros((tq, W_GROUP), F32))
    _, l_fin, acc = lax.fori_loop(0, nc, attend_chunk, init)
    inv = jnp.zeros((tq, W_GROUP), F32)
    for h in range(ATT_HEADS):
        inv = jnp.where((lane_o // HEAD_DIM) == h, 1.0 / l_fin[h], inv)
    o_ref[0] = (acc * inv).astype(o_ref.dtype)


def dsa_attention(q, qi, wi, k, v, ki2, *, q_off, tq, kc):
    b, t_len, _ = q.shape
    s_len = k.shape[1]
    nc = s_len // kc
    topk = min(TOPK_MAX, (q_off + t_len) // 4)
    return pl.pallas_call(
        functools.partial(_dsa_kernel, tq=tq, kc=kc, nc=nc, q_off=q_off, topk=topk),
        grid=(b, t_len // tq),
        in_specs=[pl.BlockSpec((1, tq, W_GROUP), lambda i, j: (i, j, 0)),
                  pl.BlockSpec((1, tq, 2 * W_GROUP), lambda i, j: (i, j, 0)),
                  pl.BlockSpec((1, tq, LANES), lambda i, j: (i, j, 0)),
                  pl.BlockSpec((1, s_len, W_GROUP), lambda i, j: (i, 0, 0)),
                  pl.BlockSpec((1, s_len, W_GROUP), lambda i, j: (i, 0, 0)),
                  pl.BlockSpec((1, s_len, LANES), lambda i, j: (i, 0, 0))],
        out_specs=pl.BlockSpec((1, tq, W_GROUP), lambda i, j: (i, j, 0)),
        out_shape=jax.ShapeDtypeStruct((b, t_len, W_GROUP), BF16),
        scratch_shapes=[pltpu.VMEM((nc, tq, kc), I32)],
        compiler_params=_cparams(("parallel", "arbitrary")),
        name="dsa_attention",
    )(q, qi, wi, k, v, ki2)


def _mem_attn_kernel(q_ref, mk_ref, mv_ref, o_ref):
    q = q_ref[0]
    mk = mk_ref[0].astype(BF16)
    mv = mv_ref[0].astype(BF16)
    outs = []
    for h in range(MEM_HEADS):
        sl = slice(h * MEM_HEAD_DIM, (h + 1) * MEM_HEAD_DIM)
        logit = _dot_nt(q[:, sl], mk[:, sl]) * (MEM_HEAD_DIM ** -0.5)
        m = jnp.max(logit, axis=1, keepdims=True)
        p = jnp.exp(logit - m)
        p = p / jnp.sum(p, axis=1, keepdims=True)
        outs.append(_dot(p.astype(BF16), mv[:, sl]))
    o_ref[0] = jnp.concatenate(outs, axis=-1).astype(o_ref.dtype)


def mem_attention(q, mk, mv, *, tq=512):
    b, t_len, d = q.shape
    m_len = mk.shape[1]
    tq = min(tq, t_len)
    return pl.pallas_call(
        _mem_attn_kernel,
        grid=(b, t_len // tq),
        in_specs=[pl.BlockSpec((1, tq, d), lambda i, j: (i, j, 0)),
                  pl.BlockSpec((1, m_len, d), lambda i, j: (i, 0, 0)),
                  pl.BlockSpec((1, m_len, d), lambda i, j: (i, 0, 0))],
        out_specs=pl.BlockSpec((1, tq, d), lambda i, j: (i, j, 0)),
        out_shape=jax.ShapeDtypeStruct((b, t_len, d), BF16),
        compiler_params=_cparams(("parallel", "arbitrary")),
        name="mem_attention",
    )(q, mk, mv)


def _prep_layer(l, p):
    w = p['w_in'][l]
    cuts = [0]
    for s in (W_GROUP,) * 9 + (IDX_HEADS * IDX_DIM, IDX_DIM, IDX_HEADS):
        cuts.append(cuts[-1] + s)
    piece = lambda i: w[:, cuts[i]:cuts[i + 1]]
    zeros = lambda n: jnp.zeros((D_MODEL, n), w.dtype)
    w_in = jnp.concatenate(
        [piece(i) for i in range(6)] + [piece(6), piece(7), piece(9), piece(8), piece(10), zeros(LANES - IDX_DIM),
                                        piece(11), zeros(LANES - IDX_HEADS)], axis=1).astype(BF16)
    pool_w = p['pool_w'][l]
    pool_bd = jnp.zeros((W_GROUP, W_GROUP), F32)
    for g in range(len(POOL_WINDOWS)):
        pool_bd = pool_bd.at[g * POOL_CH:(g + 1) * POOL_CH, g * POOL_CH:(g + 1) * POOL_CH].set(pool_w[g])
    return {
        'norm_mix': p['norm_mix'][l], 'w_in': w_in,
        'conv_w': p['conv_w'][l], 'pool_w_bd': pool_bd.astype(BF16),
        'pool_scale': p['pool_scale'][l].reshape(1, W_GROUP), 'chunk_norm': p['chunk_norm'][l].reshape(1, W_GROUP),
        'chunk_ws': p['chunk_ws'][l], 'chunk_b': p['chunk_b'][l],
        'w_out_abc': p['w_out'][l][:3 * W_GROUP].astype(BF16), 'w_out_d': p['w_out'][l][3 * W_GROUP:].astype(BF16),
        'norm_mem': p['norm_mem'][l], 'wq_mem': p['wq_mem'][l].astype(BF16), 'wo_mem': p['wo_mem'][l].astype(BF16),
        'norm_ffn': p['norm_ffn'][l],
    }


def _chunk_consts(lw, chunk):
    mask = jnp.tril(jnp.ones((chunk, chunk), dtype=bool))
    wm = jnp.where(mask[None], lw['chunk_ws'][:, :chunk, :chunk], 0).astype(BF16)
    bias = jnp.repeat(lw['chunk_b'][:, :chunk].T, W_GROUP // CHUNK_HEADS, axis=1)
    return wm, bias


def _rope_tables(pos):
    half = HEAD_DIM // 2
    inv = ROPE_THETA ** (-jnp.arange(half, dtype=F32) / half)
    ang = pos.astype(F32)[:, None] * inv[None, :]
    cos, sin = jnp.cos(ang), jnp.sin(ang)
    cos_t = jnp.concatenate([cos, cos] * (LANES // HEAD_DIM), axis=1)
    sin_t = jnp.concatenate([-sin, sin] * (LANES // HEAD_DIM), axis=1)
    return cos_t, sin_t


def _run_trunk(x, pos0, conv_prev, pool_prev, past_fn, mem_k, mem_v, layers, ffn_w, norm_final):
    b, t_len, d = x.shape
    n = b * t_len
    depth = len(layers)
    from_state = past_fn is not None
    pos = pos0 + jnp.arange(t_len, dtype=I32)
    cos_t, sin_t = _rope_tables(pos)
    h = x.reshape(n, d)
    ks, vs, kis, convs, pools, cvs = [], [], [], [], [], []
    for l, lw in enumerate(layers):
        proj = norm_matmul(h, lw['norm_mix'], lw['w_in']).reshape(b, t_len, P_PAD)
        chunk = min(t_len, CHUNK)
        wm, cbias = _chunk_consts(lw, chunk)
        lw_m = dict(lw, chunk_wm=wm, chunk_bias=cbias)
        outs = mixers(proj, conv_prev[:, l] if from_state else None, pool_prev[:, l] if from_state else None,
                      cos_t, sin_t, lw_m, pos0=pos0, from_state=from_state, emit_cv=from_state)
        mix, q, k, kb, v, vb, qi, ki, ki2, wi, conv_s, pool_s = outs[:12]
        if from_state:
            cvs.append(outs[12])
            pk, pv, pki2 = past_fn(l)
            s_real = pk.shape[1] + t_len
            kc = 128
            s_pad = -(-s_real // kc) * kc
            padk = lambda past, new: jnp.concatenate(
                [past, new, jnp.zeros((b, s_pad - s_real, new.shape[2]), new.dtype)], axis=1)
            yd = dsa_attention(q, qi, wi, padk(pk, kb), padk(pv, vb), padk(pki2, ki2),
                               q_off=pk.shape[1], tq=t_len, kc=kc)
        else:
            yd = dsa_attention(q, qi, wi, kb, vb, ki2, q_off=0, tq=min(t_len, 128), kc=min(t_len, 512))
        h = matmul_res([mix.reshape(n, 3 * W_GROUP), yd.reshape(n, W_GROUP)], [lw['w_out_abc'], lw['w_out_d']], h)
        qm = norm_matmul(h, lw['norm_mem'], lw['wq_mem'], out_dtype=BF16).reshape(b, t_len, d)
        om = mem_attention(qm, mem_k[:, l], mem_v[:, l])
        h = matmul_res([om.reshape(n, d)], [lw['wo_mem']], h)
        last = l == depth - 1
        fw = ffn_w[l]
        if fw['kind'] == 'dense':
            h = ffn_dense(h, lw['norm_ffn'], fw['w1'], fw['w3'], fw['w2'], norm_final, final_norm=last)
        else:
            h = moe_ffn(h, lw['norm_ffn'], fw['router'], fw['w1'], fw['w3'], fw['w2'], norm_final, final_norm=last)
        ks.append(k.reshape(b, t_len, ATT_HEADS, HEAD_DIM))
        vs.append(v.reshape(b, t_len, ATT_HEADS, HEAD_DIM))
        kis.append(ki[:, :, :IDX_DIM])
        convs.append(conv_s)
        pools.append(pool_s)
    st = lambda xs: jnp.stack(xs, axis=1)
    y = h.reshape(b, t_len, d)
    return y, st(ks), st(vs), st(kis), st(convs), st(pools), (st(cvs) if cvs else None)


def kernel(x_prompt, x_sample, mem_prompt, cache_k, cache_v, cache_kidx, cache_mem_k, cache_mem_v, state_conv, state_pool, page_table, norm_mix, w_in, conv_w, pool_w, pool_scale, chunk_norm, chunk_ws, chunk_b, w_out, norm_mem, wq_mem, wk_mem, wv_mem, wo_mem, norm_ffn, w1_dense, w3_dense, w2_dense, router, w1_moe, w3_moe, w2_moe, norm_final):
    p = {
        'norm_mix': norm_mix, 'w_in': w_in, 'conv_w': conv_w, 'pool_w': pool_w, 'pool_scale': pool_scale,
        'chunk_norm': chunk_norm, 'chunk_ws': chunk_ws, 'chunk_b': chunk_b, 'w_out': w_out, 'norm_mem': norm_mem,
        'wq_mem': wq_mem, 'wo_mem': wo_mem, 'norm_ffn': norm_ffn,
    }
    depth = norm_mix.shape[0]
    layers = [_prep_layer(l, p) for l in range(depth)]
    ffn_w = []
    for l in range(depth):
        j = l // 2
        if l % 2 == 0:
            ffn_w.append({'kind': 'dense', 'w1': w1_dense[j].astype(BF16), 'w3': w3_dense[j].astype(BF16),
                          'w2': w2_dense[j].astype(BF16)})
        else:
            router_pad = jnp.concatenate([router[j], jnp.zeros((D_MODEL, LANES - N_EXPERTS), F32)], axis=1)
            ffn_w.append({'kind': 'moe', 'router': router_pad, 'w1': w1_moe[j].astype(BF16),
                          'w3': w3_moe[j].astype(BF16), 'w2': w2_moe[j].astype(BF16)})

    bp, tp, d = x_prompt.shape
    n_mem = mem_prompt.shape[1]
    w_kv = jnp.concatenate([wk_mem[l] for l in range(depth)] + [wv_mem[l] for l in range(depth)], axis=1).astype(BF16)
    mem_kv = norm_matmul(mem_prompt.reshape(bp * n_mem, d), jnp.ones((d,), F32), w_kv, norm=False)
    mem_kv = mem_kv.reshape(bp, n_mem, 2, depth, d)
    mem_k_p = jnp.moveaxis(mem_kv[:, :, 0], 2, 1)
    mem_v_p = jnp.moveaxis(mem_kv[:, :, 1], 2, 1)
    y_p, k_p, v_p, ki_p, conv_p, pool_p, _ = _run_trunk(
        x_prompt, 0, None, None, None, mem_k_p, mem_v_p, layers, ffn_w, norm_final)

    bs, ts, _ = x_sample.shape
    n_pages, page = page_table.shape[1], cache_k.shape[2]
    past_len = n_pages * page

    def past_fn(l):
        pk = cache_k[page_table, l].reshape(bs, past_len, W_GROUP).astype(BF16)
        pv = cache_v[page_table, l].reshape(bs, past_len, W_GROUP).astype(BF16)
        pki = cache_kidx[page_table, l].reshape(bs, past_len, IDX_DIM).astype(BF16)
        return pk, pv, jnp.concatenate([pki, pki], axis=-1)

    cmk = cache_mem_k.reshape(bs, depth, n_mem, d)
    cmv = cache_mem_v.reshape(bs, depth, n_mem, d)
    y_s, k_s, v_s, ki_s, conv_s, pool_s, cv_s = _run_trunk(
        x_sample, past_len, state_conv, state_pool, past_fn, cmk, cmv, layers, ffn_w, norm_final)

    mem_shape = (bp, depth, n_mem, MEM_HEADS, MEM_HEAD_DIM)
    return (y_p, y_s, k_p, v_p, ki_p, mem_k_p.reshape(mem_shape), mem_v_p.reshape(mem_shape), conv_p, pool_p,
            k_s, v_s, ki_s, conv_s, pool_s, cv_s)
```

```python
import functools
import math

import jax
import jax.numpy as jnp
from jax import lax
from jax.experimental import pallas as pl
from jax.experimental.pallas import tpu as pltpu

F32 = jnp.float32
BF16 = jnp.bfloat16
I32 = jnp.int32

D_MODEL = 1024
W_GROUP = 256
CONV_W = 3
POOL_WINDOWS = (2, 4, 8, 16)
POOL_CH = 64
POOL_BUF = 15
CHUNK = 128
CHUNK_HEADS = 4
ATT_HEADS = 4
HEAD_DIM = 64
IDX_HEADS = 8
IDX_DIM = 64
IDX_W_SCALE = (IDX_HEADS ** -0.5) * (IDX_DIM ** -0.5)
TOPK_MAX = 256
ROPE_THETA = 10000.0
MEM_HEADS = 4
MEM_HEAD_DIM = 256
D_FF = 3584
N_EXPERTS = 8
EPS = 1e-6

LANES = 128
PREV_ROWS = 16
P_PAD = 3072
HALF = P_PAD // 2
OFF_Q, OFF_K, OFF_QI, OFF_V, OFF_KI, OFF_WI = 1536, 1792, 2048, 2560, 2816, 2944
VMEM_LIMIT = 56 * 1024 * 1024

INT_MIN = -2 ** 31


def _cparams(sem):
    return pltpu.CompilerParams(dimension_semantics=sem, vmem_limit_bytes=VMEM_LIMIT)


def _rms(x, g):
    ms = jnp.mean(x * x, axis=-1, keepdims=True)
    return x * lax.rsqrt(ms + EPS) * g


def _dot(a, b):
    return jnp.dot(a, b, preferred_element_type=F32)


def _dot_nt(a, b):
    return lax.dot_general(a, b, (((1,), (1,)), ((), ())), preferred_element_type=F32)


def _norm_matmul_kernel(x_ref, g_ref, w_ref, o_ref, xn_ref, *, norm):
    @pl.when(pl.program_id(1) == 0)
    def _():
        x = x_ref[...]
        if norm:
            x = _rms(x, g_ref[...])
        xn_ref[...] = x.astype(BF16)

    o_ref[...] = _dot(xn_ref[...], w_ref[...]).astype(o_ref.dtype)


def norm_matmul(x, g, w, *, norm=True, out_dtype=F32, tm=512, tn=1024):
    m, k = x.shape
    n = w.shape[1]
    tm = min(tm, m)
    tn = min(tn, n)
    return pl.pallas_call(
        functools.partial(_norm_matmul_kernel, norm=norm),
        grid=(m // tm, n // tn),
        in_specs=[pl.BlockSpec((tm, k), lambda i, j: (i, 0)),
                  pl.BlockSpec((1, k), lambda i, j: (0, 0)),
                  pl.BlockSpec((k, tn), lambda i, j: (0, j))],
        out_specs=pl.BlockSpec((tm, tn), lambda i, j: (i, j)),
        out_shape=jax.ShapeDtypeStruct((m, n), out_dtype),
        scratch_shapes=[pltpu.VMEM((tm, k), BF16)],
        compiler_params=_cparams(("parallel", "arbitrary")),
        name="norm_matmul",
    )(x, g.reshape(1, k), w)


def _matmul_res_kernel(*refs, n_in):
    a_refs, w_refs = refs[:n_in], refs[n_in:2 * n_in]
    r_ref, o_ref = refs[2 * n_in], refs[2 * n_in + 1]
    acc = r_ref[...]
    for a, w in zip(a_refs, w_refs):
        acc = acc + _dot(a[...].astype(BF16), w[...])
    o_ref[...] = acc


def matmul_res(a_list, w_list, res, *, tm=512):
    m, n = res.shape
    tm = min(tm, m)
    n_in = len(a_list)
    in_specs = [pl.BlockSpec((tm, a.shape[1]), lambda i: (i, 0)) for a in a_list]
    in_specs += [pl.BlockSpec(w.shape, lambda i: (0, 0)) for w in w_list]
    in_specs += [pl.BlockSpec((tm, n), lambda i: (i, 0))]
    return pl.pallas_call(
        functools.partial(_matmul_res_kernel, n_in=n_in),
        grid=(m // tm,),
        in_specs=in_specs,
        out_specs=pl.BlockSpec((tm, n), lambda i: (i, 0)),
        out_shape=jax.ShapeDtypeStruct((m, n), F32),
        compiler_params=_cparams(("parallel",)),
        name="matmul_res",
    )(*a_list, *w_list, res)


def _ffn_kernel(x_ref, g_ref, w1_ref, w3_ref, w2_ref, gf_ref, o_ref, xn_ref, acc_ref, *, final_norm):
    f = pl.program_id(1)

    @pl.when(f == 0)
    def _():
        xn_ref[...] = _rms(x_ref[...], g_ref[...]).astype(BF16)
        acc_ref[...] = jnp.zeros_like(acc_ref)

    xn = xn_ref[...]
    a = _dot(xn, w1_ref[...])
    b = _dot(xn, w3_ref[...])
    hidden = (a * jax.nn.sigmoid(a) * b).astype(BF16)
    acc_ref[...] += _dot(hidden, w2_ref[...])

    @pl.when(f == pl.num_programs(1) - 1)
    def _():
        y = x_ref[...] + acc_ref[...]
        if final_norm:
            y = _rms(y, gf_ref[...])
        o_ref[...] = y


def ffn_dense(x, g, w1, w3, w2, g_final, *, final_norm, tm=512, tf=512):
    m, d = x.shape
    ff = w1.shape[1]
    tm = min(tm, m)
    return pl.pallas_call(
        functools.partial(_ffn_kernel, final_norm=final_norm),
        grid=(m // tm, ff // tf),
        in_specs=[pl.BlockSpec((tm, d), lambda i, f: (i, 0)),
                  pl.BlockSpec((1, d), lambda i, f: (0, 0)),
                  pl.BlockSpec((d, tf), lambda i, f: (0, f)),
                  pl.BlockSpec((d, tf), lambda i, f: (0, f)),
                  pl.BlockSpec((tf, d), lambda i, f: (f, 0)),
                  pl.BlockSpec((1, d), lambda i, f: (0, 0))],
        out_specs=pl.BlockSpec((tm, d), lambda i, f: (i, 0)),
        out_shape=jax.ShapeDtypeStruct((m, d), F32),
        scratch_shapes=[pltpu.VMEM((tm, d), BF16), pltpu.VMEM((tm, d), F32)],
        compiler_params=_cparams(("parallel", "arbitrary")),
        name="ffn_dense",
    )(x, g.reshape(1, d), w1, w3, w2, g_final.reshape(1, d))


def _router_kernel(x_ref, g_ref, r_ref, hn_ref, e_ref, gate_ref):
    hn = _rms(x_ref[...], g_ref[...])
    hn_ref[...] = hn.astype(BF16)
    logits = jnp.dot(hn, r_ref[...], preferred_element_type=F32, precision=lax.Precision.HIGHEST)
    lane = lax.broadcasted_iota(I32, logits.shape, 1)
    neg = jnp.float32(-jnp.inf)
    l1 = jnp.where(lane < N_EXPERTS, logits, neg)
    m1 = jnp.max(l1, axis=1, keepdims=True)
    i1 = jnp.min(jnp.where(l1 == m1, lane, LANES), axis=1, keepdims=True)
    l2 = jnp.where(lane == i1, neg, l1)
    m2 = jnp.max(l2, axis=1, keepdims=True)
    i2 = jnp.min(jnp.where(l2 == m2, lane, LANES), axis=1, keepdims=True)
    t = jnp.exp(m2 - m1)
    denom = 1.0 + t
    e_ref[...] = jnp.where(lane == 0, i1, jnp.where(lane == 1, i2, 0))
    gate_ref[...] = jnp.where(lane == 0, 1.0 / denom, jnp.where(lane == 1, t / denom, 0.0))


def moe_router(x, g, router_pad, *, tm=512):
    m, d = x.shape
    tm = min(tm, m)
    return pl.pallas_call(
        _router_kernel,
        grid=(m // tm,),
        in_specs=[pl.BlockSpec((tm, d), lambda i: (i, 0)),
                  pl.BlockSpec((1, d), lambda i: (0, 0)),
                  pl.BlockSpec((d, LANES), lambda i: (0, 0))],
        out_specs=[pl.BlockSpec((tm, d), lambda i: (i, 0)),
                   pl.BlockSpec((tm, LANES), lambda i: (i, 0)),
                   pl.BlockSpec((tm, LANES), lambda i: (i, 0))],
        out_shape=[jax.ShapeDtypeStruct((m, d), BF16),
                   jax.ShapeDtypeStruct((m, LANES), I32),
                   jax.ShapeDtypeStruct((m, LANES), F32)],
        compiler_params=_cparams(("parallel",)),
        name="moe_router",
    )(x, g.reshape(1, d), router_pad)


def _expert_kernel(be_ref, nb_ref, xs_ref, gate_ref, w1_ref, w3_ref, w2_ref, o_ref, acc_ref):
    j, f = pl.program_id(0), pl.program_id(1)
    live = j < nb_ref[0]

    @pl.when(f == 0)
    def _():
        acc_ref[...] = jnp.zeros_like(acc_ref)

    @pl.when(live)
    def _():
        xs = xs_ref[...]
        a = _dot(xs, w1_ref[0])
        b = _dot(xs, w3_ref[0])
        hidden = (a * jax.nn.sigmoid(a) * b).astype(BF16)
        acc_ref[...] += _dot(hidden, w2_ref[0])

    @pl.when(f == pl.num_programs(1) - 1)
    def _():
        o_ref[...] = acc_ref[...] * gate_ref[...]


def moe_experts(block_e, n_live, xs, slot_gate, w1, w3, w2, *, bm, tf=512):
    n_slots, d = xs.shape
    ff = w1.shape[2]
    n_blocks = n_slots // bm
    grid_spec = pltpu.PrefetchScalarGridSpec(
        num_scalar_prefetch=2,
        grid=(n_blocks, ff // tf),
        in_specs=[pl.BlockSpec((bm, d), lambda j, f, be, nb: (j, 0)),
                  pl.BlockSpec((bm, 1), lambda j, f, be, nb: (j, 0)),
                  pl.BlockSpec((1, d, tf), lambda j, f, be, nb: (be[j], 0, f)),
                  pl.BlockSpec((1, d, tf), lambda j, f, be, nb: (be[j], 0, f)),
                  pl.BlockSpec((1, tf, d), lambda j, f, be, nb: (be[j], f, 0))],
        out_specs=pl.BlockSpec((bm, d), lambda j, f, be, nb: (j, 0)),
        scratch_shapes=[pltpu.VMEM((bm, d), F32)],
    )
    return pl.pallas_call(
        _expert_kernel,
        grid_spec=grid_spec,
        out_shape=jax.ShapeDtypeStruct((n_slots, d), F32),
        compiler_params=_cparams(("parallel", "arbitrary")),
        name="moe_experts",
    )(block_e, n_live, xs, slot_gate, w1, w3, w2)


def _combine_kernel(x_ref, a_ref, b_ref, gf_ref, o_ref, *, final_norm):
    y = x_ref[...] + (a_ref[...] + b_ref[...])
    if final_norm:
        y = _rms(y, gf_ref[...])
    o_ref[...] = y


def moe_combine(x, ya, yb, g_final, *, final_norm, tm=512):
    m, d = x.shape
    tm = min(tm, m)
    spec = pl.BlockSpec((tm, d), lambda i: (i, 0))
    return pl.pallas_call(
        functools.partial(_combine_kernel, final_norm=final_norm),
        grid=(m // tm,),
        in_specs=[spec, spec, spec, pl.BlockSpec((1, d), lambda i: (0, 0))],
        out_specs=spec,
        out_shape=jax.ShapeDtypeStruct((m, d), F32),
        compiler_params=_cparams(("parallel",)),
        name="moe_combine",
    )(x, ya, yb, g_final.reshape(1, d))


def moe_ffn(x, g, router_pad, w1, w3, w2, g_final, *, final_norm):
    n, d = x.shape
    bm = 512 if n >= 4096 else 128
    hn, e12, g12 = moe_router(x, g, router_pad)
    e_flat = e12[:, :2].reshape(-1)
    g_flat = g12[:, :2].reshape(-1)
    n_assign = 2 * n
    order = jnp.argsort(e_flat, stable=True).astype(I32)
    e_sorted = e_flat[order]
    counts = jnp.sum(e_flat[:, None] == jnp.arange(N_EXPERTS, dtype=I32)[None, :], axis=0).astype(I32)
    padded = (counts + bm - 1) // bm * bm
    start = jnp.cumsum(counts) - counts
    pend = jnp.cumsum(padded)
    pstart = pend - padded
    slot = pstart[e_sorted] + (jnp.arange(n_assign, dtype=I32) - start[e_sorted])
    n_blocks = -(-n_assign // bm) + N_EXPERTS
    n_slots = n_blocks * bm
    slot_tok = jnp.zeros((n_slots,), I32).at[slot].set(order // 2)
    slot_gate = jnp.zeros((n_slots,), F32).at[slot].set(g_flat[order])
    assign_slot = jnp.zeros((n_assign,), I32).at[order].set(slot)
    block_e = jnp.minimum(
        jnp.searchsorted(pend, jnp.arange(n_blocks, dtype=I32) * bm, side='right'), N_EXPERTS - 1).astype(I32)
    n_live = (pend[-1] // bm).astype(I32).reshape(1)
    xs = hn[slot_tok]
    ys = moe_experts(block_e, n_live, xs, slot_gate.reshape(n_slots, 1), w1, w3, w2, bm=bm)
    pair = assign_slot.reshape(n, 2)
    return moe_combine(x, ys[pair[:, 0]], ys[pair[:, 1]], g_final, final_norm=final_norm)


def _swap_half_heads(x):
    n = x.shape[-1]
    lane = lax.broadcasted_iota(I32, x.shape, x.ndim - 1)
    fwd = pltpu.roll(x, n - HEAD_DIM // 2, x.ndim - 1)
    bwd = pltpu.roll(x, HEAD_DIM // 2, x.ndim - 1)
    return jnp.where(lane % HEAD_DIM < HEAD_DIM // 2, fwd, bwd)


def _rope(x, cos, sin):
    reps = x.shape[-1] // LANES
    if reps > 1:
        cos = jnp.concatenate([cos] * reps, axis=-1)
        sin = jnp.concatenate([sin] * reps, axis=-1)
    return x * cos + _swap_half_heads(x) * sin


def _mixer_kernel(*refs, tt, chunk, pos0, from_state, emit_cv):
    it = iter(refs)
    xa_ref, xb_ref = next(it), next(it)
    if from_state:
        convp_ref, poolp_ref = next(it), next(it)
    else:
        prev_ref = next(it)
    cos_ref, sin_ref = next(it), next(it)
    convw_ref, poolw_ref, pscale_ref, cnorm_ref, wm_ref, cbias_ref = (next(it) for _ in range(6))
    mix_ref, q_ref, k_ref, kb_ref, v_ref, vb_ref, qi_ref, ki_ref, ki2_ref, wi_ref, convs_ref, pools_ref = (
        next(it) for _ in range(12))
    cv_ref = next(it) if emit_cv else None
    ybuf, xbuf = next(it), next(it)

    t = pl.program_id(1)
    xa = xa_ref[0]
    a_in, a_b, a_c = xa[:, 0:256], xa[:, 256:512], xa[:, 512:768]
    b_in, c_u, c_v = xa[:, 768:1024], xa[:, 1024:1280], xa[:, 1280:1536]

    y = a_c * a_in
    if from_state:
        ybuf[PREV_ROWS - 2:PREV_ROWS, :] = convp_ref[0]
        xbuf[0:1, :] = jnp.zeros((1, W_GROUP), F32)
        xbuf[1:PREV_ROWS, :] = poolp_ref[0]
    else:
        first = (t == 0).astype(F32)
        prev = prev_ref[0] * (1.0 - first)
        ybuf[0:PREV_ROWS, :] = prev[:, 512:768] * prev[:, 0:256]
        xbuf[0:PREV_ROWS, :] = prev[:, 768:1024]
    ybuf[PREV_ROWS:PREV_ROWS + tt, :] = y
    xbuf[PREV_ROWS:PREV_ROWS + tt, :] = b_in

    cw = convw_ref[...]
    z = cw[0:1, :] * ybuf[PREV_ROWS - 2:PREV_ROWS - 2 + tt, :]
    z = z + cw[1:2, :] * ybuf[PREV_ROWS - 1:PREV_ROWS - 1 + tt, :]
    z = z + cw[2:3, :] * y
    ya = a_b * z
    convs_ref[0] = ybuf[PREV_ROWS + tt - 2:PREV_ROWS + tt, :]

    lane = lax.broadcasted_iota(I32, (tt, W_GROUP), 1)
    row = lax.broadcasted_iota(I32, (tt, W_GROUP), 0)
    pos = (pos0 + t * tt + row).astype(F32)
    run = b_in
    mean = jnp.zeros((tt, W_GROUP), F32)
    for j in range(1, POOL_WINDOWS[-1] + 1):
        if j > 1:
            run = run + xbuf[PREV_ROWS - (j - 1):PREV_ROWS - (j - 1) + tt, :]
        if j in POOL_WINDOWS:
            grp = POOL_WINDOWS.index(j)
            cnt = jnp.minimum(pos + 1.0, float(j))
            mean = jnp.where(lane // POOL_CH == grp, run / cnt, mean)
    dlt = (mean - b_in).astype(BF16)
    yb = _dot(dlt, poolw_ref[...]) * pscale_ref[...]
    pools_ref[0] = xbuf[PREV_ROWS + tt - POOL_BUF:PREV_ROWS + tt, :]

    vn = _rms(c_v, cnorm_ref[...])
    if emit_cv:
        cv_ref[0] = vn
    vnb = vn.astype(BF16)
    lane_c = lax.broadcasted_iota(I32, (chunk, W_GROUP), 1)
    parts = []
    for c in range(tt // chunk):
        vc = vnb[c * chunk:(c + 1) * chunk, :]
        s = cbias_ref[...]
        for hd in range(CHUNK_HEADS):
            sh = _dot(wm_ref[hd], vc)
            s = s + jnp.where(lane_c // (W_GROUP // CHUNK_HEADS) == hd, sh, 0.0)
        parts.append(s)
    s_all = parts[0] if len(parts) == 1 else jnp.concatenate(parts, axis=0)
    yc = c_u * s_all

    mix_ref[0] = jnp.concatenate([ya, yb, yc], axis=-1).astype(BF16)

    xb = xb_ref[0]
    cos, sin = cos_ref[...], sin_ref[...]
    q = _rope(xb[:, 0:256], cos, sin)
    k = _rope(xb[:, 256:512], cos, sin)
    qi = _rope(xb[:, 512:1024], cos, sin)
    v = xb[:, 1024:1280]
    ki = _rope(xb[:, 1280:1408], cos, sin)
    wi = xb[:, 1408:1536] * IDX_W_SCALE
    q_ref[0] = (q * (HEAD_DIM ** -0.5)).astype(BF16)
    k_ref[0] = k
    kb_ref[0] = k.astype(BF16)
    v_ref[0] = v
    vb_ref[0] = v.astype(BF16)
    qi_ref[0] = qi.astype(BF16)
    ki_ref[0] = ki
    lane128 = lax.broadcasted_iota(I32, ki.shape, 1)
    ki2_ref[0] = jnp.where(lane128 < IDX_DIM, ki, pltpu.roll(ki, IDX_DIM, 1)).astype(BF16)
    wi_ref[0] = wi


def mixers(proj, conv_prev, pool_prev, cos, sin, lw, *, pos0, from_state, emit_cv):
    b, t_len, _ = proj.shape
    tt = min(t_len, 256)
    chunk = min(t_len, CHUNK)
    nt = t_len // tt
    in_specs = [pl.BlockSpec((1, tt, HALF), lambda i, t: (i, t, 0)),
                pl.BlockSpec((1, tt, HALF), lambda i, t: (i, t, 1))]
    args = [proj, proj]
    if from_state:
        in_specs += [pl.BlockSpec((1, CONV_W - 1, W_GROUP), lambda i, t: (i, 0, 0)),
                     pl.BlockSpec((1, POOL_BUF, W_GROUP), lambda i, t: (i, 0, 0))]
        args += [conv_prev, pool_prev]
    else:
        per = tt // PREV_ROWS
        in_specs += [pl.BlockSpec((1, PREV_ROWS, HALF), lambda i, t: (i, jnp.maximum(t * per - 1, 0), 0))]
        args += [proj]
    in_specs += [pl.BlockSpec((tt, LANES), lambda i, t: (t, 0)),
                 pl.BlockSpec((tt, LANES), lambda i, t: (t, 0))]
    args += [cos, sin]
    consts = [lw['conv_w'], lw['pool_w_bd'], lw['pool_scale'], lw['chunk_norm'], lw['chunk_wm'], lw['chunk_bias']]
    for c in consts:
        in_specs.append(pl.BlockSpec(c.shape, lambda i, t, nd=c.ndim: (0,) * nd))
    args += consts

    def tok(width, dtype):
        return (pl.BlockSpec((1, tt, width), lambda i, t: (i, t, 0)), jax.ShapeDtypeStruct((b, t_len, width), dtype))

    outs = [tok(768, BF16),
            tok(256, BF16),
            tok(256, F32), tok(256, BF16),
            tok(256, F32), tok(256, BF16),
            tok(512, BF16),
            tok(LANES, F32),
            tok(LANES, BF16),
            tok(LANES, F32),
            (pl.BlockSpec((1, CONV_W - 1, W_GROUP), lambda i, t: (i, 0, 0)),
             jax.ShapeDtypeStruct((b, CONV_W - 1, W_GROUP), F32)),
            (pl.BlockSpec((1, POOL_BUF, W_GROUP), lambda i, t: (i, 0, 0)),
             jax.ShapeDtypeStruct((b, POOL_BUF, W_GROUP), F32))]
    if emit_cv:
        outs.append(tok(256, F32))
    return pl.pallas_call(
        functools.partial(_mixer_kernel, tt=tt, chunk=chunk, pos0=pos0, from_state=from_state, emit_cv=emit_cv),
        grid=(b, nt),
        in_specs=in_specs,
        out_specs=[o[0] for o in outs],
        out_shape=[o[1] for o in outs],
        scratch_shapes=[pltpu.VMEM((PREV_ROWS + tt, W_GROUP), F32), pltpu.VMEM((PREV_ROWS + tt, W_GROUP), F32)],
        compiler_params=_cparams(("parallel", "arbitrary")),
        name="mixers",
    )(*args)


def _ordered_key(x):
    x = jnp.where(x == 0.0, 0.0, x)
    bits = pltpu.bitcast(x, I32)
    return bits ^ ((bits >> 31) & jnp.int32(0x7FFFFFFF))


NEG_INF_KEY = -2139095041


def _count_rows(key_ref, n_tiles, n_groups, tpg, pred):
    rows = key_ref.shape[1]
    zero = jnp.zeros((rows, LANES), F32)

    def add(j, acc):
        return acc + jnp.where(pred(key_ref[j], j), 1.0, 0.0)

    if n_tiles is not None:
        acc = zero
        for j in range(n_tiles):
            acc = add(j, acc)
    else:
        def body(c, acc):
            for j in range(tpg):
                acc = add(c * tpg + j, acc)
            return acc

        acc = lax.fori_loop(0, n_groups, body, zero)
    return jnp.sum(acc, axis=1, keepdims=True)


def _topk_threshold(count, rows, topk, n_keys):
    lane = lax.broadcasted_iota(I32, (rows, LANES), 1)
    thr0 = jnp.where(count(lambda key, j: key >= 0) >= topk, 0, INT_MIN).astype(I32)

    def thr_bit(i, thr):
        cand = thr | (jnp.int32(1) << (30 - i))
        return jnp.where(count(lambda key, j: key >= cand) >= topk, cand, thr)

    thr = lax.fori_loop(0, 31, thr_bit, thr0)
    need = topk - count(lambda key, j: key > thr)
    n_eq = count(lambda key, j: key == thr)
    idx_bits = n_keys.bit_length()
    excess = jnp.max(jnp.where((n_eq > need) & (thr > NEG_INF_KEY), 1.0, 0.0))

    def tie_bound():
        def bound_bit(i, bound):
            cand = bound | (jnp.int32(1) << (idx_bits - 1 - i))
            hits = count(lambda key, j: (key == thr) & (j * LANES + lane < cand))
            return jnp.where(hits <= need, cand, bound)

        return lax.fori_loop(0, idx_bits, bound_bit, jnp.zeros((rows, 1), I32))

    bound = lax.cond(excess > 0.0, tie_bound, lambda: jnp.full((rows, 1), 2 ** idx_bits - 1, I32))
    return thr, bound


def _dsa_prompt_kernel(q_ref, qi_ref, wi_ref, k_ref, v_ref, ki2_ref, o_ref, key_ref, *, tq, kc, topk, n_keys):
    qb = pl.program_id(1)
    tpc = kc // LANES
    n_chunks = (qb * tq + tq + kc - 1) // kc
    q_pos = qb * tq + lax.broadcasted_iota(I32, (tq, kc), 0)
    col0 = lax.broadcasted_iota(I32, (tq, kc), 1)
    lane = lax.broadcasted_iota(I32, (tq, LANES), 1)
    neg_inf = jnp.float32(-jnp.inf)

    qi = qi_ref[0]
    wi = wi_ref[0]
    zero_b = jnp.zeros((tq, LANES), BF16)
    qm = []
    for h in range(IDX_HEADS):
        tile = qi[:, (h // 2) * LANES:(h // 2 + 1) * LANES]
        qm.append(jnp.where((lane // IDX_DIM) == (h % 2), tile, zero_b))

    def score_chunk(c, carry):
        ki2 = ki2_ref[0, pl.ds(pl.multiple_of(c * kc, kc), kc), :]
        acc = jnp.zeros((tq, kc), F32)
        for h in range(IDX_HEADS):
            d = _dot_nt(qm[h], ki2)
            acc = acc + wi[:, h:h + 1] * jnp.maximum(d, 0.0)
        key = _ordered_key(jnp.where(c * kc + col0 <= q_pos, acc, neg_inf))
        for j in range(tpc):
            key_ref[c * tpc + j] = key[:, j * LANES:(j + 1) * LANES]
        return carry

    lax.fori_loop(0, n_chunks, score_chunk, 0)

    count = functools.partial(_count_rows, key_ref, None, n_chunks, tpc)
    thr, bound = _topk_threshold(count, tq, topk, n_keys)

    q = q_ref[0]
    qh = []
    for h in range(ATT_HEADS):
        tile = q[:, (h // 2) * LANES:(h // 2 + 1) * LANES]
        qh.append(jnp.where((lane // HEAD_DIM) == (h % 2), tile, zero_b))
    lane_o = lax.broadcasted_iota(I32, (tq, W_GROUP), 1)

    def attend_chunk(c, carry):
        m_run, l_run, acc = carry
        off = pl.multiple_of(c * kc, kc)
        key = jnp.concatenate([key_ref[c * tpc + j] for j in range(tpc)], axis=1)
        col = c * kc + col0
        sel = ((key > thr) | ((key == thr) & (col < bound))) & (col <= q_pos)
        vb = v_ref[0, pl.ds(off, kc), :]
        m_new, l_new, scale_full, add_full = [], [], jnp.zeros((tq, W_GROUP), F32), jnp.zeros((tq, W_GROUP), F32)
        for h in range(ATT_HEADS):
            kh = k_ref[0, pl.ds(off, kc), (h // 2) * LANES:(h // 2 + 1) * LANES]
            logit = jnp.where(sel, _dot_nt(qh[h], kh), neg_inf)
            m_h = jnp.maximum(m_run[h], jnp.max(logit, axis=1, keepdims=True))
            m_safe = jnp.where(m_h == neg_inf, 0.0, m_h)
            p = jnp.exp(logit - m_safe)
            alpha = jnp.exp(m_run[h] - m_safe)
            l_new.append(alpha * l_run[h] + jnp.sum(p, axis=1, keepdims=True))
            m_new.append(m_h)
            pv = _dot(p.astype(BF16), vb)
            in_head = (lane_o // HEAD_DIM) == h
            scale_full = jnp.where(in_head, alpha, scale_full)
            add_full = jnp.where(in_head, pv, add_full)
        return tuple(m_new), tuple(l_new), acc * scale_full + add_full

    init = (tuple(jnp.full((tq, 1), neg_inf, F32) for _ in range(ATT_HEADS)),
            tuple(jnp.zeros((tq, 1), F32) for _ in range(ATT_HEADS)),
            jnp.zeros((tq, W_GROUP), F32))
    _, l_fin, acc = lax.fori_loop(0, n_chunks, attend_chunk, init)
    inv = jnp.zeros((tq, W_GROUP), F32)
    for h in range(ATT_HEADS):
        inv = jnp.where((lane_o // HEAD_DIM) == h, 1.0 / l_fin[h], inv)
    o_ref[0] = (acc * inv).astype(o_ref.dtype)


def dsa_prompt(q, qi, wi, k, v, ki2):
    b, t_len, _ = q.shape
    tq = min(t_len, 128)
    kc = min(t_len, 512)
    topk = min(TOPK_MAX, t_len // 4)
    return pl.pallas_call(
        functools.partial(_dsa_prompt_kernel, tq=tq, kc=kc, topk=topk, n_keys=t_len),
        grid=(b, t_len // tq),
        in_specs=[pl.BlockSpec((1, tq, W_GROUP), lambda i, j: (i, j, 0)),
                  pl.BlockSpec((1, tq, 2 * W_GROUP), lambda i, j: (i, j, 0)),
                  pl.BlockSpec((1, tq, LANES), lambda i, j: (i, j, 0)),
                  pl.BlockSpec((1, t_len, W_GROUP), lambda i, j: (i, 0, 0)),
                  pl.BlockSpec((1, t_len, W_GROUP), lambda i, j: (i, 0, 0)),
                  pl.BlockSpec((1, t_len, LANES), lambda i, j: (i, 0, 0))],
        out_specs=pl.BlockSpec((1, tq, W_GROUP), lambda i, j: (i, j, 0)),
        out_shape=jax.ShapeDtypeStruct((b, t_len, W_GROUP), BF16),
        scratch_shapes=[pltpu.VMEM((t_len // LANES, tq, LANES), I32)],
        compiler_params=_cparams(("parallel", "arbitrary")),
        name="dsa_prompt",
    )(q, qi, wi, k, v, ki2)


def _dsa_sample_kernel(q_ref, qi_ref, wi_ref, k_ref, v_ref, ki2_ref, o_ref, key_ref, *, g, t, s_len, q_off, topk):
    n_tiles = s_len // LANES
    lane = lax.broadcasted_iota(I32, (t, LANES), 1)
    lane_o = lax.broadcasted_iota(I32, (t, W_GROUP), 1)
    col = lax.broadcasted_iota(I32, (t, s_len), 1)
    causal = col <= q_off + lax.broadcasted_iota(I32, (t, s_len), 0)
    neg_inf = jnp.float32(-jnp.inf)

    for b in range(g):
        qi = qi_ref[b].astype(F32)
        lhs = jnp.concatenate(
            [jnp.where((lane // IDX_DIM) == (h % 2), qi[:, (h // 2) * LANES:(h // 2 + 1) * LANES], 0.0)
             for h in range(IDX_HEADS)], axis=0).astype(BF16)
        dots = _dot_nt(lhs, ki2_ref[b])
        wi = wi_ref[b]
        acc = jnp.zeros((t, s_len), F32)
        for h in range(IDX_HEADS):
            acc = acc + wi[:, h:h + 1] * jnp.maximum(dots[h * t:(h + 1) * t], 0.0)
        key = _ordered_key(jnp.where(causal, acc, neg_inf))
        for j in range(n_tiles):
            key_ref[j, b * t:(b + 1) * t, :] = key[:, j * LANES:(j + 1) * LANES]

    count = functools.partial(_count_rows, key_ref, n_tiles, None, None)
    thr, bound = _topk_threshold(count, g * t, topk, s_len)

    for b in range(g):
        rows = slice(b * t, (b + 1) * t)
        key = jnp.concatenate([key_ref[j, rows, :] for j in range(n_tiles)], axis=1)
        sel = ((key > thr[rows]) | ((key == thr[rows]) & (col < bound[rows]))) & causal
        q = q_ref[b].astype(F32)
        lhs = jnp.concatenate([jnp.where((lane_o // HEAD_DIM) == h, q, 0.0) for h in range(ATT_HEADS)],
                              axis=0).astype(BF16)
        logit = _dot_nt(lhs, k_ref[b])
        logit = jnp.where(jnp.concatenate([sel] * ATT_HEADS, axis=0), logit, neg_inf)
        p = jnp.exp(logit - jnp.max(logit, axis=1, keepdims=True))
        res = _dot(p.astype(BF16), v_ref[b]) / jnp.sum(p, axis=1, keepdims=True)
        out = jnp.zeros((t, W_GROUP), F32)
        for h in range(ATT_HEADS):
            out = jnp.where((lane_o // HEAD_DIM) == h, res[h * t:(h + 1) * t], out)
        o_ref[b] = out.astype(o_ref.dtype)


def dsa_sample(q, qi, wi, k, v, ki2, *, q_off, n_valid, g=4):
    b, t_len, _ = q.shape
    s_len = k.shape[1]
    g = math.gcd(g, b)
    topk = min(TOPK_MAX, n_valid // 4)

    def spec(rows, width):
        return pl.BlockSpec((g, rows, width), lambda i: (i, 0, 0))

    return pl.pallas_call(
        functools.partial(_dsa_sample_kernel, g=g, t=t_len, s_len=s_len, q_off=q_off, topk=topk),
        grid=(b // g,),
        in_specs=[spec(t_len, W_GROUP), spec(t_len, 2 * W_GROUP), spec(t_len, LANES),
                  spec(s_len, W_GROUP), spec(s_len, W_GROUP), spec(s_len, LANES)],
        out_specs=spec(t_len, W_GROUP),
        out_shape=jax.ShapeDtypeStruct((b, t_len, W_GROUP), BF16),
        scratch_shapes=[pltpu.VMEM((s_len // LANES, g * t_len, LANES), I32)],
        compiler_params=_cparams(("parallel",)),
        name="dsa_sample",
    )(q, qi, wi, k, v, ki2)


def _mem_attn_kernel(q_ref, mk_ref, mv_ref, o_ref):
    q = q_ref[0]
    mk = mk_ref[0].astype(BF16)
    mv = mv_ref[0].astype(BF16)
    outs = []
    for h in range(MEM_HEADS):
        sl = slice(h * MEM_HEAD_DIM, (h + 1) * MEM_HEAD_DIM)
        logit = _dot_nt(q[:, sl], mk[:, sl]) * (MEM_HEAD_DIM ** -0.5)
        m = jnp.max(logit, axis=1, keepdims=True)
        p = jnp.exp(logit - m)
        p = p / jnp.sum(p, axis=1, keepdims=True)
        outs.append(_dot(p.astype(BF16), mv[:, sl]))
    o_ref[0] = jnp.concatenate(outs, axis=-1).astype(o_ref.dtype)


def mem_attention(q, mk, mv, *, tq=512):
    b, t_len, d = q.shape
    m_len = mk.shape[1]
    tq = min(tq, t_len)
    return pl.pallas_call(
        _mem_attn_kernel,
        grid=(b, t_len // tq),
        in_specs=[pl.BlockSpec((1, tq, d), lambda i, j: (i, j, 0)),
                  pl.BlockSpec((1, m_len, d), lambda i, j: (i, 0, 0)),
                  pl.BlockSpec((1, m_len, d), lambda i, j: (i, 0, 0))],
        out_specs=pl.BlockSpec((1, tq, d), lambda i, j: (i, j, 0)),
        out_shape=jax.ShapeDtypeStruct((b, t_len, d), BF16),
        compiler_params=_cparams(("parallel", "arbitrary")),
        name="mem_attention",
    )(q, mk, mv)


def _prep_layer(l, p):
    w = p['w_in'][l]
    cuts = [0]
    for s in (W_GROUP,) * 9 + (IDX_HEADS * IDX_DIM, IDX_DIM, IDX_HEADS):
        cuts.append(cuts[-1] + s)
    piece = lambda i: w[:, cuts[i]:cuts[i + 1]]
    zeros = lambda n: jnp.zeros((D_MODEL, n), w.dtype)
    w_in = jnp.concatenate(
        [piece(i) for i in range(6)] + [piece(6), piece(7), piece(9), piece(8), piece(10), zeros(LANES - IDX_DIM),
                                        piece(11), zeros(LANES - IDX_HEADS)], axis=1).astype(BF16)
    pool_w = p['pool_w'][l]
    pool_bd = jnp.zeros((W_GROUP, W_GROUP), F32)
    for g in range(len(POOL_WINDOWS)):
        pool_bd = pool_bd.at[g * POOL_CH:(g + 1) * POOL_CH, g * POOL_CH:(g + 1) * POOL_CH].set(pool_w[g])
    return {
        'norm_mix': p['norm_mix'][l], 'w_in': w_in,
        'conv_w': p['conv_w'][l], 'pool_w_bd': pool_bd.astype(BF16),
        'pool_scale': p['pool_scale'][l].reshape(1, W_GROUP), 'chunk_norm': p['chunk_norm'][l].reshape(1, W_GROUP),
        'chunk_ws': p['chunk_ws'][l], 'chunk_b': p['chunk_b'][l],
        'w_out_abc': p['w_out'][l][:3 * W_GROUP].astype(BF16), 'w_out_d': p['w_out'][l][3 * W_GROUP:].astype(BF16),
        'norm_mem': p['norm_mem'][l], 'wq_mem': p['wq_mem'][l].astype(BF16), 'wo_mem': p['wo_mem'][l].astype(BF16),
        'norm_ffn': p['norm_ffn'][l],
    }


def _chunk_consts(lw, chunk):
    mask = jnp.tril(jnp.ones((chunk, chunk), dtype=bool))
    wm = jnp.where(mask[None], lw['chunk_ws'][:, :chunk, :chunk], 0).astype(BF16)
    bias = jnp.repeat(lw['chunk_b'][:, :chunk].T, W_GROUP // CHUNK_HEADS, axis=1)
    return wm, bias


def _rope_tables(pos):
    half = HEAD_DIM // 2
    inv = ROPE_THETA ** (-jnp.arange(half, dtype=F32) / half)
    ang = pos.astype(F32)[:, None] * inv[None, :]
    cos, sin = jnp.cos(ang), jnp.sin(ang)
    cos_t = jnp.concatenate([cos, cos] * (LANES // HEAD_DIM), axis=1)
    sin_t = jnp.concatenate([-sin, sin] * (LANES // HEAD_DIM), axis=1)
    return cos_t, sin_t


def _run_trunk(x, pos0, conv_prev, pool_prev, past_fn, mem_k, mem_v, layers, ffn_w, norm_final):
    b, t_len, d = x.shape
    n = b * t_len
    depth = len(layers)
    from_state = past_fn is not None
    pos = pos0 + jnp.arange(t_len, dtype=I32)
    cos_t, sin_t = _rope_tables(pos)
    h = x.reshape(n, d)
    ks, vs, kis, convs, pools, cvs = [], [], [], [], [], []
    for l, lw in enumerate(layers):
        proj = norm_matmul(h, lw['norm_mix'], lw['w_in']).reshape(b, t_len, P_PAD)
        chunk = min(t_len, CHUNK)
        wm, cbias = _chunk_consts(lw, chunk)
        lw_m = dict(lw, chunk_wm=wm, chunk_bias=cbias)
        outs = mixers(proj, conv_prev[:, l] if from_state else None, pool_prev[:, l] if from_state else None,
                      cos_t, sin_t, lw_m, pos0=pos0, from_state=from_state, emit_cv=from_state)
        mix, q, k, kb, v, vb, qi, ki, ki2, wi, conv_s, pool_s = outs[:12]
        if from_state:
            cvs.append(outs[12])
            pk, pv, pki2 = past_fn(l)
            s_real = pk.shape[1] + t_len
            s_pad = -(-s_real // LANES) * LANES
            padk = lambda past, new: jnp.concatenate(
                [past, new, jnp.zeros((b, s_pad - s_real, new.shape[2]), new.dtype)], axis=1)
            yd = dsa_sample(q, qi, wi, padk(pk, kb), padk(pv, vb), padk(pki2, ki2),
                            q_off=pk.shape[1], n_valid=s_real)
        else:
            yd = dsa_prompt(q, qi, wi, kb, vb, ki2)
        h = matmul_res([mix.reshape(n, 3 * W_GROUP), yd.reshape(n, W_GROUP)], [lw['w_out_abc'], lw['w_out_d']], h)
        qm = norm_matmul(h, lw['norm_mem'], lw['wq_mem'], out_dtype=BF16).reshape(b, t_len, d)
        om = mem_attention(qm, mem_k[:, l], mem_v[:, l])
        h = matmul_res([om.reshape(n, d)], [lw['wo_mem']], h)
        last = l == depth - 1
        fw = ffn_w[l]
        if fw['kind'] == 'dense':
            h = ffn_dense(h, lw['norm_ffn'], fw['w1'], fw['w3'], fw['w2'], norm_final, final_norm=last)
        else:
            h = moe_ffn(h, lw['norm_ffn'], fw['router'], fw['w1'], fw['w3'], fw['w2'], norm_final, final_norm=last)
        ks.append(k.reshape(b, t_len, ATT_HEADS, HEAD_DIM))
        vs.append(v.reshape(b, t_len, ATT_HEADS, HEAD_DIM))
        kis.append(ki[:, :, :IDX_DIM])
        convs.append(conv_s)
        pools.append(pool_s)
    st = lambda xs: jnp.stack(xs, axis=1)
    y = h.reshape(b, t_len, d)
    return y, st(ks), st(vs), st(kis), st(convs), st(pools), (st(cvs) if cvs else None)


def kernel(x_prompt, x_sample, mem_prompt, cache_k, cache_v, cache_kidx, cache_mem_k, cache_mem_v, state_conv, state_pool, page_table, norm_mix, w_in, conv_w, pool_w, pool_scale, chunk_norm, chunk_ws, chunk_b, w_out, norm_mem, wq_mem, wk_mem, wv_mem, wo_mem, norm_ffn, w1_dense, w3_dense, w2_dense, router, w1_moe, w3_moe, w2_moe, norm_final):
    p = {
        'norm_mix': norm_mix, 'w_in': w_in, 'conv_w': conv_w, 'pool_w': pool_w, 'pool_scale': pool_scale,
        'chunk_norm': chunk_norm, 'chunk_ws': chunk_ws, 'chunk_b': chunk_b, 'w_out': w_out, 'norm_mem': norm_mem,
        'wq_mem': wq_mem, 'wo_mem': wo_mem, 'norm_ffn': norm_ffn,
    }
    depth = norm_mix.shape[0]
    layers = [_prep_layer(l, p) for l in range(depth)]
    ffn_w = []
    for l in range(depth):
        j = l // 2
        if l % 2 == 0:
            ffn_w.append({'kind': 'dense', 'w1': w1_dense[j].astype(BF16), 'w3': w3_dense[j].astype(BF16),
                          'w2': w2_dense[j].astype(BF16)})
        else:
            router_pad = jnp.concatenate([router[j], jnp.zeros((D_MODEL, LANES - N_EXPERTS), F32)], axis=1)
            ffn_w.append({'kind': 'moe', 'router': router_pad, 'w1': w1_moe[j].astype(BF16),
                          'w3': w3_moe[j].astype(BF16), 'w2': w2_moe[j].astype(BF16)})

    bp, tp, d = x_prompt.shape
    n_mem = mem_prompt.shape[1]
    w_kv = jnp.concatenate([wk_mem[l] for l in range(depth)] + [wv_mem[l] for l in range(depth)], axis=1).astype(BF16)
    mem_kv = norm_matmul(mem_prompt.reshape(bp * n_mem, d), jnp.ones((d,), F32), w_kv, norm=False)
    mem_kv = mem_kv.reshape(bp, n_mem, 2, depth, d)
    mem_k_p = jnp.moveaxis(mem_kv[:, :, 0], 2, 1)
    mem_v_p = jnp.moveaxis(mem_kv[:, :, 1], 2, 1)
    y_p, k_p, v_p, ki_p, conv_p, pool_p, _ = _run_trunk(
        x_prompt, 0, None, None, None, mem_k_p, mem_v_p, layers, ffn_w, norm_final)

    bs, ts, _ = x_sample.shape
    n_pages, page = page_table.shape[1], cache_k.shape[2]
    past_len = n_pages * page

    def past_fn(l):
        pk = cache_k[page_table, l].reshape(bs, past_len, W_GROUP).astype(BF16)
        pv = cache_v[page_table, l].reshape(bs, past_len, W_GROUP).astype(BF16)
        pki = cache_kidx[page_table, l].reshape(bs, past_len, IDX_DIM).astype(BF16)
        return pk, pv, jnp.concatenate([pki, pki], axis=-1)

    cmk = cache_mem_k.reshape(bs, depth, n_mem, d)
    cmv = cache_mem_v.reshape(bs, depth, n_mem, d)
    y_s, k_s, v_s, ki_s, conv_s, pool_s, cv_s = _run_trunk(
        x_sample, past_len, state_conv, state_pool, past_fn, cmk, cmv, layers, ffn_w, norm_final)

    mem_shape = (bp, depth, n_mem, MEM_HEADS, MEM_HEAD_DIM)
    return (y_p, y_s, k_p, v_p, ki_p, mem_k_p.reshape(mem_shape), mem_v_p.reshape(mem_shape), conv_p, pool_p,
            k_s, v_s, ki_s, conv_s, pool_s, cv_s)
```

```python
import functools
import math

import jax
import jax.numpy as jnp
from jax import lax
from jax.experimental import pallas as pl
from jax.experimental.pallas import tpu as pltpu

F32 = jnp.float32
BF16 = jnp.bfloat16
I32 = jnp.int32

D_MODEL = 1024
W_GROUP = 256
CONV_W = 3
POOL_WINDOWS = (2, 4, 8, 16)
POOL_CH = 64
POOL_BUF = 15
CHUNK = 128
CHUNK_HEADS = 4
ATT_HEADS = 4
HEAD_DIM = 64
IDX_HEADS = 8
IDX_DIM = 64
IDX_W_SCALE = (IDX_HEADS ** -0.5) * (IDX_DIM ** -0.5)
TOPK_MAX = 256
ROPE_THETA = 10000.0
MEM_HEADS = 4
MEM_HEAD_DIM = 256
D_FF = 3584
N_EXPERTS = 8
EPS = 1e-6

LANES = 128
PREV_ROWS = 16
P_PAD = 3072
HALF = P_PAD // 2
OFF_Q, OFF_K, OFF_QI, OFF_V, OFF_KI, OFF_WI = 1536, 1792, 2048, 2560, 2816, 2944
VMEM_LIMIT = 56 * 1024 * 1024

INT_MIN = -2 ** 31
Q_SCALE = (HEAD_DIM ** -0.5) * math.log2(math.e)


def _cparams(sem):
    return pltpu.CompilerParams(dimension_semantics=sem, vmem_limit_bytes=VMEM_LIMIT)


def _rms(x, g):
    ms = jnp.mean(x * x, axis=-1, keepdims=True)
    return x * lax.rsqrt(ms + EPS) * g


def _dot(a, b):
    return jnp.dot(a, b, preferred_element_type=F32)


def _dot_nt(a, b):
    return lax.dot_general(a, b, (((1,), (1,)), ((), ())), preferred_element_type=F32)


def _norm_matmul_kernel(x_ref, g_ref, w_ref, o_ref, xn_ref, *, norm):
    @pl.when(pl.program_id(1) == 0)
    def _():
        x = x_ref[...]
        if norm:
            x = _rms(x, g_ref[...])
        xn_ref[...] = x.astype(BF16)

    o_ref[...] = _dot(xn_ref[...], w_ref[...]).astype(o_ref.dtype)


def norm_matmul(x, g, w, *, norm=True, out_dtype=F32, tm=512, tn=1024):
    m, k = x.shape
    n = w.shape[1]
    tm = min(tm, m)
    tn = min(tn, n)
    return pl.pallas_call(
        functools.partial(_norm_matmul_kernel, norm=norm),
        grid=(m // tm, n // tn),
        in_specs=[pl.BlockSpec((tm, k), lambda i, j: (i, 0)),
                  pl.BlockSpec((1, k), lambda i, j: (0, 0)),
                  pl.BlockSpec((k, tn), lambda i, j: (0, j))],
        out_specs=pl.BlockSpec((tm, tn), lambda i, j: (i, j)),
        out_shape=jax.ShapeDtypeStruct((m, n), out_dtype),
        scratch_shapes=[pltpu.VMEM((tm, k), BF16)],
        compiler_params=_cparams(("parallel", "arbitrary")),
        name="norm_matmul",
    )(x, g.reshape(1, k), w)


def _matmul_res_kernel(*refs, n_in):
    a_refs, w_refs = refs[:n_in], refs[n_in:2 * n_in]
    r_ref, o_ref = refs[2 * n_in], refs[2 * n_in + 1]
    acc = r_ref[...]
    for a, w in zip(a_refs, w_refs):
        acc = acc + _dot(a[...].astype(BF16), w[...])
    o_ref[...] = acc


def matmul_res(a_list, w_list, res, *, tm=512):
    m, n = res.shape
    tm = min(tm, m)
    n_in = len(a_list)
    in_specs = [pl.BlockSpec((tm, a.shape[1]), lambda i: (i, 0)) for a in a_list]
    in_specs += [pl.BlockSpec(w.shape, lambda i: (0, 0)) for w in w_list]
    in_specs += [pl.BlockSpec((tm, n), lambda i: (i, 0))]
    return pl.pallas_call(
        functools.partial(_matmul_res_kernel, n_in=n_in),
        grid=(m // tm,),
        in_specs=in_specs,
        out_specs=pl.BlockSpec((tm, n), lambda i: (i, 0)),
        out_shape=jax.ShapeDtypeStruct((m, n), F32),
        compiler_params=_cparams(("parallel",)),
        name="matmul_res",
    )(*a_list, *w_list, res)


def _ffn_kernel(x_ref, g_ref, w1_ref, w3_ref, w2_ref, gf_ref, o_ref, xn_ref, acc_ref, *, final_norm):
    f = pl.program_id(1)

    @pl.when(f == 0)
    def _():
        xn_ref[...] = _rms(x_ref[...], g_ref[...]).astype(BF16)
        acc_ref[...] = jnp.zeros_like(acc_ref)

    xn = xn_ref[...]
    a = _dot(xn, w1_ref[...])
    b = _dot(xn, w3_ref[...])
    hidden = (a * jax.nn.sigmoid(a) * b).astype(BF16)
    acc_ref[...] += _dot(hidden, w2_ref[...])

    @pl.when(f == pl.num_programs(1) - 1)
    def _():
        y = x_ref[...] + acc_ref[...]
        if final_norm:
            y = _rms(y, gf_ref[...])
        o_ref[...] = y


def ffn_dense(x, g, w1, w3, w2, g_final, *, final_norm, tm=512, tf=512):
    m, d = x.shape
    ff = w1.shape[1]
    tm = min(tm, m)
    return pl.pallas_call(
        functools.partial(_ffn_kernel, final_norm=final_norm),
        grid=(m // tm, ff // tf),
        in_specs=[pl.BlockSpec((tm, d), lambda i, f: (i, 0)),
                  pl.BlockSpec((1, d), lambda i, f: (0, 0)),
                  pl.BlockSpec((d, tf), lambda i, f: (0, f)),
                  pl.BlockSpec((d, tf), lambda i, f: (0, f)),
                  pl.BlockSpec((tf, d), lambda i, f: (f, 0)),
                  pl.BlockSpec((1, d), lambda i, f: (0, 0))],
        out_specs=pl.BlockSpec((tm, d), lambda i, f: (i, 0)),
        out_shape=jax.ShapeDtypeStruct((m, d), F32),
        scratch_shapes=[pltpu.VMEM((tm, d), BF16), pltpu.VMEM((tm, d), F32)],
        compiler_params=_cparams(("parallel", "arbitrary")),
        name="ffn_dense",
    )(x, g.reshape(1, d), w1, w3, w2, g_final.reshape(1, d))


def _router_kernel(x_ref, g_ref, r_ref, hn_ref, e_ref, gate_ref):
    hn = _rms(x_ref[...], g_ref[...])
    hn_ref[...] = hn.astype(BF16)
    logits = jnp.dot(hn, r_ref[...], preferred_element_type=F32, precision=lax.Precision.HIGHEST)
    lane = lax.broadcasted_iota(I32, logits.shape, 1)
    neg = jnp.float32(-jnp.inf)
    l1 = jnp.where(lane < N_EXPERTS, logits, neg)
    m1 = jnp.max(l1, axis=1, keepdims=True)
    i1 = jnp.min(jnp.where(l1 == m1, lane, LANES), axis=1, keepdims=True)
    l2 = jnp.where(lane == i1, neg, l1)
    m2 = jnp.max(l2, axis=1, keepdims=True)
    i2 = jnp.min(jnp.where(l2 == m2, lane, LANES), axis=1, keepdims=True)
    t = jnp.exp(m2 - m1)
    denom = 1.0 + t
    e_ref[...] = jnp.where(lane == 0, i1, jnp.where(lane == 1, i2, 0))
    gate_ref[...] = jnp.where(lane == 0, 1.0 / denom, jnp.where(lane == 1, t / denom, 0.0))


def moe_router(x, g, router_pad, *, tm=512):
    m, d = x.shape
    tm = min(tm, m)
    return pl.pallas_call(
        _router_kernel,
        grid=(m // tm,),
        in_specs=[pl.BlockSpec((tm, d), lambda i: (i, 0)),
                  pl.BlockSpec((1, d), lambda i: (0, 0)),
                  pl.BlockSpec((d, LANES), lambda i: (0, 0))],
        out_specs=[pl.BlockSpec((tm, d), lambda i: (i, 0)),
                   pl.BlockSpec((tm, LANES), lambda i: (i, 0)),
                   pl.BlockSpec((tm, LANES), lambda i: (i, 0))],
        out_shape=[jax.ShapeDtypeStruct((m, d), BF16),
                   jax.ShapeDtypeStruct((m, LANES), I32),
                   jax.ShapeDtypeStruct((m, LANES), F32)],
        compiler_params=_cparams(("parallel",)),
        name="moe_router",
    )(x, g.reshape(1, d), router_pad)


def _expert_kernel(be_ref, nb_ref, xs_ref, gate_ref, w1_ref, w3_ref, w2_ref, o_ref, acc_ref):
    j, f = pl.program_id(0), pl.program_id(1)
    live = j < nb_ref[0]

    @pl.when(f == 0)
    def _():
        acc_ref[...] = jnp.zeros_like(acc_ref)

    @pl.when(live)
    def _():
        xs = xs_ref[...]
        a = _dot(xs, w1_ref[0])
        b = _dot(xs, w3_ref[0])
        hidden = (a * jax.nn.sigmoid(a) * b).astype(BF16)
        acc_ref[...] += _dot(hidden, w2_ref[0])

    @pl.when(f == pl.num_programs(1) - 1)
    def _():
        o_ref[...] = acc_ref[...] * gate_ref[...]


def moe_experts(block_e, n_live, xs, slot_gate, w1, w3, w2, *, bm, tf=512):
    n_slots, d = xs.shape
    ff = w1.shape[2]
    n_blocks = n_slots // bm
    grid_spec = pltpu.PrefetchScalarGridSpec(
        num_scalar_prefetch=2,
        grid=(n_blocks, ff // tf),
        in_specs=[pl.BlockSpec((bm, d), lambda j, f, be, nb: (j, 0)),
                  pl.BlockSpec((bm, 1), lambda j, f, be, nb: (j, 0)),
                  pl.BlockSpec((1, d, tf), lambda j, f, be, nb: (be[j], 0, f)),
                  pl.BlockSpec((1, d, tf), lambda j, f, be, nb: (be[j], 0, f)),
                  pl.BlockSpec((1, tf, d), lambda j, f, be, nb: (be[j], f, 0))],
        out_specs=pl.BlockSpec((bm, d), lambda j, f, be, nb: (j, 0)),
        scratch_shapes=[pltpu.VMEM((bm, d), F32)],
    )
    return pl.pallas_call(
        _expert_kernel,
        grid_spec=grid_spec,
        out_shape=jax.ShapeDtypeStruct((n_slots, d), F32),
        compiler_params=_cparams(("parallel", "arbitrary")),
        name="moe_experts",
    )(block_e, n_live, xs, slot_gate, w1, w3, w2)


def _combine_kernel(x_ref, a_ref, b_ref, gf_ref, o_ref, *, final_norm):
    y = x_ref[...] + (a_ref[...] + b_ref[...])
    if final_norm:
        y = _rms(y, gf_ref[...])
    o_ref[...] = y


def moe_combine(x, ya, yb, g_final, *, final_norm, tm=512):
    m, d = x.shape
    tm = min(tm, m)
    spec = pl.BlockSpec((tm, d), lambda i: (i, 0))
    return pl.pallas_call(
        functools.partial(_combine_kernel, final_norm=final_norm),
        grid=(m // tm,),
        in_specs=[spec, spec, spec, pl.BlockSpec((1, d), lambda i: (0, 0))],
        out_specs=spec,
        out_shape=jax.ShapeDtypeStruct((m, d), F32),
        compiler_params=_cparams(("parallel",)),
        name="moe_combine",
    )(x, ya, yb, g_final.reshape(1, d))


def moe_ffn(x, g, router_pad, w1, w3, w2, g_final, *, final_norm):
    n, d = x.shape
    bm = 512 if n >= 4096 else 128
    hn, e12, g12 = moe_router(x, g, router_pad)
    e_flat = e12[:, :2].reshape(-1)
    g_flat = g12[:, :2].reshape(-1)
    n_assign = 2 * n
    order = jnp.argsort(e_flat, stable=True).astype(I32)
    e_sorted = e_flat[order]
    counts = jnp.sum(e_flat[:, None] == jnp.arange(N_EXPERTS, dtype=I32)[None, :], axis=0).astype(I32)
    padded = (counts + bm - 1) // bm * bm
    start = jnp.cumsum(counts) - counts
    pend = jnp.cumsum(padded)
    pstart = pend - padded
    slot = pstart[e_sorted] + (jnp.arange(n_assign, dtype=I32) - start[e_sorted])
    n_blocks = -(-n_assign // bm) + N_EXPERTS
    n_slots = n_blocks * bm
    slot_tok = jnp.zeros((n_slots,), I32).at[slot].set(order // 2)
    slot_gate = jnp.zeros((n_slots,), F32).at[slot].set(g_flat[order])
    assign_slot = jnp.zeros((n_assign,), I32).at[order].set(slot)
    block_e = jnp.minimum(
        jnp.searchsorted(pend, jnp.arange(n_blocks, dtype=I32) * bm, side='right'), N_EXPERTS - 1).astype(I32)
    n_live = (pend[-1] // bm).astype(I32).reshape(1)
    xs = hn[slot_tok]
    ys = moe_experts(block_e, n_live, xs, slot_gate.reshape(n_slots, 1), w1, w3, w2, bm=bm)
    pair = assign_slot.reshape(n, 2)
    return moe_combine(x, ys[pair[:, 0]], ys[pair[:, 1]], g_final, final_norm=final_norm)


def _swap_half_heads(x):
    n = x.shape[-1]
    lane = lax.broadcasted_iota(I32, x.shape, x.ndim - 1)
    fwd = pltpu.roll(x, n - HEAD_DIM // 2, x.ndim - 1)
    bwd = pltpu.roll(x, HEAD_DIM // 2, x.ndim - 1)
    return jnp.where(lane % HEAD_DIM < HEAD_DIM // 2, fwd, bwd)


def _rope(x, cos, sin):
    reps = x.shape[-1] // LANES
    if reps > 1:
        cos = jnp.concatenate([cos] * reps, axis=-1)
        sin = jnp.concatenate([sin] * reps, axis=-1)
    return x * cos + _swap_half_heads(x) * sin


def _mixer_kernel(*refs, tt, chunk, pos0, from_state, emit_cv):
    it = iter(refs)
    xa_ref, xb_ref = next(it), next(it)
    if from_state:
        convp_ref, poolp_ref = next(it), next(it)
    else:
        prev_ref = next(it)
    cos_ref, sin_ref = next(it), next(it)
    convw_ref, poolw_ref, pscale_ref, cnorm_ref, wm_ref, cbias_ref = (next(it) for _ in range(6))
    mix_ref, q_ref, k_ref, kb_ref, v_ref, vb_ref, qi_ref, ki_ref, ki2_ref, wi_ref, convs_ref, pools_ref = (
        next(it) for _ in range(12))
    cv_ref = next(it) if emit_cv else None
    ybuf, xbuf = next(it), next(it)

    t = pl.program_id(1)
    xa = xa_ref[0]
    a_in, a_b, a_c = xa[:, 0:256], xa[:, 256:512], xa[:, 512:768]
    b_in, c_u, c_v = xa[:, 768:1024], xa[:, 1024:1280], xa[:, 1280:1536]

    y = a_c * a_in
    if from_state:
        ybuf[PREV_ROWS - 2:PREV_ROWS, :] = convp_ref[0]
        xbuf[0:1, :] = jnp.zeros((1, W_GROUP), F32)
        xbuf[1:PREV_ROWS, :] = poolp_ref[0]
    else:
        first = (t == 0).astype(F32)
        prev = prev_ref[0] * (1.0 - first)
        ybuf[0:PREV_ROWS, :] = prev[:, 512:768] * prev[:, 0:256]
        xbuf[0:PREV_ROWS, :] = prev[:, 768:1024]
    ybuf[PREV_ROWS:PREV_ROWS + tt, :] = y
    xbuf[PREV_ROWS:PREV_ROWS + tt, :] = b_in

    cw = convw_ref[...]
    z = cw[0:1, :] * ybuf[PREV_ROWS - 2:PREV_ROWS - 2 + tt, :]
    z = z + cw[1:2, :] * ybuf[PREV_ROWS - 1:PREV_ROWS - 1 + tt, :]
    z = z + cw[2:3, :] * y
    ya = a_b * z
    convs_ref[0] = ybuf[PREV_ROWS + tt - 2:PREV_ROWS + tt, :]

    lane = lax.broadcasted_iota(I32, (tt, W_GROUP), 1)
    row = lax.broadcasted_iota(I32, (tt, W_GROUP), 0)
    pos = (pos0 + t * tt + row).astype(F32)
    run = b_in
    mean = jnp.zeros((tt, W_GROUP), F32)
    for j in range(1, POOL_WINDOWS[-1] + 1):
        if j > 1:
            run = run + xbuf[PREV_ROWS - (j - 1):PREV_ROWS - (j - 1) + tt, :]
        if j in POOL_WINDOWS:
            grp = POOL_WINDOWS.index(j)
            cnt = jnp.minimum(pos + 1.0, float(j))
            mean = jnp.where(lane // POOL_CH == grp, run / cnt, mean)
    dlt = (mean - b_in).astype(BF16)
    yb = _dot(dlt, poolw_ref[...]) * pscale_ref[...]
    pools_ref[0] = xbuf[PREV_ROWS + tt - POOL_BUF:PREV_ROWS + tt, :]

    vn = _rms(c_v, cnorm_ref[...])
    if emit_cv:
        cv_ref[0] = vn
    vnb = vn.astype(BF16)
    lane_c = lax.broadcasted_iota(I32, (chunk, W_GROUP), 1)
    parts = []
    for c in range(tt // chunk):
        vc = vnb[c * chunk:(c + 1) * chunk, :]
        s = cbias_ref[...]
        for hd in range(CHUNK_HEADS):
            sh = _dot(wm_ref[hd], vc)
            s = s + jnp.where(lane_c // (W_GROUP // CHUNK_HEADS) == hd, sh, 0.0)
        parts.append(s)
    s_all = parts[0] if len(parts) == 1 else jnp.concatenate(parts, axis=0)
    yc = c_u * s_all

    mix_ref[0] = jnp.concatenate([ya, yb, yc], axis=-1).astype(BF16)

    xb = xb_ref[0]
    cos, sin = cos_ref[...], sin_ref[...]
    q = _rope(xb[:, 0:256], cos, sin)
    k = _rope(xb[:, 256:512], cos, sin)
    qi = _rope(xb[:, 512:1024], cos, sin)
    v = xb[:, 1024:1280]
    ki = _rope(xb[:, 1280:1408], cos, sin)
    wi = xb[:, 1408:1536] * IDX_W_SCALE
    q_ref[0] = (q * Q_SCALE).astype(BF16)
    k_ref[0] = k
    kb_ref[0] = k.astype(BF16)
    v_ref[0] = v
    vb_ref[0] = v.astype(BF16)
    qi_ref[0] = qi.astype(BF16)
    ki_ref[0] = ki
    lane128 = lax.broadcasted_iota(I32, ki.shape, 1)
    ki2_ref[0] = jnp.where(lane128 < IDX_DIM, ki, pltpu.roll(ki, IDX_DIM, 1)).astype(BF16)
    wi_ref[0] = wi


def mixers(proj, conv_prev, pool_prev, cos, sin, lw, *, pos0, from_state, emit_cv):
    b, t_len, _ = proj.shape
    tt = min(t_len, 256)
    chunk = min(t_len, CHUNK)
    nt = t_len // tt
    in_specs = [pl.BlockSpec((1, tt, HALF), lambda i, t: (i, t, 0)),
                pl.BlockSpec((1, tt, HALF), lambda i, t: (i, t, 1))]
    args = [proj, proj]
    if from_state:
        in_specs += [pl.BlockSpec((1, CONV_W - 1, W_GROUP), lambda i, t: (i, 0, 0)),
                     pl.BlockSpec((1, POOL_BUF, W_GROUP), lambda i, t: (i, 0, 0))]
        args += [conv_prev, pool_prev]
    else:
        per = tt // PREV_ROWS
        in_specs += [pl.BlockSpec((1, PREV_ROWS, HALF), lambda i, t: (i, jnp.maximum(t * per - 1, 0), 0))]
        args += [proj]
    in_specs += [pl.BlockSpec((tt, LANES), lambda i, t: (t, 0)),
                 pl.BlockSpec((tt, LANES), lambda i, t: (t, 0))]
    args += [cos, sin]
    consts = [lw['conv_w'], lw['pool_w_bd'], lw['pool_scale'], lw['chunk_norm'], lw['chunk_wm'], lw['chunk_bias']]
    for c in consts:
        in_specs.append(pl.BlockSpec(c.shape, lambda i, t, nd=c.ndim: (0,) * nd))
    args += consts

    def tok(width, dtype):
        return (pl.BlockSpec((1, tt, width), lambda i, t: (i, t, 0)), jax.ShapeDtypeStruct((b, t_len, width), dtype))

    outs = [tok(768, BF16),
            tok(256, BF16),
            tok(256, F32), tok(256, BF16),
            tok(256, F32), tok(256, BF16),
            tok(512, BF16),
            tok(LANES, F32),
            tok(LANES, BF16),
            tok(LANES, F32),
            (pl.BlockSpec((1, CONV_W - 1, W_GROUP), lambda i, t: (i, 0, 0)),
             jax.ShapeDtypeStruct((b, CONV_W - 1, W_GROUP), F32)),
            (pl.BlockSpec((1, POOL_BUF, W_GROUP), lambda i, t: (i, 0, 0)),
             jax.ShapeDtypeStruct((b, POOL_BUF, W_GROUP), F32))]
    if emit_cv:
        outs.append(tok(256, F32))
    return pl.pallas_call(
        functools.partial(_mixer_kernel, tt=tt, chunk=chunk, pos0=pos0, from_state=from_state, emit_cv=emit_cv),
        grid=(b, nt),
        in_specs=in_specs,
        out_specs=[o[0] for o in outs],
        out_shape=[o[1] for o in outs],
        scratch_shapes=[pltpu.VMEM((PREV_ROWS + tt, W_GROUP), F32), pltpu.VMEM((PREV_ROWS + tt, W_GROUP), F32)],
        compiler_params=_cparams(("parallel", "arbitrary")),
        name="mixers",
    )(*args)


def _ordered_key(x):
    x = jnp.where(x == 0.0, 0.0, x)
    bits = pltpu.bitcast(x, I32)
    return bits ^ ((bits >> 31) & jnp.int32(0x7FFFFFFF))


NEG_INF_KEY = -2139095041


def _count_rows(key_ref, n_tiles, n_groups, tpg, pred):
    rows = key_ref.shape[1]
    zero = jnp.zeros((rows, LANES), F32)

    def add(j, acc):
        return acc + jnp.where(pred(key_ref[j], j), 1.0, 0.0)

    if n_tiles is not None:
        acc = zero
        for j in range(n_tiles):
            acc = add(j, acc)
    else:
        def body(c, acc):
            for j in range(tpg):
                acc = add(c * tpg + j, acc)
            return acc

        acc = lax.fori_loop(0, n_groups, body, zero)
    return jnp.sum(acc, axis=1, keepdims=True)


def _topk_threshold(count, rows, topk, n_keys):
    lane = lax.broadcasted_iota(I32, (rows, LANES), 1)
    thr0 = jnp.where(count(lambda key, j: key >= 0) >= topk, 0, INT_MIN).astype(I32)

    def thr_bit(i, thr):
        cand = thr | (jnp.int32(1) << (30 - i))
        return jnp.where(count(lambda key, j: key >= cand) >= topk, cand, thr)

    thr = lax.fori_loop(0, 31, thr_bit, thr0)
    need = topk - count(lambda key, j: key > thr)
    n_eq = count(lambda key, j: key == thr)
    idx_bits = n_keys.bit_length()
    excess = jnp.max(jnp.where((n_eq > need) & (thr > NEG_INF_KEY), 1.0, 0.0))

    def tie_bound():
        def bound_bit(i, bound):
            cand = bound | (jnp.int32(1) << (idx_bits - 1 - i))
            hits = count(lambda key, j: (key == thr) & (j * LANES + lane < cand))
            return jnp.where(hits <= need, cand, bound)

        return lax.fori_loop(0, idx_bits, bound_bit, jnp.zeros((rows, 1), I32))

    bound = lax.cond(excess > 0.0, tie_bound, lambda: jnp.full((rows, 1), 2 ** idx_bits - 1, I32))
    return thr, bound


def _dsa_prompt_kernel(q_ref, qi_ref, wi_ref, k_ref, v_ref, ki2_ref, o_ref, key_ref, *, tq, rows, kc1, kc, topk,
                       n_keys):
    qb = pl.program_id(1)
    tpc = kc // LANES
    n_chunks = (qb * tq + tq + kc - 1) // kc
    lane = lax.broadcasted_iota(I32, (rows, LANES), 1)
    lane_o = lax.broadcasted_iota(I32, (rows, W_GROUP), 1)
    neg_inf = jnp.float32(-jnp.inf)
    zero_b = jnp.zeros((rows, LANES), BF16)

    for r0 in range(0, tq, rows):
        q_pos1 = qb * tq + r0 + lax.broadcasted_iota(I32, (rows, kc1), 0)
        col1 = lax.broadcasted_iota(I32, (rows, kc1), 1)
        qi = qi_ref[0, r0:r0 + rows, :]
        wi = wi_ref[0, r0:r0 + rows, :]
        qm = []
        for h in range(IDX_HEADS):
            tile = qi[:, (h // 2) * LANES:(h // 2 + 1) * LANES]
            qm.append(jnp.where((lane // IDX_DIM) == (h % 2), tile, zero_b))

        def score_chunk(c, carry, r0=r0, qm=qm, wi=wi, q_pos1=q_pos1, col1=col1):
            ki2 = ki2_ref[0, pl.ds(pl.multiple_of(c * kc1, kc1), kc1), :]
            acc = jnp.zeros((rows, kc1), F32)
            for h in range(IDX_HEADS):
                d = _dot_nt(qm[h], ki2)
                acc = acc + wi[:, h:h + 1] * jnp.maximum(d, 0.0)
            key = _ordered_key(jnp.where(c * kc1 + col1 <= q_pos1, acc, neg_inf))
            for j in range(kc1 // LANES):
                key_ref[c * (kc1 // LANES) + j, r0:r0 + rows, :] = key[:, j * LANES:(j + 1) * LANES]
            return carry

        lax.fori_loop(0, (qb * tq + tq + kc1 - 1) // kc1, score_chunk, 0)

    count = functools.partial(_count_rows, key_ref, None, n_chunks, tpc)
    thr_all, bound_all = _topk_threshold(count, tq, topk, n_keys)

    for r0 in range(0, tq, rows):
        q_pos = qb * tq + r0 + lax.broadcasted_iota(I32, (rows, kc), 0)
        col0 = lax.broadcasted_iota(I32, (rows, kc), 1)
        thr, bound = thr_all[r0:r0 + rows], bound_all[r0:r0 + rows]
        q = q_ref[0, r0:r0 + rows, :]
        qh = []
        for h in range(ATT_HEADS):
            tile = q[:, (h // 2) * LANES:(h // 2 + 1) * LANES]
            qh.append(jnp.where((lane // HEAD_DIM) == (h % 2), tile, zero_b))

        def attend_chunk(c, carry, r0=r0, q_pos=q_pos, col0=col0, thr=thr, bound=bound, qh=qh):
            m_run, l_run, acc = carry
            off = pl.multiple_of(c * kc, kc)
            key = jnp.concatenate([key_ref[c * tpc + j, r0:r0 + rows, :] for j in range(tpc)], axis=1)
            col = c * kc + col0
            sel = ((key > thr) | ((key == thr) & (col < bound))) & (col <= q_pos)
            vb = v_ref[0, pl.ds(off, kc), :]
            m_new, l_new = [], []
            scale_full, add_full = jnp.zeros((rows, W_GROUP), F32), jnp.zeros((rows, W_GROUP), F32)
            for h in range(ATT_HEADS):
                kh = k_ref[0, pl.ds(off, kc), (h // 2) * LANES:(h // 2 + 1) * LANES]
                logit = jnp.where(sel, _dot_nt(qh[h], kh), neg_inf)
                m_h = jnp.maximum(m_run[h], jnp.max(logit, axis=1, keepdims=True))
                m_safe = jnp.where(m_h == neg_inf, 0.0, m_h)
                p = jnp.exp2(logit - m_safe)
                alpha = jnp.exp2(m_run[h] - m_safe)
                l_new.append(alpha * l_run[h] + jnp.sum(p, axis=1, keepdims=True))
                m_new.append(m_h)
                pv = _dot(p.astype(BF16), vb)
                in_head = (lane_o // HEAD_DIM) == h
                scale_full = jnp.where(in_head, alpha, scale_full)
                add_full = jnp.where(in_head, pv, add_full)
            return tuple(m_new), tuple(l_new), acc * scale_full + add_full

        init = (tuple(jnp.full((rows, 1), neg_inf, F32) for _ in range(ATT_HEADS)),
                tuple(jnp.zeros((rows, 1), F32) for _ in range(ATT_HEADS)),
                jnp.zeros((rows, W_GROUP), F32))
        _, l_fin, acc = lax.fori_loop(0, n_chunks, attend_chunk, init)
        inv = jnp.zeros((rows, W_GROUP), F32)
        for h in range(ATT_HEADS):
            inv = jnp.where((lane_o // HEAD_DIM) == h, 1.0 / l_fin[h], inv)
        o_ref[0, r0:r0 + rows, :] = (acc * inv).astype(o_ref.dtype)


def dsa_prompt(q, qi, wi, k, v, ki2):
    b, t_len, _ = q.shape
    tq = min(t_len, 256)
    rows = min(tq, 128)
    kc = min(t_len, 512)
    kc1 = min(t_len, 512)
    topk = min(TOPK_MAX, t_len // 4)
    return pl.pallas_call(
        functools.partial(_dsa_prompt_kernel, tq=tq, rows=rows, kc1=kc1, kc=kc, topk=topk, n_keys=t_len),
        grid=(b, t_len // tq),
        in_specs=[pl.BlockSpec((1, tq, W_GROUP), lambda i, j: (i, j, 0)),
                  pl.BlockSpec((1, tq, 2 * W_GROUP), lambda i, j: (i, j, 0)),
                  pl.BlockSpec((1, tq, LANES), lambda i, j: (i, j, 0)),
                  pl.BlockSpec((1, t_len, W_GROUP), lambda i, j: (i, 0, 0)),
                  pl.BlockSpec((1, t_len, W_GROUP), lambda i, j: (i, 0, 0)),
                  pl.BlockSpec((1, t_len, LANES), lambda i, j: (i, 0, 0))],
        out_specs=pl.BlockSpec((1, tq, W_GROUP), lambda i, j: (i, j, 0)),
        out_shape=jax.ShapeDtypeStruct((b, t_len, W_GROUP), BF16),
        scratch_shapes=[pltpu.VMEM((t_len // LANES, tq, LANES), I32)],
        compiler_params=_cparams(("parallel", "arbitrary")),
        name="dsa_prompt",
    )(q, qi, wi, k, v, ki2)


def _dsa_sample_kernel(*refs, g, t, n_pages, page, topk):
    pt_ref, q_ref, qi_ref, wi_ref, kn_ref, vn_ref, kin_ref = refs[:7]
    n_pg = g * n_pages
    kt_refs, vt_refs, kit_refs = refs[7:7 + n_pg], refs[7 + n_pg:7 + 2 * n_pg], refs[7 + 2 * n_pg:7 + 3 * n_pg]
    o_ref, key_ref = refs[7 + 3 * n_pg], refs[8 + 3 * n_pg]
    del pt_ref
    past = n_pages * page
    s_len = past + LANES
    n_tiles = s_len // LANES
    lane = lax.broadcasted_iota(I32, (t, LANES), 1)
    lane_o = lax.broadcasted_iota(I32, (t, W_GROUP), 1)
    col = lax.broadcasted_iota(I32, (t, s_len), 1)
    causal = col <= past + lax.broadcasted_iota(I32, (t, s_len), 0)
    neg_inf = jnp.float32(-jnp.inf)

    def pad_rows(x):
        return jnp.concatenate([x.astype(F32), jnp.zeros((LANES - t, x.shape[1]), F32)], axis=0).astype(BF16)

    def pages(page_refs, b):
        return jnp.concatenate([page_refs[b * n_pages + p][0, 0] for p in range(n_pages)], axis=1).astype(BF16)

    for b in range(g):
        qi = qi_ref[b].astype(F32)
        lhs = jnp.concatenate(
            [jnp.where((lane // IDX_DIM) == (h % 2), qi[:, (h // 2) * LANES:(h // 2 + 1) * LANES], 0.0)
             for h in range(IDX_HEADS)], axis=0).astype(BF16)
        kit = pages(kit_refs, b)
        dots = jnp.concatenate([_dot(lhs, jnp.concatenate([kit, kit], axis=0)),
                                _dot_nt(lhs, pad_rows(kin_ref[b]))], axis=1)
        wi = wi_ref[b]
        acc = jnp.zeros((t, s_len), F32)
        for h in range(IDX_HEADS):
            acc = acc + wi[:, h:h + 1] * jnp.maximum(dots[h * t:(h + 1) * t], 0.0)
        key = _ordered_key(jnp.where(causal, acc, neg_inf))
        for j in range(n_tiles):
            key_ref[j, b * t:(b + 1) * t, :] = key[:, j * LANES:(j + 1) * LANES]

    count = functools.partial(_count_rows, key_ref, n_tiles, None, None)
    thr, bound = _topk_threshold(count, g * t, topk, s_len)

    for b in range(g):
        rows = slice(b * t, (b + 1) * t)
        key = jnp.concatenate([key_ref[j, rows, :] for j in range(n_tiles)], axis=1)
        sel = ((key > thr[rows]) | ((key == thr[rows]) & (col < bound[rows]))) & causal
        q = q_ref[b].astype(F32)
        lhs = jnp.concatenate([jnp.where((lane_o // HEAD_DIM) == h, q, 0.0) for h in range(ATT_HEADS)],
                              axis=0).astype(BF16)
        logit = jnp.concatenate([_dot(lhs, pages(kt_refs, b)), _dot_nt(lhs, pad_rows(kn_ref[b]))], axis=1)
        logit = jnp.where(jnp.concatenate([sel] * ATT_HEADS, axis=0), logit, neg_inf)
        p = jnp.exp2(logit - jnp.max(logit, axis=1, keepdims=True))
        pb = p.astype(BF16)
        res = _dot_nt(pb[:, :past], pages(vt_refs, b)) + _dot(pb[:, past:], pad_rows(vn_ref[b]))
        res = res / jnp.sum(p, axis=1, keepdims=True)
        out = jnp.zeros((t, W_GROUP), F32)
        for h in range(ATT_HEADS):
            out = jnp.where((lane_o // HEAD_DIM) == h, res[h * t:(h + 1) * t], out)
        o_ref[b] = out.astype(o_ref.dtype)


def dsa_sample(q, qi, wi, k_new, v_new, ki2_new, kt_pages, vt_pages, kit_pages, page_table, layer, *, g=4):
    b, t_len, _ = q.shape
    n_pages = page_table.shape[1]
    page = kt_pages.shape[-1]
    g = math.gcd(g, b)
    topk = min(TOPK_MAX, (n_pages * page + t_len) // 4)

    def spec(width):
        return pl.BlockSpec((g, t_len, width), lambda i, pt: (i, 0, 0))

    def page_specs(rows):
        return [pl.BlockSpec((1, 1, rows, page),
                             lambda i, pt, s=s, p=p: (pt[(i * g + s) * n_pages + p], layer, 0, 0))
                for s in range(g) for p in range(n_pages)]

    grid_spec = pltpu.PrefetchScalarGridSpec(
        num_scalar_prefetch=1,
        grid=(b // g,),
        in_specs=[spec(W_GROUP), spec(2 * W_GROUP), spec(LANES), spec(W_GROUP), spec(W_GROUP), spec(LANES)]
        + page_specs(W_GROUP) + page_specs(W_GROUP) + page_specs(IDX_DIM),
        out_specs=spec(W_GROUP),
        scratch_shapes=[pltpu.VMEM((n_pages + 1, g * t_len, LANES), I32)],
    )
    n_pg = g * n_pages
    return pl.pallas_call(
        functools.partial(_dsa_sample_kernel, g=g, t=t_len, n_pages=n_pages, page=page, topk=topk),
        grid_spec=grid_spec,
        out_shape=jax.ShapeDtypeStruct((b, t_len, W_GROUP), BF16),
        compiler_params=_cparams(("parallel",)),
        name="dsa_sample",
    )(page_table.reshape(-1), q, qi, wi, k_new, v_new, ki2_new,
      *([kt_pages] * n_pg), *([vt_pages] * n_pg), *([kit_pages] * n_pg))


def _mem_attn_kernel(q_ref, mk_ref, mv_ref, o_ref, *, g, split_heads):
    for b in range(g):
        q = q_ref[b]
        outs = []
        for h in range(MEM_HEADS):
            sl = slice(h * MEM_HEAD_DIM, (h + 1) * MEM_HEAD_DIM)
            if split_heads:
                mk, mv = mk_ref[b, 0, :, h, :], mv_ref[b, 0, :, h, :]
            else:
                mk, mv = mk_ref[b, 0, :, sl], mv_ref[b, 0, :, sl]
            logit = _dot_nt(q[:, sl], mk.astype(BF16)) * (MEM_HEAD_DIM ** -0.5)
            m = jnp.max(logit, axis=1, keepdims=True)
            p = jnp.exp(logit - m)
            p = p / jnp.sum(p, axis=1, keepdims=True)
            outs.append(_dot(p.astype(BF16), mv.astype(BF16)))
        o_ref[b] = jnp.concatenate(outs, axis=-1).astype(o_ref.dtype)


def mem_attention(q, mk, mv, layer, *, tq=512, g=4):
    b, t_len, d = q.shape
    m_len = mk.shape[2]
    split_heads = mk.ndim == 5
    if t_len >= tq:
        g = 1
    else:
        tq, g = t_len, math.gcd(g, b)
    mem_block = (g, 1) + mk.shape[2:]
    zeros = (0,) * (mk.ndim - 2)
    mem_spec = pl.BlockSpec(mem_block, lambda i, j: (i, layer) + zeros)
    return pl.pallas_call(
        functools.partial(_mem_attn_kernel, g=g, split_heads=split_heads),
        grid=(b // g, t_len // tq),
        in_specs=[pl.BlockSpec((g, tq, d), lambda i, j: (i, j, 0)), mem_spec, mem_spec],
        out_specs=pl.BlockSpec((g, tq, d), lambda i, j: (i, j, 0)),
        out_shape=jax.ShapeDtypeStruct((b, t_len, d), BF16),
        compiler_params=_cparams(("parallel", "arbitrary")),
        name="mem_attention",
    )(q, mk, mv)


def _prep_layer(l, p):
    w = p['w_in'][l]
    cuts = [0]
    for s in (W_GROUP,) * 9 + (IDX_HEADS * IDX_DIM, IDX_DIM, IDX_HEADS):
        cuts.append(cuts[-1] + s)
    piece = lambda i: w[:, cuts[i]:cuts[i + 1]]
    zeros = lambda n: jnp.zeros((D_MODEL, n), w.dtype)
    w_in = jnp.concatenate(
        [piece(i) for i in range(6)] + [piece(6), piece(7), piece(9), piece(8), piece(10), zeros(LANES - IDX_DIM),
                                        piece(11), zeros(LANES - IDX_HEADS)], axis=1).astype(BF16)
    pool_w = p['pool_w'][l]
    pool_bd = jnp.zeros((W_GROUP, W_GROUP), F32)
    for g in range(len(POOL_WINDOWS)):
        pool_bd = pool_bd.at[g * POOL_CH:(g + 1) * POOL_CH, g * POOL_CH:(g + 1) * POOL_CH].set(pool_w[g])
    return {
        'norm_mix': p['norm_mix'][l], 'w_in': w_in,
        'conv_w': p['conv_w'][l], 'pool_w_bd': pool_bd.astype(BF16),
        'pool_scale': p['pool_scale'][l].reshape(1, W_GROUP), 'chunk_norm': p['chunk_norm'][l].reshape(1, W_GROUP),
        'chunk_ws': p['chunk_ws'][l], 'chunk_b': p['chunk_b'][l],
        'w_out_abc': p['w_out'][l][:3 * W_GROUP].astype(BF16), 'w_out_d': p['w_out'][l][3 * W_GROUP:].astype(BF16),
        'norm_mem': p['norm_mem'][l], 'wq_mem': p['wq_mem'][l].astype(BF16), 'wo_mem': p['wo_mem'][l].astype(BF16),
        'norm_ffn': p['norm_ffn'][l],
    }


def _chunk_consts(lw, chunk):
    mask = jnp.tril(jnp.ones((chunk, chunk), dtype=bool))
    wm = jnp.where(mask[None], lw['chunk_ws'][:, :chunk, :chunk], 0).astype(BF16)
    bias = jnp.repeat(lw['chunk_b'][:, :chunk].T, W_GROUP // CHUNK_HEADS, axis=1)
    return wm, bias


def _rope_tables(pos):
    half = HEAD_DIM // 2
    inv = ROPE_THETA ** (-jnp.arange(half, dtype=F32) / half)
    ang = pos.astype(F32)[:, None] * inv[None, :]
    cos, sin = jnp.cos(ang), jnp.sin(ang)
    cos_t = jnp.concatenate([cos, cos] * (LANES // HEAD_DIM), axis=1)
    sin_t = jnp.concatenate([-sin, sin] * (LANES // HEAD_DIM), axis=1)
    return cos_t, sin_t


def _run_trunk(x, pos0, conv_prev, pool_prev, past, mem_k, mem_v, layers, ffn_w, norm_final):
    b, t_len, d = x.shape
    n = b * t_len
    depth = len(layers)
    from_state = past is not None
    pos = pos0 + jnp.arange(t_len, dtype=I32)
    cos_t, sin_t = _rope_tables(pos)
    h = x.reshape(n, d)
    ks, vs, kis, convs, pools, cvs = [], [], [], [], [], []
    for l, lw in enumerate(layers):
        proj = norm_matmul(h, lw['norm_mix'], lw['w_in']).reshape(b, t_len, P_PAD)
        chunk = min(t_len, CHUNK)
        wm, cbias = _chunk_consts(lw, chunk)
        lw_m = dict(lw, chunk_wm=wm, chunk_bias=cbias)
        outs = mixers(proj, conv_prev[:, l] if from_state else None, pool_prev[:, l] if from_state else None,
                      cos_t, sin_t, lw_m, pos0=pos0, from_state=from_state, emit_cv=from_state)
        mix, q, k, kb, v, vb, qi, ki, ki2, wi, conv_s, pool_s = outs[:12]
        if from_state:
            cvs.append(outs[12])
            yd = dsa_sample(q, qi, wi, kb, vb, ki2, *past, l)
        else:
            yd = dsa_prompt(q, qi, wi, kb, vb, ki2)
        h = matmul_res([mix.reshape(n, 3 * W_GROUP), yd.reshape(n, W_GROUP)], [lw['w_out_abc'], lw['w_out_d']], h)
        qm = norm_matmul(h, lw['norm_mem'], lw['wq_mem'], out_dtype=BF16).reshape(b, t_len, d)
        om = mem_attention(qm, mem_k, mem_v, l)
        h = matmul_res([om.reshape(n, d)], [lw['wo_mem']], h)
        last = l == depth - 1
        fw = ffn_w[l]
        if fw['kind'] == 'dense':
            h = ffn_dense(h, lw['norm_ffn'], fw['w1'], fw['w3'], fw['w2'], norm_final, final_norm=last)
        else:
            h = moe_ffn(h, lw['norm_ffn'], fw['router'], fw['w1'], fw['w3'], fw['w2'], norm_final, final_norm=last)
        ks.append(k.reshape(b, t_len, ATT_HEADS, HEAD_DIM))
        vs.append(v.reshape(b, t_len, ATT_HEADS, HEAD_DIM))
        kis.append(ki[:, :, :IDX_DIM])
        convs.append(conv_s)
        pools.append(pool_s)
    st = lambda xs: jnp.stack(xs, axis=1)
    y = h.reshape(b, t_len, d)
    return y, st(ks), st(vs), st(kis), st(convs), st(pools), (st(cvs) if cvs else None)


def kernel(x_prompt, x_sample, mem_prompt, cache_k, cache_v, cache_kidx, cache_mem_k, cache_mem_v, state_conv, state_pool, page_table, norm_mix, w_in, conv_w, pool_w, pool_scale, chunk_norm, chunk_ws, chunk_b, w_out, norm_mem, wq_mem, wk_mem, wv_mem, wo_mem, norm_ffn, w1_dense, w3_dense, w2_dense, router, w1_moe, w3_moe, w2_moe, norm_final):
    p = {
        'norm_mix': norm_mix, 'w_in': w_in, 'conv_w': conv_w, 'pool_w': pool_w, 'pool_scale': pool_scale,
        'chunk_norm': chunk_norm, 'chunk_ws': chunk_ws, 'chunk_b': chunk_b, 'w_out': w_out, 'norm_mem': norm_mem,
        'wq_mem': wq_mem, 'wo_mem': wo_mem, 'norm_ffn': norm_ffn,
    }
    depth = norm_mix.shape[0]
    layers = [_prep_layer(l, p) for l in range(depth)]
    ffn_w = []
    for l in range(depth):
        j = l // 2
        if l % 2 == 0:
            ffn_w.append({'kind': 'dense', 'w1': w1_dense[j].astype(BF16), 'w3': w3_dense[j].astype(BF16),
                          'w2': w2_dense[j].astype(BF16)})
        else:
            router_pad = jnp.concatenate([router[j], jnp.zeros((D_MODEL, LANES - N_EXPERTS), F32)], axis=1)
            ffn_w.append({'kind': 'moe', 'router': router_pad, 'w1': w1_moe[j].astype(BF16),
                          'w3': w3_moe[j].astype(BF16), 'w2': w2_moe[j].astype(BF16)})

    bp, tp, d = x_prompt.shape
    n_mem = mem_prompt.shape[1]
    w_kv = jnp.concatenate([wk_mem[l] for l in range(depth)] + [wv_mem[l] for l in range(depth)], axis=1).astype(BF16)
    mem_kv = norm_matmul(mem_prompt.reshape(bp * n_mem, d), jnp.ones((d,), F32), w_kv, norm=False)
    mem_kv = mem_kv.reshape(bp, n_mem, 2, depth, d)
    mem_k_p = jnp.moveaxis(mem_kv[:, :, 0], 2, 1)
    mem_v_p = jnp.moveaxis(mem_kv[:, :, 1], 2, 1)
    y_p, k_p, v_p, ki_p, conv_p, pool_p, _ = _run_trunk(
        x_prompt, 0, None, None, None, mem_k_p, mem_v_p, layers, ffn_w, norm_final)

    bs, ts, _ = x_sample.shape
    n_pages, page = page_table.shape[1], cache_k.shape[2]
    past_len = n_pages * page

    n_phys = cache_k.shape[0]
    kt_pages = jnp.transpose(cache_k, (0, 1, 3, 4, 2)).reshape(n_phys, depth, W_GROUP, page)
    vt_pages = jnp.transpose(cache_v, (0, 1, 3, 4, 2)).reshape(n_phys, depth, W_GROUP, page)
    kit_pages = jnp.transpose(cache_kidx, (0, 1, 3, 2))
    past = (kt_pages, vt_pages, kit_pages, page_table)
    y_s, k_s, v_s, ki_s, conv_s, pool_s, cv_s = _run_trunk(
        x_sample, past_len, state_conv, state_pool, past, cache_mem_k, cache_mem_v, layers, ffn_w, norm_final)

    mem_shape = (bp, depth, n_mem, MEM_HEADS, MEM_HEAD_DIM)
    return (y_p, y_s, k_p, v_p, ki_p, mem_k_p.reshape(mem_shape), mem_v_p.reshape(mem_shape), conv_p, pool_p,
            k_s, v_s, ki_s, conv_s, pool_s, cv_s)
```

```python
import functools
import math

import jax
import jax.numpy as jnp
from jax import lax
from jax.experimental import pallas as pl
from jax.experimental.pallas import tpu as pltpu

F32 = jnp.float32
BF16 = jnp.bfloat16
I32 = jnp.int32

D_MODEL = 1024
W_GROUP = 256
CONV_W = 3
POOL_WINDOWS = (2, 4, 8, 16)
POOL_CH = 64
POOL_BUF = 15
CHUNK = 128
CHUNK_HEADS = 4
ATT_HEADS = 4
HEAD_DIM = 64
IDX_HEADS = 8
IDX_DIM = 64
IDX_W_SCALE = (IDX_HEADS ** -0.5) * (IDX_DIM ** -0.5)
TOPK_MAX = 256
ROPE_THETA = 10000.0
MEM_HEADS = 4
MEM_HEAD_DIM = 256
D_FF = 3584
N_EXPERTS = 8
EPS = 1e-6

LANES = 128
SUBLANES = 8
PREV_ROWS = 16
P_PAD = 3072
HALF = P_PAD // 2
OFF_Q, OFF_K, OFF_QI, OFF_V, OFF_KI, OFF_WI = 1536, 1792, 2048, 2560, 2816, 2944
VMEM_LIMIT = 56 * 1024 * 1024

INT_MIN = -2 ** 31
Q_SCALE = (HEAD_DIM ** -0.5) * math.log2(math.e)


def _cparams(sem):
    return pltpu.CompilerParams(dimension_semantics=sem, vmem_limit_bytes=VMEM_LIMIT)


def _rms(x, g):
    ms = jnp.mean(x * x, axis=-1, keepdims=True)
    return x * lax.rsqrt(ms + EPS) * g


def _dot(a, b):
    return jnp.dot(a, b, preferred_element_type=F32)


def _dot_nt(a, b):
    return lax.dot_general(a, b, (((1,), (1,)), ((), ())), preferred_element_type=F32)


def _norm_matmul_kernel(x_ref, g_ref, w_ref, o_ref, xn_ref, *, norm):
    @pl.when(pl.program_id(1) == 0)
    def _():
        x = x_ref[...]
        if norm:
            x = _rms(x, g_ref[...])
        xn_ref[...] = x.astype(BF16)

    o_ref[...] = _dot(xn_ref[...], w_ref[...]).astype(o_ref.dtype)


def norm_matmul(x, g, w, *, norm=True, out_dtype=F32, tm=512, tn=1024):
    m, k = x.shape
    n = w.shape[1]
    tm = min(tm, m)
    tn = min(tn, n)
    return pl.pallas_call(
        functools.partial(_norm_matmul_kernel, norm=norm),
        grid=(m // tm, n // tn),
        in_specs=[pl.BlockSpec((tm, k), lambda i, j: (i, 0)),
                  pl.BlockSpec((1, k), lambda i, j: (0, 0)),
                  pl.BlockSpec((k, tn), lambda i, j: (0, j))],
        out_specs=pl.BlockSpec((tm, tn), lambda i, j: (i, j)),
        out_shape=jax.ShapeDtypeStruct((m, n), out_dtype),
        scratch_shapes=[pltpu.VMEM((tm, k), BF16)],
        compiler_params=_cparams(("parallel", "arbitrary")),
        name="norm_matmul",
    )(x, g.reshape(1, k), w)


def _matmul_res_kernel(*refs, n_in):
    a_refs, w_refs = refs[:n_in], refs[n_in:2 * n_in]
    r_ref, o_ref = refs[2 * n_in], refs[2 * n_in + 1]
    acc = r_ref[...]
    for a, w in zip(a_refs, w_refs):
        acc = acc + _dot(a[...].astype(BF16), w[...])
    o_ref[...] = acc


def matmul_res(a_list, w_list, res, *, tm=512):
    m, n = res.shape
    tm = min(tm, m)
    n_in = len(a_list)
    in_specs = [pl.BlockSpec((tm, a.shape[1]), lambda i: (i, 0)) for a in a_list]
    in_specs += [pl.BlockSpec(w.shape, lambda i: (0, 0)) for w in w_list]
    in_specs += [pl.BlockSpec((tm, n), lambda i: (i, 0))]
    return pl.pallas_call(
        functools.partial(_matmul_res_kernel, n_in=n_in),
        grid=(m // tm,),
        in_specs=in_specs,
        out_specs=pl.BlockSpec((tm, n), lambda i: (i, 0)),
        out_shape=jax.ShapeDtypeStruct((m, n), F32),
        compiler_params=_cparams(("parallel",)),
        name="matmul_res",
    )(*a_list, *w_list, res)


def _ffn_kernel(x_ref, g_ref, w1_ref, w3_ref, w2_ref, gf_ref, o_ref, xn_ref, acc_ref, *, final_norm):
    f = pl.program_id(1)

    @pl.when(f == 0)
    def _():
        xn_ref[...] = _rms(x_ref[...], g_ref[...]).astype(BF16)
        acc_ref[...] = jnp.zeros_like(acc_ref)

    xn = xn_ref[...]
    a = _dot(xn, w1_ref[...])
    b = _dot(xn, w3_ref[...])
    hidden = (a * jax.nn.sigmoid(a) * b).astype(BF16)
    acc_ref[...] += _dot(hidden, w2_ref[...])

    @pl.when(f == pl.num_programs(1) - 1)
    def _():
        y = x_ref[...] + acc_ref[...]
        if final_norm:
            y = _rms(y, gf_ref[...])
        o_ref[...] = y


def ffn_dense(x, g, w1, w3, w2, g_final, *, final_norm, tm=512, tf=512):
    m, d = x.shape
    ff = w1.shape[1]
    tm = min(tm, m)
    return pl.pallas_call(
        functools.partial(_ffn_kernel, final_norm=final_norm),
        grid=(m // tm, ff // tf),
        in_specs=[pl.BlockSpec((tm, d), lambda i, f: (i, 0)),
                  pl.BlockSpec((1, d), lambda i, f: (0, 0)),
                  pl.BlockSpec((d, tf), lambda i, f: (0, f)),
                  pl.BlockSpec((d, tf), lambda i, f: (0, f)),
                  pl.BlockSpec((tf, d), lambda i, f: (f, 0)),
                  pl.BlockSpec((1, d), lambda i, f: (0, 0))],
        out_specs=pl.BlockSpec((tm, d), lambda i, f: (i, 0)),
        out_shape=jax.ShapeDtypeStruct((m, d), F32),
        scratch_shapes=[pltpu.VMEM((tm, d), BF16), pltpu.VMEM((tm, d), F32)],
        compiler_params=_cparams(("parallel", "arbitrary")),
        name="ffn_dense",
    )(x, g.reshape(1, d), w1, w3, w2, g_final.reshape(1, d))


def _router_kernel(x_ref, g_ref, r_ref, hn_ref, e_ref, gate_ref):
    hn = _rms(x_ref[...], g_ref[...])
    hn_ref[...] = hn.astype(BF16)
    logits = jnp.dot(hn, r_ref[...], preferred_element_type=F32, precision=lax.Precision.HIGHEST)
    lane = lax.broadcasted_iota(I32, logits.shape, 1)
    neg = jnp.float32(-jnp.inf)
    l1 = jnp.where(lane < N_EXPERTS, logits, neg)
    m1 = jnp.max(l1, axis=1, keepdims=True)
    i1 = jnp.min(jnp.where(l1 == m1, lane, LANES), axis=1, keepdims=True)
    l2 = jnp.where(lane == i1, neg, l1)
    m2 = jnp.max(l2, axis=1, keepdims=True)
    i2 = jnp.min(jnp.where(l2 == m2, lane, LANES), axis=1, keepdims=True)
    t = jnp.exp(m2 - m1)
    denom = 1.0 + t
    e_ref[...] = jnp.where(lane == 0, i1, jnp.where(lane == 1, i2, 0))
    gate_ref[...] = jnp.where(lane == 0, 1.0 / denom, jnp.where(lane == 1, t / denom, 0.0))


def moe_router(x, g, router_pad, *, tm=512):
    m, d = x.shape
    tm = min(tm, m)
    return pl.pallas_call(
        _router_kernel,
        grid=(m // tm,),
        in_specs=[pl.BlockSpec((tm, d), lambda i: (i, 0)),
                  pl.BlockSpec((1, d), lambda i: (0, 0)),
                  pl.BlockSpec((d, LANES), lambda i: (0, 0))],
        out_specs=[pl.BlockSpec((tm, d), lambda i: (i, 0)),
                   pl.BlockSpec((tm, LANES), lambda i: (i, 0)),
                   pl.BlockSpec((tm, LANES), lambda i: (i, 0))],
        out_shape=[jax.ShapeDtypeStruct((m, d), BF16),
                   jax.ShapeDtypeStruct((m, LANES), I32),
                   jax.ShapeDtypeStruct((m, LANES), F32)],
        compiler_params=_cparams(("parallel",)),
        name="moe_router",
    )(x, g.reshape(1, d), router_pad)


def _expert_kernel(be_ref, nb_ref, xs_ref, gate_ref, w1_ref, w3_ref, w2_ref, o_ref, acc_ref):
    j, f = pl.program_id(0), pl.program_id(1)
    live = j < nb_ref[0]

    @pl.when(f == 0)
    def _():
        acc_ref[...] = jnp.zeros_like(acc_ref)

    @pl.when(live)
    def _():
        xs = xs_ref[...]
        a = _dot(xs, w1_ref[0])
        b = _dot(xs, w3_ref[0])
        hidden = (a * jax.nn.sigmoid(a) * b).astype(BF16)
        acc_ref[...] += _dot(hidden, w2_ref[0])

    @pl.when(f == pl.num_programs(1) - 1)
    def _():
        o_ref[...] = acc_ref[...] * gate_ref[...]


def moe_experts(block_e, n_live, xs, slot_gate, w1, w3, w2, *, bm, tf=512):
    n_slots, d = xs.shape
    ff = w1.shape[2]
    n_blocks = n_slots // bm
    grid_spec = pltpu.PrefetchScalarGridSpec(
        num_scalar_prefetch=2,
        grid=(n_blocks, ff // tf),
        in_specs=[pl.BlockSpec((bm, d), lambda j, f, be, nb: (j, 0)),
                  pl.BlockSpec((bm, 1), lambda j, f, be, nb: (j, 0)),
                  pl.BlockSpec((1, d, tf), lambda j, f, be, nb: (be[j], 0, f)),
                  pl.BlockSpec((1, d, tf), lambda j, f, be, nb: (be[j], 0, f)),
                  pl.BlockSpec((1, tf, d), lambda j, f, be, nb: (be[j], f, 0))],
        out_specs=pl.BlockSpec((bm, d), lambda j, f, be, nb: (j, 0)),
        scratch_shapes=[pltpu.VMEM((bm, d), F32)],
    )
    return pl.pallas_call(
        _expert_kernel,
        grid_spec=grid_spec,
        out_shape=jax.ShapeDtypeStruct((n_slots, d), F32),
        compiler_params=_cparams(("parallel", "arbitrary")),
        name="moe_experts",
    )(block_e, n_live, xs, slot_gate, w1, w3, w2)


def _combine_kernel(x_ref, a_ref, b_ref, gf_ref, o_ref, *, final_norm):
    y = x_ref[...] + (a_ref[...] + b_ref[...])
    if final_norm:
        y = _rms(y, gf_ref[...])
    o_ref[...] = y


def moe_combine(x, ya, yb, g_final, *, final_norm, tm=512):
    m, d = x.shape
    tm = min(tm, m)
    spec = pl.BlockSpec((tm, d), lambda i: (i, 0))
    return pl.pallas_call(
        functools.partial(_combine_kernel, final_norm=final_norm),
        grid=(m // tm,),
        in_specs=[spec, spec, spec, pl.BlockSpec((1, d), lambda i: (0, 0))],
        out_specs=spec,
        out_shape=jax.ShapeDtypeStruct((m, d), F32),
        compiler_params=_cparams(("parallel",)),
        name="moe_combine",
    )(x, ya, yb, g_final.reshape(1, d))


def moe_ffn(x, g, router_pad, w1, w3, w2, g_final, *, final_norm):
    n, d = x.shape
    bm = 512 if n >= 4096 else 128
    hn, e12, g12 = moe_router(x, g, router_pad)
    e_flat = e12[:, :2].reshape(-1)
    g_flat = g12[:, :2].reshape(-1)
    n_assign = 2 * n
    order = jnp.argsort(e_flat, stable=True).astype(I32)
    e_sorted = e_flat[order]
    counts = jnp.sum(e_flat[:, None] == jnp.arange(N_EXPERTS, dtype=I32)[None, :], axis=0).astype(I32)
    padded = (counts + bm - 1) // bm * bm
    start = jnp.cumsum(counts) - counts
    pend = jnp.cumsum(padded)
    pstart = pend - padded
    slot = pstart[e_sorted] + (jnp.arange(n_assign, dtype=I32) - start[e_sorted])
    n_blocks = -(-n_assign // bm) + N_EXPERTS
    n_slots = n_blocks * bm
    slot_tok = jnp.zeros((n_slots,), I32).at[slot].set(order // 2)
    slot_gate = jnp.zeros((n_slots,), F32).at[slot].set(g_flat[order])
    assign_slot = jnp.zeros((n_assign,), I32).at[order].set(slot)
    block_e = jnp.minimum(
        jnp.searchsorted(pend, jnp.arange(n_blocks, dtype=I32) * bm, side='right'), N_EXPERTS - 1).astype(I32)
    n_live = (pend[-1] // bm).astype(I32).reshape(1)
    xs = hn[slot_tok]
    ys = moe_experts(block_e, n_live, xs, slot_gate.reshape(n_slots, 1), w1, w3, w2, bm=bm)
    pair = assign_slot.reshape(n, 2)
    return moe_combine(x, ys[pair[:, 0]], ys[pair[:, 1]], g_final, final_norm=final_norm)


def _swap_half_heads(x):
    n = x.shape[-1]
    lane = lax.broadcasted_iota(I32, x.shape, x.ndim - 1)
    fwd = pltpu.roll(x, n - HEAD_DIM // 2, x.ndim - 1)
    bwd = pltpu.roll(x, HEAD_DIM // 2, x.ndim - 1)
    return jnp.where(lane % HEAD_DIM < HEAD_DIM // 2, fwd, bwd)


def _rope(x, cos, sin):
    reps = x.shape[-1] // LANES
    if reps > 1:
        cos = jnp.concatenate([cos] * reps, axis=-1)
        sin = jnp.concatenate([sin] * reps, axis=-1)
    return x * cos + _swap_half_heads(x) * sin


def _mixer_kernel(*refs, tt, chunk, pos0, from_state, emit_cv):
    it = iter(refs)
    xa_ref, xb_ref = next(it), next(it)
    if from_state:
        convp_ref, poolp_ref = next(it), next(it)
    else:
        prev_ref = next(it)
    cos_ref, sin_ref = next(it), next(it)
    convw_ref, poolw_ref, pscale_ref, cnorm_ref, wm_ref, cbias_ref = (next(it) for _ in range(6))
    mix_ref, q_ref, k_ref, kb_ref, v_ref, vb_ref, qi_ref, ki_ref, ki2_ref, wi_ref, convs_ref, pools_ref = (
        next(it) for _ in range(12))
    cv_ref = next(it) if emit_cv else None
    ybuf, xbuf = next(it), next(it)

    t = pl.program_id(1)
    xa = xa_ref[0]
    a_in, a_b, a_c = xa[:, 0:256], xa[:, 256:512], xa[:, 512:768]
    b_in, c_u, c_v = xa[:, 768:1024], xa[:, 1024:1280], xa[:, 1280:1536]

    y = a_c * a_in
    if from_state:
        ybuf[PREV_ROWS - 2:PREV_ROWS, :] = convp_ref[0]
        xbuf[0:1, :] = jnp.zeros((1, W_GROUP), F32)
        xbuf[1:PREV_ROWS, :] = poolp_ref[0]
    else:
        first = (t == 0).astype(F32)
        prev = prev_ref[0] * (1.0 - first)
        ybuf[0:PREV_ROWS, :] = prev[:, 512:768] * prev[:, 0:256]
        xbuf[0:PREV_ROWS, :] = prev[:, 768:1024]
    ybuf[PREV_ROWS:PREV_ROWS + tt, :] = y
    xbuf[PREV_ROWS:PREV_ROWS + tt, :] = b_in

    cw = convw_ref[...]
    z = cw[0:1, :] * ybuf[PREV_ROWS - 2:PREV_ROWS - 2 + tt, :]
    z = z + cw[1:2, :] * ybuf[PREV_ROWS - 1:PREV_ROWS - 1 + tt, :]
    z = z + cw[2:3, :] * y
    ya = a_b * z
    convs_ref[0] = ybuf[PREV_ROWS + tt - 2:PREV_ROWS + tt, :]

    lane = lax.broadcasted_iota(I32, (tt, W_GROUP), 1)
    row = lax.broadcasted_iota(I32, (tt, W_GROUP), 0)
    pos = (pos0 + t * tt + row).astype(F32)
    run = b_in
    mean = jnp.zeros((tt, W_GROUP), F32)
    for j in range(1, POOL_WINDOWS[-1] + 1):
        if j > 1:
            run = run + xbuf[PREV_ROWS - (j - 1):PREV_ROWS - (j - 1) + tt, :]
        if j in POOL_WINDOWS:
            grp = POOL_WINDOWS.index(j)
            cnt = jnp.minimum(pos + 1.0, float(j))
            mean = jnp.where(lane // POOL_CH == grp, run / cnt, mean)
    dlt = (mean - b_in).astype(BF16)
    yb = _dot(dlt, poolw_ref[...]) * pscale_ref[...]
    pools_ref[0] = xbuf[PREV_ROWS + tt - POOL_BUF:PREV_ROWS + tt, :]

    vn = _rms(c_v, cnorm_ref[...])
    if emit_cv:
        cv_ref[0] = vn
    vnb = vn.astype(BF16)
    lane_c = lax.broadcasted_iota(I32, (chunk, W_GROUP), 1)
    parts = []
    for c in range(tt // chunk):
        vc = vnb[c * chunk:(c + 1) * chunk, :]
        s = cbias_ref[...]
        for hd in range(CHUNK_HEADS):
            sh = _dot(wm_ref[hd], vc)
            s = s + jnp.where(lane_c // (W_GROUP // CHUNK_HEADS) == hd, sh, 0.0)
        parts.append(s)
    s_all = parts[0] if len(parts) == 1 else jnp.concatenate(parts, axis=0)
    yc = c_u * s_all

    mix_ref[0] = jnp.concatenate([ya, yb, yc], axis=-1).astype(BF16)

    xb = xb_ref[0]
    cos, sin = cos_ref[...], sin_ref[...]
    q = _rope(xb[:, 0:256], cos, sin)
    k = _rope(xb[:, 256:512], cos, sin)
    qi = _rope(xb[:, 512:1024], cos, sin)
    v = xb[:, 1024:1280]
    ki = _rope(xb[:, 1280:1408], cos, sin)
    wi = xb[:, 1408:1536] * IDX_W_SCALE
    q_ref[0] = (q * Q_SCALE).astype(BF16)
    k_ref[0] = k
    kb_ref[0] = k.astype(BF16)
    v_ref[0] = v
    if from_state:
        vb_ref[0] = v.astype(BF16)
    else:
        vb_ref[0, 0] = v.T.astype(BF16)
    qi_ref[0] = qi.astype(BF16)
    ki_ref[0] = ki
    lane128 = lax.broadcasted_iota(I32, ki.shape, 1)
    ki2_ref[0] = jnp.where(lane128 < IDX_DIM, ki, pltpu.roll(ki, IDX_DIM, 1)).astype(BF16)
    wi_ref[0] = wi


def mixers(proj, conv_prev, pool_prev, cos, sin, lw, *, pos0, from_state, emit_cv):
    b, t_len, _ = proj.shape
    tt = min(t_len, 256)
    chunk = min(t_len, CHUNK)
    nt = t_len // tt
    in_specs = [pl.BlockSpec((1, tt, HALF), lambda i, t: (i, t, 0)),
                pl.BlockSpec((1, tt, HALF), lambda i, t: (i, t, 1))]
    args = [proj, proj]
    if from_state:
        in_specs += [pl.BlockSpec((1, CONV_W - 1, W_GROUP), lambda i, t: (i, 0, 0)),
                     pl.BlockSpec((1, POOL_BUF, W_GROUP), lambda i, t: (i, 0, 0))]
        args += [conv_prev, pool_prev]
    else:
        per = tt // PREV_ROWS
        in_specs += [pl.BlockSpec((1, PREV_ROWS, HALF), lambda i, t: (i, jnp.maximum(t * per - 1, 0), 0))]
        args += [proj]
    in_specs += [pl.BlockSpec((tt, LANES), lambda i, t: (t, 0)),
                 pl.BlockSpec((tt, LANES), lambda i, t: (t, 0))]
    args += [cos, sin]
    consts = [lw['conv_w'], lw['pool_w_bd'], lw['pool_scale'], lw['chunk_norm'], lw['chunk_wm'], lw['chunk_bias']]
    for c in consts:
        in_specs.append(pl.BlockSpec(c.shape, lambda i, t, nd=c.ndim: (0,) * nd))
    args += consts

    def tok(width, dtype):
        return (pl.BlockSpec((1, tt, width), lambda i, t: (i, t, 0)), jax.ShapeDtypeStruct((b, t_len, width), dtype))

    outs = [tok(768, BF16),
            tok(256, BF16),
            tok(256, F32), tok(256, BF16),
            tok(256, F32),
            tok(256, BF16) if from_state else
            (pl.BlockSpec((1, 1, W_GROUP, tt), lambda i, t: (i, t, 0, 0)),
             jax.ShapeDtypeStruct((b, nt, W_GROUP, tt), BF16)),
            tok(512, BF16),
            tok(LANES, F32),
            tok(LANES, BF16),
            tok(LANES, F32),
            (pl.BlockSpec((1, CONV_W - 1, W_GROUP), lambda i, t: (i, 0, 0)),
             jax.ShapeDtypeStruct((b, CONV_W - 1, W_GROUP), F32)),
            (pl.BlockSpec((1, POOL_BUF, W_GROUP), lambda i, t: (i, 0, 0)),
             jax.ShapeDtypeStruct((b, POOL_BUF, W_GROUP), F32))]
    if emit_cv:
        outs.append(tok(256, F32))
    return pl.pallas_call(
        functools.partial(_mixer_kernel, tt=tt, chunk=chunk, pos0=pos0, from_state=from_state, emit_cv=emit_cv),
        grid=(b, nt),
        in_specs=in_specs,
        out_specs=[o[0] for o in outs],
        out_shape=[o[1] for o in outs],
        scratch_shapes=[pltpu.VMEM((PREV_ROWS + tt, W_GROUP), F32), pltpu.VMEM((PREV_ROWS + tt, W_GROUP), F32)],
        compiler_params=_cparams(("parallel", "arbitrary")),
        name="mixers",
    )(*args)


def _ordered_key(x):
    x = jnp.where(x == 0.0, 0.0, x)
    bits = pltpu.bitcast(x, I32)
    return bits ^ ((bits >> 31) & jnp.int32(0x7FFFFFFF))


NEG_INF_KEY = -2139095041


def _count_rows(key_ref, n_tiles, pred):
    rows = key_ref.shape[1]
    lane = lax.broadcasted_iota(I32, (rows, LANES), 1)
    acc = jnp.zeros((rows, LANES), F32)
    for j in range(n_tiles):
        acc = acc + jnp.where(pred(key_ref[j], j * LANES + lane), 1.0, 0.0)
    return jnp.sum(acc, axis=1, keepdims=True)


def _count_cols(key_ref, n_groups, gsz, pred):
    cols = key_ref.shape[1]
    pos0 = lax.broadcasted_iota(I32, (gsz, cols), 0)

    def body(c, acc):
        blk = key_ref[pl.ds(pl.multiple_of(c * gsz, gsz), gsz), :]
        hit = jnp.where(pred(blk, c * gsz + pos0), 1.0, 0.0)
        return acc + jnp.sum(hit.reshape(gsz // SUBLANES, SUBLANES, cols), axis=0)

    acc = lax.fori_loop(0, n_groups, body, jnp.zeros((SUBLANES, cols), F32))
    return jnp.sum(acc, axis=0, keepdims=True)


def _topk_threshold(count, shape, topk, n_keys):
    thr0 = jnp.where(count(lambda key, pos: key >= 0) >= topk, 0, INT_MIN).astype(I32)

    def thr_bit(i, thr):
        cand = thr | (jnp.int32(1) << (30 - i))
        return jnp.where(count(lambda key, pos: key >= cand) >= topk, cand, thr)

    thr = lax.fori_loop(0, 31, thr_bit, thr0)
    need = topk - count(lambda key, pos: key > thr)
    n_eq = count(lambda key, pos: key == thr)
    idx_bits = n_keys.bit_length()
    excess = jnp.max(jnp.where((n_eq > need) & (thr > NEG_INF_KEY), 1.0, 0.0))

    def tie_bound():
        def bound_bit(i, bound):
            cand = bound | (jnp.int32(1) << (idx_bits - 1 - i))
            hits = count(lambda key, pos: (key == thr) & (pos < cand))
            return jnp.where(hits <= need, cand, bound)

        return lax.fori_loop(0, idx_bits, bound_bit, jnp.zeros(shape, I32))

    bound = lax.cond(excess > 0.0, tie_bound, lambda: jnp.full(shape, 2 ** idx_bits - 1, I32))
    return thr, bound


def _dsa_prompt_kernel(q_ref, qi_ref, wi_ref, k_ref, vt_ref, ki2_ref, o_ref, key_ref, *acc_refs, tq, kc1, kc, topk,
                       n_keys):
    qb = pl.program_id(1)
    n_chunks = (qb * tq + tq + kc - 1) // kc
    neg_inf = jnp.float32(-jnp.inf)
    q_pos = qb * tq + lax.broadcasted_iota(I32, (kc, tq), 1)
    pos0 = lax.broadcasted_iota(I32, (kc, tq), 0)
    sub = lax.broadcasted_iota(I32, (LANES, tq), 0)

    q_t = q_ref[0].astype(F32).T
    qi_t = qi_ref[0].astype(F32).T
    w_t = wi_ref[0].T

    def head_tile(x_t, h, dim):
        tile = x_t[(h // 2) * LANES:(h // 2 + 1) * LANES]
        return jnp.where((sub // dim) == (h % 2), tile, 0.0).astype(BF16)

    qim = [head_tile(qi_t, h, IDX_DIM) for h in range(IDX_HEADS)]
    qhm = [head_tile(q_t, h, HEAD_DIM) for h in range(ATT_HEADS)]

    q_pos1 = qb * tq + lax.broadcasted_iota(I32, (kc1, tq), 1)
    pos1 = lax.broadcasted_iota(I32, (kc1, tq), 0)

    def score_chunk(c, carry):
        off = pl.multiple_of(c * kc1, kc1)
        ki2 = ki2_ref[0, pl.ds(off, kc1), :]
        acc = jnp.zeros((kc1, tq), F32)
        for h in range(IDX_HEADS):
            acc = acc + w_t[h:h + 1] * jnp.maximum(_dot(ki2, qim[h]), 0.0)
        key_ref[pl.ds(off, kc1), :] = _ordered_key(jnp.where(c * kc1 + pos1 <= q_pos1, acc, neg_inf))
        return carry

    lax.fori_loop(0, (qb * tq + tq + kc1 - 1) // kc1, score_chunk, 0)

    count = functools.partial(_count_cols, key_ref, n_chunks, kc)
    thr, bound = _topk_threshold(count, (1, tq), topk, n_keys)

    for acc_ref in acc_refs:
        acc_ref[...] = jnp.zeros_like(acc_ref)

    def attend_chunk(c, carry):
        m_run, l_run = carry
        off = pl.multiple_of(c * kc, kc)
        key = key_ref[pl.ds(off, kc), :]
        pos = c * kc + pos0
        sel = ((key > thr) | ((key == thr) & (pos < bound))) & (pos <= q_pos)
        logits = []
        for h in range(ATT_HEADS):
            kh = k_ref[0, pl.ds(off, kc), (h // 2) * LANES:(h // 2 + 1) * LANES]
            logits.append(_dot(kh, qhm[h]))
        m_new, l_new, probs, alphas = [], [], [], []
        for h in range(ATT_HEADS):
            logit = jnp.where(sel, logits[h], neg_inf)
            m_h = jnp.maximum(m_run[h], jnp.max(logit, axis=0, keepdims=True))
            m_safe = jnp.where(m_h == neg_inf, 0.0, m_h)
            p = jnp.exp2(logit - m_safe)
            alpha = jnp.exp2(m_run[h] - m_safe)
            l_new.append(alpha * l_run[h] + jnp.sum(p, axis=0, keepdims=True))
            m_new.append(m_h)
            probs.append(p.astype(BF16))
            alphas.append(alpha)
        for h in range(ATT_HEADS):
            pv = _dot(vt_ref[0, c, h * HEAD_DIM:(h + 1) * HEAD_DIM, :], probs[h])
            acc_refs[h][...] = acc_refs[h][...] * alphas[h] + pv
        return tuple(m_new), tuple(l_new)

    init = (tuple(jnp.full((1, tq), neg_inf, F32) for _ in range(ATT_HEADS)),
            tuple(jnp.zeros((1, tq), F32) for _ in range(ATT_HEADS)))
    _, l_fin = lax.fori_loop(0, n_chunks, attend_chunk, init)
    out_t = jnp.concatenate([acc_refs[h][...] * (1.0 / l_fin[h]) for h in range(ATT_HEADS)], axis=0)
    o_ref[0] = out_t.T.astype(o_ref.dtype)


def dsa_prompt(q, qi, wi, k, vt, ki2):
    b, t_len, _ = q.shape
    kc = vt.shape[-1]
    tq = kc
    kc1 = math.gcd(t_len, 2 * kc)
    topk = min(TOPK_MAX, t_len // 4)
    return pl.pallas_call(
        functools.partial(_dsa_prompt_kernel, tq=tq, kc1=kc1, kc=kc, topk=topk, n_keys=t_len),
        grid=(b, t_len // tq),
        in_specs=[pl.BlockSpec((1, tq, W_GROUP), lambda i, j: (i, j, 0)),
                  pl.BlockSpec((1, tq, 2 * W_GROUP), lambda i, j: (i, j, 0)),
                  pl.BlockSpec((1, tq, LANES), lambda i, j: (i, j, 0)),
                  pl.BlockSpec((1, t_len, W_GROUP), lambda i, j: (i, 0, 0)),
                  pl.BlockSpec((1, t_len // kc, W_GROUP, kc), lambda i, j: (i, 0, 0, 0)),
                  pl.BlockSpec((1, t_len, LANES), lambda i, j: (i, 0, 0))],
        out_specs=pl.BlockSpec((1, tq, W_GROUP), lambda i, j: (i, j, 0)),
        out_shape=jax.ShapeDtypeStruct((b, t_len, W_GROUP), BF16),
        scratch_shapes=[pltpu.VMEM((t_len, tq), I32)] + [pltpu.VMEM((HEAD_DIM, tq), F32)] * ATT_HEADS,
        compiler_params=_cparams(("parallel", "arbitrary")),
        name="dsa_prompt",
    )(q, qi, wi, k, vt, ki2)


def _dsa_sample_kernel(*refs, g, t, n_pages, page, topk):
    pt_ref, q_ref, qi_ref, wi_ref, kn_ref, vn_ref, kin_ref = refs[:7]
    n_pg = g * n_pages
    kt_refs, vt_refs, kit_refs = refs[7:7 + n_pg], refs[7 + n_pg:7 + 2 * n_pg], refs[7 + 2 * n_pg:7 + 3 * n_pg]
    o_ref, key_ref = refs[7 + 3 * n_pg], refs[8 + 3 * n_pg]
    del pt_ref
    past = n_pages * page
    s_len = past + LANES
    n_tiles = s_len // LANES
    lane = lax.broadcasted_iota(I32, (t, LANES), 1)
    lane_o = lax.broadcasted_iota(I32, (t, W_GROUP), 1)
    col = lax.broadcasted_iota(I32, (t, s_len), 1)
    causal = col <= past + lax.broadcasted_iota(I32, (t, s_len), 0)
    neg_inf = jnp.float32(-jnp.inf)

    def pad_rows(x):
        return jnp.concatenate([x.astype(F32), jnp.zeros((LANES - t, x.shape[1]), F32)], axis=0).astype(BF16)

    def pages(page_refs, b):
        return jnp.concatenate([page_refs[b * n_pages + p][0, 0] for p in range(n_pages)], axis=1).astype(BF16)

    for b in range(g):
        qi = qi_ref[b].astype(F32)
        lhs = jnp.concatenate(
            [jnp.where((lane // IDX_DIM) == (h % 2), qi[:, (h // 2) * LANES:(h // 2 + 1) * LANES], 0.0)
             for h in range(IDX_HEADS)], axis=0).astype(BF16)
        kit = pages(kit_refs, b)
        dots = jnp.concatenate([_dot(lhs, jnp.concatenate([kit, kit], axis=0)),
                                _dot_nt(lhs, pad_rows(kin_ref[b]))], axis=1)
        wi = wi_ref[b]
        acc = jnp.zeros((t, s_len), F32)
        for h in range(IDX_HEADS):
            acc = acc + wi[:, h:h + 1] * jnp.maximum(dots[h * t:(h + 1) * t], 0.0)
        key = _ordered_key(jnp.where(causal, acc, neg_inf))
        for j in range(n_tiles):
            key_ref[j, b * t:(b + 1) * t, :] = key[:, j * LANES:(j + 1) * LANES]

    count = functools.partial(_count_rows, key_ref, n_tiles)
    thr, bound = _topk_threshold(count, (g * t, 1), topk, s_len)

    for b in range(g):
        rows = slice(b * t, (b + 1) * t)
        key = jnp.concatenate([key_ref[j, rows, :] for j in range(n_tiles)], axis=1)
        sel = ((key > thr[rows]) | ((key == thr[rows]) & (col < bound[rows]))) & causal
        q = q_ref[b].astype(F32)
        lhs = jnp.concatenate([jnp.where((lane_o // HEAD_DIM) == h, q, 0.0) for h in range(ATT_HEADS)],
                              axis=0).astype(BF16)
        logit = jnp.concatenate([_dot(lhs, pages(kt_refs, b)), _dot_nt(lhs, pad_rows(kn_ref[b]))], axis=1)
        logit = jnp.where(jnp.concatenate([sel] * ATT_HEADS, axis=0), logit, neg_inf)
        p = jnp.exp2(logit - jnp.max(logit, axis=1, keepdims=True))
        pb = p.astype(BF16)
        res = _dot_nt(pb[:, :past], pages(vt_refs, b)) + _dot(pb[:, past:], pad_rows(vn_ref[b]))
        res = res / jnp.sum(p, axis=1, keepdims=True)
        out = jnp.zeros((t, W_GROUP), F32)
        for h in range(ATT_HEADS):
            out = jnp.where((lane_o // HEAD_DIM) == h, res[h * t:(h + 1) * t], out)
        o_ref[b] = out.astype(o_ref.dtype)


def dsa_sample(q, qi, wi, k_new, v_new, ki2_new, kt_pages, vt_pages, kit_pages, page_table, layer, *, g=4):
    b, t_len, _ = q.shape
    n_pages = page_table.shape[1]
    page = kt_pages.shape[-1]
    g = math.gcd(g, b)
    topk = min(TOPK_MAX, (n_pages * page + t_len) // 4)

    def spec(width):
        return pl.BlockSpec((g, t_len, width), lambda i, pt: (i, 0, 0))

    def page_specs(rows):
        return [pl.BlockSpec((1, 1, rows, page),
                             lambda i, pt, s=s, p=p: (pt[(i * g + s) * n_pages + p], layer, 0, 0))
                for s in range(g) for p in range(n_pages)]

    grid_spec = pltpu.PrefetchScalarGridSpec(
        num_scalar_prefetch=1,
        grid=(b // g,),
        in_specs=[spec(W_GROUP), spec(2 * W_GROUP), spec(LANES), spec(W_GROUP), spec(W_GROUP), spec(LANES)]
        + page_specs(W_GROUP) + page_specs(W_GROUP) + page_specs(IDX_DIM),
        out_specs=spec(W_GROUP),
        scratch_shapes=[pltpu.VMEM((n_pages + 1, g * t_len, LANES), I32)],
    )
    n_pg = g * n_pages
    return pl.pallas_call(
        functools.partial(_dsa_sample_kernel, g=g, t=t_len, n_pages=n_pages, page=page, topk=topk),
        grid_spec=grid_spec,
        out_shape=jax.ShapeDtypeStruct((b, t_len, W_GROUP), BF16),
        compiler_params=_cparams(("parallel",)),
        name="dsa_sample",
    )(page_table.reshape(-1), q, qi, wi, k_new, v_new, ki2_new,
      *([kt_pages] * n_pg), *([vt_pages] * n_pg), *([kit_pages] * n_pg))


def _mem_attn_kernel(q_ref, mk_ref, mv_ref, o_ref, *, g):
    for b in range(g):
        q = q_ref[b]
        outs = []
        for h in range(MEM_HEADS):
            sl = slice(h * MEM_HEAD_DIM, (h + 1) * MEM_HEAD_DIM)
            logit = _dot_nt(q[:, sl], mk_ref[b, 0, :, sl].astype(BF16)) * (MEM_HEAD_DIM ** -0.5)
            m = jnp.max(logit, axis=1, keepdims=True)
            p = jnp.exp(logit - m)
            p = p / jnp.sum(p, axis=1, keepdims=True)
            outs.append(_dot(p.astype(BF16), mv_ref[b, 0, :, sl].astype(BF16)))
        o_ref[b] = jnp.concatenate(outs, axis=-1).astype(o_ref.dtype)


def mem_attention(q, mk, mv, layer, *, tq=512, g=4):
    b, t_len, d = q.shape
    m_len = mk.shape[2]
    if t_len >= tq:
        g = 1
    else:
        tq, g = t_len, math.gcd(g, b)
    mem_spec = pl.BlockSpec((g, 1, m_len, d), lambda i, j: (i, layer, 0, 0))
    return pl.pallas_call(
        functools.partial(_mem_attn_kernel, g=g),
        grid=(b // g, t_len // tq),
        in_specs=[pl.BlockSpec((g, tq, d), lambda i, j: (i, j, 0)), mem_spec, mem_spec],
        out_specs=pl.BlockSpec((g, tq, d), lambda i, j: (i, j, 0)),
        out_shape=jax.ShapeDtypeStruct((b, t_len, d), BF16),
        compiler_params=_cparams(("parallel", "arbitrary")),
        name="mem_attention",
    )(q, mk, mv)


def _prep_layer(l, p):
    w = p['w_in'][l]
    cuts = [0]
    for s in (W_GROUP,) * 9 + (IDX_HEADS * IDX_DIM, IDX_DIM, IDX_HEADS):
        cuts.append(cuts[-1] + s)
    piece = lambda i: w[:, cuts[i]:cuts[i + 1]]
    zeros = lambda n: jnp.zeros((D_MODEL, n), w.dtype)
    w_in = jnp.concatenate(
        [piece(i) for i in range(6)] + [piece(6), piece(7), piece(9), piece(8), piece(10), zeros(LANES - IDX_DIM),
                                        piece(11), zeros(LANES - IDX_HEADS)], axis=1).astype(BF16)
    pool_w = p['pool_w'][l]
    pool_bd = jnp.zeros((W_GROUP, W_GROUP), F32)
    for g in range(len(POOL_WINDOWS)):
        pool_bd = pool_bd.at[g * POOL_CH:(g + 1) * POOL_CH, g * POOL_CH:(g + 1) * POOL_CH].set(pool_w[g])
    return {
        'norm_mix': p['norm_mix'][l], 'w_in': w_in,
        'conv_w': p['conv_w'][l], 'pool_w_bd': pool_bd.astype(BF16),
        'pool_scale': p['pool_scale'][l].reshape(1, W_GROUP), 'chunk_norm': p['chunk_norm'][l].reshape(1, W_GROUP),
        'chunk_ws': p['chunk_ws'][l], 'chunk_b': p['chunk_b'][l],
        'w_out_abc': p['w_out'][l][:3 * W_GROUP].astype(BF16), 'w_out_d': p['w_out'][l][3 * W_GROUP:].astype(BF16),
        'norm_mem': p['norm_mem'][l], 'wq_mem': p['wq_mem'][l].astype(BF16), 'wo_mem': p['wo_mem'][l].astype(BF16),
        'norm_ffn': p['norm_ffn'][l],
    }


def _chunk_consts(lw, chunk):
    mask = jnp.tril(jnp.ones((chunk, chunk), dtype=bool))
    wm = jnp.where(mask[None], lw['chunk_ws'][:, :chunk, :chunk], 0).astype(BF16)
    bias = jnp.repeat(lw['chunk_b'][:, :chunk].T, W_GROUP // CHUNK_HEADS, axis=1)
    return wm, bias


def _rope_tables(pos):
    half = HEAD_DIM // 2
    inv = ROPE_THETA ** (-jnp.arange(half, dtype=F32) / half)
    ang = pos.astype(F32)[:, None] * inv[None, :]
    cos, sin = jnp.cos(ang), jnp.sin(ang)
    cos_t = jnp.concatenate([cos, cos] * (LANES // HEAD_DIM), axis=1)
    sin_t = jnp.concatenate([-sin, sin] * (LANES // HEAD_DIM), axis=1)
    return cos_t, sin_t


def _run_trunk(x, pos0, conv_prev, pool_prev, past, mem_k, mem_v, layers, ffn_w, norm_final):
    b, t_len, d = x.shape
    n = b * t_len
    depth = len(layers)
    from_state = past is not None
    pos = pos0 + jnp.arange(t_len, dtype=I32)
    cos_t, sin_t = _rope_tables(pos)
    h = x.reshape(n, d)
    ks, vs, kis, convs, pools, cvs = [], [], [], [], [], []
    for l, lw in enumerate(layers):
        proj = norm_matmul(h, lw['norm_mix'], lw['w_in']).reshape(b, t_len, P_PAD)
        chunk = min(t_len, CHUNK)
        wm, cbias = _chunk_consts(lw, chunk)
        lw_m = dict(lw, chunk_wm=wm, chunk_bias=cbias)
        outs = mixers(proj, conv_prev[:, l] if from_state else None, pool_prev[:, l] if from_state else None,
                      cos_t, sin_t, lw_m, pos0=pos0, from_state=from_state, emit_cv=from_state)
        mix, q, k, kb, v, vb, qi, ki, ki2, wi, conv_s, pool_s = outs[:12]
        if from_state:
            cvs.append(outs[12])
            yd = dsa_sample(q, qi, wi, kb, vb, ki2, *past, l)
        else:
            yd = dsa_prompt(q, qi, wi, kb, vb, ki2)
        h = matmul_res([mix.reshape(n, 3 * W_GROUP), yd.reshape(n, W_GROUP)], [lw['w_out_abc'], lw['w_out_d']], h)
        qm = norm_matmul(h, lw['norm_mem'], lw['wq_mem'], out_dtype=BF16).reshape(b, t_len, d)
        om = mem_attention(qm, mem_k, mem_v, l)
        h = matmul_res([om.reshape(n, d)], [lw['wo_mem']], h)
        last = l == depth - 1
        fw = ffn_w[l]
        if fw['kind'] == 'dense':
            h = ffn_dense(h, lw['norm_ffn'], fw['w1'], fw['w3'], fw['w2'], norm_final, final_norm=last)
        else:
            h = moe_ffn(h, lw['norm_ffn'], fw['router'], fw['w1'], fw['w3'], fw['w2'], norm_final, final_norm=last)
        ks.append(k.reshape(b, t_len, ATT_HEADS, HEAD_DIM))
        vs.append(v.reshape(b, t_len, ATT_HEADS, HEAD_DIM))
        kis.append(ki[:, :, :IDX_DIM])
        convs.append(conv_s)
        pools.append(pool_s)
    st = lambda xs: jnp.stack(xs, axis=1)
    y = h.reshape(b, t_len, d)
    return y, st(ks), st(vs), st(kis), st(convs), st(pools), (st(cvs) if cvs else None)


def kernel(x_prompt, x_sample, mem_prompt, cache_k, cache_v, cache_kidx, cache_mem_k, cache_mem_v, state_conv, state_pool, page_table, norm_mix, w_in, conv_w, pool_w, pool_scale, chunk_norm, chunk_ws, chunk_b, w_out, norm_mem, wq_mem, wk_mem, wv_mem, wo_mem, norm_ffn, w1_dense, w3_dense, w2_dense, router, w1_moe, w3_moe, w2_moe, norm_final):
    p = {
        'norm_mix': norm_mix, 'w_in': w_in, 'conv_w': conv_w, 'pool_w': pool_w, 'pool_scale': pool_scale,
        'chunk_norm': chunk_norm, 'chunk_ws': chunk_ws, 'chunk_b': chunk_b, 'w_out': w_out, 'norm_mem': norm_mem,
        'wq_mem': wq_mem, 'wo_mem': wo_mem, 'norm_ffn': norm_ffn,
    }
    depth = norm_mix.shape[0]
    layers = [_prep_layer(l, p) for l in range(depth)]
    ffn_w = []
    for l in range(depth):
        j = l // 2
        if l % 2 == 0:
            ffn_w.append({'kind': 'dense', 'w1': w1_dense[j].astype(BF16), 'w3': w3_dense[j].astype(BF16),
                          'w2': w2_dense[j].astype(BF16)})
        else:
            router_pad = jnp.concatenate([router[j], jnp.zeros((D_MODEL, LANES - N_EXPERTS), F32)], axis=1)
            ffn_w.append({'kind': 'moe', 'router': router_pad, 'w1': w1_moe[j].astype(BF16),
                          'w3': w3_moe[j].astype(BF16), 'w2': w2_moe[j].astype(BF16)})

    bp, tp, d = x_prompt.shape
    n_mem = mem_prompt.shape[1]
    w_kv = jnp.concatenate([wk_mem[l] for l in range(depth)] + [wv_mem[l] for l in range(depth)], axis=1).astype(BF16)
    mem_kv = norm_matmul(mem_prompt.reshape(bp * n_mem, d), jnp.ones((d,), F32), w_kv, norm=False)
    mem_kv = mem_kv.reshape(bp, n_mem, 2, depth, d)
    mem_k_p = jnp.moveaxis(mem_kv[:, :, 0], 2, 1)
    mem_v_p = jnp.moveaxis(mem_kv[:, :, 1], 2, 1)
    y_p, k_p, v_p, ki_p, conv_p, pool_p, _ = _run_trunk(
        x_prompt, 0, None, None, None, mem_k_p, mem_v_p, layers, ffn_w, norm_final)

    bs, ts, _ = x_sample.shape
    n_pages, page = page_table.shape[1], cache_k.shape[2]
    past_len = n_pages * page

    n_phys = cache_k.shape[0]
    kt_pages = jnp.transpose(cache_k, (0, 1, 3, 4, 2)).reshape(n_phys, depth, W_GROUP, page)
    vt_pages = jnp.transpose(cache_v, (0, 1, 3, 4, 2)).reshape(n_phys, depth, W_GROUP, page)
    kit_pages = jnp.transpose(cache_kidx, (0, 1, 3, 2))
    past = (kt_pages, vt_pages, kit_pages, page_table)
    y_s, k_s, v_s, ki_s, conv_s, pool_s, cv_s = _run_trunk(
        x_sample, past_len, state_conv, state_pool, past, cache_mem_k.reshape(bs, depth, n_mem, d),
        cache_mem_v.reshape(bs, depth, n_mem, d), layers, ffn_w, norm_final)

    mem_shape = (bp, depth, n_mem, MEM_HEADS, MEM_HEAD_DIM)
    return (y_p, y_s, k_p, v_p, ki_p, mem_k_p.reshape(mem_shape), mem_v_p.reshape(mem_shape), conv_p, pool_p,
            k_s, v_s, ki_s, conv_s, pool_s, cv_s)
```

```python
import functools
import math

import jax
import jax.numpy as jnp
from jax import lax
from jax.experimental import pallas as pl
from jax.experimental.pallas import tpu as pltpu

F32 = jnp.float32
BF16 = jnp.bfloat16
I32 = jnp.int32
I16 = jnp.int16

D_MODEL = 1024
W_GROUP = 256
CONV_W = 3
POOL_WINDOWS = (2, 4, 8, 16)
POOL_CH = 64
POOL_BUF = 15
CHUNK = 128
CHUNK_HEADS = 4
ATT_HEADS = 4
HEAD_DIM = 64
IDX_HEADS = 8
IDX_DIM = 64
IDX_W_SCALE = (IDX_HEADS ** -0.5) * (IDX_DIM ** -0.5)
TOPK_MAX = 256
ROPE_THETA = 10000.0
MEM_HEADS = 4
MEM_HEAD_DIM = 256
D_FF = 3584
N_EXPERTS = 8
EPS = 1e-6

LANES = 128
SUBLANES = 8
PACK16 = 16
PREV_ROWS = 16
P_PAD = 3072
HALF = P_PAD // 2
OFF_Q, OFF_K, OFF_QI, OFF_V, OFF_KI, OFF_WI = 1536, 1792, 2048, 2560, 2816, 2944
VMEM_LIMIT = 56 * 1024 * 1024

INT_MIN = -2 ** 31
I16_MIN = -2 ** 15
Q_SCALE = (HEAD_DIM ** -0.5) * math.log2(math.e)


def _cparams(sem):
    return pltpu.CompilerParams(dimension_semantics=sem, vmem_limit_bytes=VMEM_LIMIT)


def _rms(x, g):
    ms = jnp.mean(x * x, axis=-1, keepdims=True)
    return x * lax.rsqrt(ms + EPS) * g


def _dot(a, b):
    return jnp.dot(a, b, preferred_element_type=F32)


def _dot_nt(a, b):
    return lax.dot_general(a, b, (((1,), (1,)), ((), ())), preferred_element_type=F32)


def _norm_matmul_kernel(x_ref, g_ref, w_ref, o_ref, xn_ref, *, norm):
    @pl.when(pl.program_id(1) == 0)
    def _():
        x = x_ref[...]
        if norm:
            x = _rms(x, g_ref[...])
        xn_ref[...] = x.astype(BF16)

    o_ref[...] = _dot(xn_ref[...], w_ref[...]).astype(o_ref.dtype)


def norm_matmul(x, g, w, *, norm=True, out_dtype=F32, tm=512, tn=1024):
    m, k = x.shape
    n = w.shape[1]
    tm = min(tm, m)
    tn = min(tn, n)
    return pl.pallas_call(
        functools.partial(_norm_matmul_kernel, norm=norm),
        grid=(m // tm, n // tn),
        in_specs=[pl.BlockSpec((tm, k), lambda i, j: (i, 0)),
                  pl.BlockSpec((1, k), lambda i, j: (0, 0)),
                  pl.BlockSpec((k, tn), lambda i, j: (0, j))],
        out_specs=pl.BlockSpec((tm, tn), lambda i, j: (i, j)),
        out_shape=jax.ShapeDtypeStruct((m, n), out_dtype),
        scratch_shapes=[pltpu.VMEM((tm, k), BF16)],
        compiler_params=_cparams(("parallel", "arbitrary")),
        name="norm_matmul",
    )(x, g.reshape(1, k), w)


def _matmul_res_kernel(*refs, n_in):
    a_refs, w_refs = refs[:n_in], refs[n_in:2 * n_in]
    r_ref, o_ref = refs[2 * n_in], refs[2 * n_in + 1]
    acc = r_ref[...]
    for a, w in zip(a_refs, w_refs):
        acc = acc + _dot(a[...].astype(BF16), w[...])
    o_ref[...] = acc


def matmul_res(a_list, w_list, res, *, tm=512):
    m, n = res.shape
    tm = min(tm, m)
    n_in = len(a_list)
    in_specs = [pl.BlockSpec((tm, a.shape[1]), lambda i: (i, 0)) for a in a_list]
    in_specs += [pl.BlockSpec(w.shape, lambda i: (0, 0)) for w in w_list]
    in_specs += [pl.BlockSpec((tm, n), lambda i: (i, 0))]
    return pl.pallas_call(
        functools.partial(_matmul_res_kernel, n_in=n_in),
        grid=(m // tm,),
        in_specs=in_specs,
        out_specs=pl.BlockSpec((tm, n), lambda i: (i, 0)),
        out_shape=jax.ShapeDtypeStruct((m, n), F32),
        compiler_params=_cparams(("parallel",)),
        name="matmul_res",
    )(*a_list, *w_list, res)


def _ffn_kernel(x_ref, g_ref, w1_ref, w3_ref, w2_ref, gf_ref, o_ref, xn_ref, acc_ref, *, final_norm):
    f = pl.program_id(1)

    @pl.when(f == 0)
    def _():
        xn_ref[...] = _rms(x_ref[...], g_ref[...]).astype(BF16)
        acc_ref[...] = jnp.zeros_like(acc_ref)

    xn = xn_ref[...]
    a = _dot(xn, w1_ref[...])
    b = _dot(xn, w3_ref[...])
    hidden = (a * jax.nn.sigmoid(a) * b).astype(BF16)
    acc_ref[...] += _dot(hidden, w2_ref[...])

    @pl.when(f == pl.num_programs(1) - 1)
    def _():
        y = x_ref[...] + acc_ref[...]
        if final_norm:
            y = _rms(y, gf_ref[...])
        o_ref[...] = y


def ffn_dense(x, g, w1, w3, w2, g_final, *, final_norm, tm=512, tf=512):
    m, d = x.shape
    ff = w1.shape[1]
    tm = min(tm, m)
    return pl.pallas_call(
        functools.partial(_ffn_kernel, final_norm=final_norm),
        grid=(m // tm, ff // tf),
        in_specs=[pl.BlockSpec((tm, d), lambda i, f: (i, 0)),
                  pl.BlockSpec((1, d), lambda i, f: (0, 0)),
                  pl.BlockSpec((d, tf), lambda i, f: (0, f)),
                  pl.BlockSpec((d, tf), lambda i, f: (0, f)),
                  pl.BlockSpec((tf, d), lambda i, f: (f, 0)),
                  pl.BlockSpec((1, d), lambda i, f: (0, 0))],
        out_specs=pl.BlockSpec((tm, d), lambda i, f: (i, 0)),
        out_shape=jax.ShapeDtypeStruct((m, d), F32),
        scratch_shapes=[pltpu.VMEM((tm, d), BF16), pltpu.VMEM((tm, d), F32)],
        compiler_params=_cparams(("parallel", "arbitrary")),
        name="ffn_dense",
    )(x, g.reshape(1, d), w1, w3, w2, g_final.reshape(1, d))


def _router_kernel(x_ref, g_ref, r_ref, hn_ref, e_ref, gate_ref):
    hn = _rms(x_ref[...], g_ref[...])
    hn_ref[...] = hn.astype(BF16)
    logits = jnp.dot(hn, r_ref[...], preferred_element_type=F32, precision=lax.Precision.HIGHEST)
    lane = lax.broadcasted_iota(I32, logits.shape, 1)
    neg = jnp.float32(-jnp.inf)
    l1 = jnp.where(lane < N_EXPERTS, logits, neg)
    m1 = jnp.max(l1, axis=1, keepdims=True)
    i1 = jnp.min(jnp.where(l1 == m1, lane, LANES), axis=1, keepdims=True)
    l2 = jnp.where(lane == i1, neg, l1)
    m2 = jnp.max(l2, axis=1, keepdims=True)
    i2 = jnp.min(jnp.where(l2 == m2, lane, LANES), axis=1, keepdims=True)
    t = jnp.exp(m2 - m1)
    denom = 1.0 + t
    e_ref[...] = jnp.where(lane == 0, i1, jnp.where(lane == 1, i2, 0))
    gate_ref[...] = jnp.where(lane == 0, 1.0 / denom, jnp.where(lane == 1, t / denom, 0.0))


def moe_router(x, g, router_pad, *, tm=512):
    m, d = x.shape
    tm = min(tm, m)
    return pl.pallas_call(
        _router_kernel,
        grid=(m // tm,),
        in_specs=[pl.BlockSpec((tm, d), lambda i: (i, 0)),
                  pl.BlockSpec((1, d), lambda i: (0, 0)),
                  pl.BlockSpec((d, LANES), lambda i: (0, 0))],
        out_specs=[pl.BlockSpec((tm, d), lambda i: (i, 0)),
                   pl.BlockSpec((tm, LANES), lambda i: (i, 0)),
                   pl.BlockSpec((tm, LANES), lambda i: (i, 0))],
        out_shape=[jax.ShapeDtypeStruct((m, d), BF16),
                   jax.ShapeDtypeStruct((m, LANES), I32),
                   jax.ShapeDtypeStruct((m, LANES), F32)],
        compiler_params=_cparams(("parallel",)),
        name="moe_router",
    )(x, g.reshape(1, d), router_pad)


def _expert_kernel(be_ref, nb_ref, xs_ref, gate_ref, w1_ref, w3_ref, w2_ref, o_ref, acc_ref):
    j, f = pl.program_id(0), pl.program_id(1)
    live = j < nb_ref[0]

    @pl.when(f == 0)
    def _():
        acc_ref[...] = jnp.zeros_like(acc_ref)

    @pl.when(live)
    def _():
        xs = xs_ref[...]
        a = _dot(xs, w1_ref[0])
        b = _dot(xs, w3_ref[0])
        hidden = (a * jax.nn.sigmoid(a) * b).astype(BF16)
        acc_ref[...] += _dot(hidden, w2_ref[0])

    @pl.when(f == pl.num_programs(1) - 1)
    def _():
        o_ref[...] = acc_ref[...] * gate_ref[...]


def moe_experts(block_e, n_live, xs, slot_gate, w1, w3, w2, *, bm, tf=512):
    n_slots, d = xs.shape
    ff = w1.shape[2]
    n_blocks = n_slots // bm
    grid_spec = pltpu.PrefetchScalarGridSpec(
        num_scalar_prefetch=2,
        grid=(n_blocks, ff // tf),
        in_specs=[pl.BlockSpec((bm, d), lambda j, f, be, nb: (j, 0)),
                  pl.BlockSpec((bm, 1), lambda j, f, be, nb: (j, 0)),
                  pl.BlockSpec((1, d, tf), lambda j, f, be, nb: (be[j], 0, f)),
                  pl.BlockSpec((1, d, tf), lambda j, f, be, nb: (be[j], 0, f)),
                  pl.BlockSpec((1, tf, d), lambda j, f, be, nb: (be[j], f, 0))],
        out_specs=pl.BlockSpec((bm, d), lambda j, f, be, nb: (j, 0)),
        scratch_shapes=[pltpu.VMEM((bm, d), F32)],
    )
    return pl.pallas_call(
        _expert_kernel,
        grid_spec=grid_spec,
        out_shape=jax.ShapeDtypeStruct((n_slots, d), F32),
        compiler_params=_cparams(("parallel", "arbitrary")),
        name="moe_experts",
    )(block_e, n_live, xs, slot_gate, w1, w3, w2)


def _combine_kernel(x_ref, a_ref, b_ref, gf_ref, o_ref, *, final_norm):
    y = x_ref[...] + (a_ref[...] + b_ref[...])
    if final_norm:
        y = _rms(y, gf_ref[...])
    o_ref[...] = y


def moe_combine(x, ya, yb, g_final, *, final_norm, tm=512):
    m, d = x.shape
    tm = min(tm, m)
    spec = pl.BlockSpec((tm, d), lambda i: (i, 0))
    return pl.pallas_call(
        functools.partial(_combine_kernel, final_norm=final_norm),
        grid=(m // tm,),
        in_specs=[spec, spec, spec, pl.BlockSpec((1, d), lambda i: (0, 0))],
        out_specs=spec,
        out_shape=jax.ShapeDtypeStruct((m, d), F32),
        compiler_params=_cparams(("parallel",)),
        name="moe_combine",
    )(x, ya, yb, g_final.reshape(1, d))


def _compact_kernel(ib_ref, ic_ref, first_ref, total_ref, tok_ref, hn_ref, o_ref, acc_ref, *, tc):
    i = pl.program_id(0)

    @pl.when(first_ref[i] == 1)
    def _():
        acc_ref[...] = jnp.zeros_like(acc_ref)

    @pl.when(i < total_ref[0])
    def _():
        local = tok_ref[...] - ic_ref[i] * tc
        lane = lax.broadcasted_iota(I32, (tok_ref.shape[0], tc), 1)
        onehot = jnp.where(local == lane, 1.0, 0.0).astype(BF16)
        acc_ref[...] += _dot(onehot, hn_ref[...])

    o_ref[...] = acc_ref[...].astype(o_ref.dtype)


def moe_compact(item_block, item_chunk, item_first, n_items, slot_tok, hn, *, bm, tc):
    n_slots = slot_tok.shape[0]
    n, d = hn.shape
    n_max = item_block.shape[0]
    grid_spec = pltpu.PrefetchScalarGridSpec(
        num_scalar_prefetch=4,
        grid=(n_max,),
        in_specs=[pl.BlockSpec((bm, 1), lambda i, ib, ic, fr, tot: (ib[i], 0)),
                  pl.BlockSpec((tc, d), lambda i, ib, ic, fr, tot: (ic[i], 0))],
        out_specs=pl.BlockSpec((bm, d), lambda i, ib, ic, fr, tot: (ib[i], 0)),
        scratch_shapes=[pltpu.VMEM((bm, d), F32)],
    )
    return pl.pallas_call(
        functools.partial(_compact_kernel, tc=tc),
        grid_spec=grid_spec,
        out_shape=jax.ShapeDtypeStruct((n_slots, d), BF16),
        compiler_params=_cparams(("arbitrary",)),
        name="moe_compact",
    )(item_block, item_chunk, item_first, n_items, slot_tok.reshape(n_slots, 1), hn)


def moe_ffn(x, g, router_pad, w1, w3, w2, g_final, *, final_norm):
    n, d = x.shape
    bm = 512 if n >= 4096 else 128
    tc = min(n, 512)
    hn, e12, g12 = moe_router(x, g, router_pad)
    e_flat = e12[:, :2].reshape(-1)
    g_flat = g12[:, :2].reshape(-1)
    n_assign = 2 * n
    order = jnp.argsort(e_flat, stable=True).astype(I32)
    rank = jnp.argsort(order).astype(I32)
    counts = jnp.sum(e_flat[:, None] == jnp.arange(N_EXPERTS, dtype=I32)[None, :], axis=0).astype(I32)
    padded = (counts + bm - 1) // bm * bm
    start = jnp.cumsum(counts) - counts
    pend = jnp.cumsum(padded)
    pstart = pend - padded
    n_blocks = -(-n_assign // bm) + N_EXPERTS
    n_slots = n_blocks * bm
    block_e = jnp.minimum(
        jnp.searchsorted(pend, jnp.arange(n_blocks, dtype=I32) * bm, side='right'), N_EXPERTS - 1).astype(I32)
    n_live = (pend[-1] // bm).astype(I32).reshape(1)
    slot_e = jnp.repeat(block_e, bm)
    slot_idx = jnp.arange(n_slots, dtype=I32) - pstart[slot_e]
    slot_live = slot_idx < counts[slot_e]
    slot_src = order[jnp.clip(start[slot_e] + slot_idx, 0, n_assign - 1)]
    slot_tok = jnp.where(slot_live, slot_src // 2, -1)
    slot_gate = jnp.where(slot_live, g_flat[slot_src], 0.0)
    assign_slot = pstart[e_flat] + rank - start[e_flat]
    tok_b = slot_tok.reshape(n_blocks, bm)
    c_lo = jnp.maximum(tok_b[:, 0], 0) // tc
    c_hi = jnp.maximum(jnp.max(tok_b, axis=1), 0) // tc
    n_items_b = c_hi - c_lo + 1
    item_end = jnp.cumsum(n_items_b)
    total = item_end[-1]
    n_items_max = n_blocks + N_EXPERTS * (n // tc)
    item_id = jnp.arange(n_items_max, dtype=I32)
    item = jnp.minimum(item_id, total - 1)
    item_block = jnp.searchsorted(item_end, item, side='right').astype(I32)
    item_off = item - (item_end - n_items_b)[item_block]
    item_chunk = (c_lo[item_block] + item_off).astype(I32)
    item_first = ((item_off == 0) & (item_id < total)).astype(I32)
    xs = moe_compact(item_block, item_chunk, item_first, total.astype(I32).reshape(1), slot_tok, hn, bm=bm, tc=tc)
    ys = moe_experts(block_e, n_live, xs, slot_gate.reshape(n_slots, 1), w1, w3, w2, bm=bm)
    pair = assign_slot.reshape(n, 2)
    return moe_combine(x, ys[pair[:, 0]], ys[pair[:, 1]], g_final, final_norm=final_norm)


def _swap_half_heads(x):
    n = x.shape[-1]
    lane = lax.broadcasted_iota(I32, x.shape, x.ndim - 1)
    fwd = pltpu.roll(x, n - HEAD_DIM // 2, x.ndim - 1)
    bwd = pltpu.roll(x, HEAD_DIM // 2, x.ndim - 1)
    return jnp.where(lane % HEAD_DIM < HEAD_DIM // 2, fwd, bwd)


def _rope(x, cos, sin):
    reps = x.shape[-1] // LANES
    if reps > 1:
        cos = jnp.concatenate([cos] * reps, axis=-1)
        sin = jnp.concatenate([sin] * reps, axis=-1)
    return x * cos + _swap_half_heads(x) * sin


def _mixer_kernel(*refs, tt, chunk, pos0, from_state, emit_cv):
    it = iter(refs)
    xa_ref, xb_ref = next(it), next(it)
    if from_state:
        convp_ref, poolp_ref = next(it), next(it)
    else:
        prev_ref = next(it)
    cos_ref, sin_ref = next(it), next(it)
    convw_ref, poolw_ref, pscale_ref, cnorm_ref, wm_ref, cbias_ref = (next(it) for _ in range(6))
    mix_ref, q_ref, k_ref, kb_ref, v_ref, vb_ref, qi_ref, ki_ref, ki2_ref, wi_ref, convs_ref, pools_ref = (
        next(it) for _ in range(12))
    cv_ref = next(it) if emit_cv else None
    ybuf, xbuf = next(it), next(it)

    t = pl.program_id(1)
    xa = xa_ref[0]
    a_in, a_b, a_c = xa[:, 0:256], xa[:, 256:512], xa[:, 512:768]
    b_in, c_u, c_v = xa[:, 768:1024], xa[:, 1024:1280], xa[:, 1280:1536]

    y = a_c * a_in
    if from_state:
        ybuf[PREV_ROWS - 2:PREV_ROWS, :] = convp_ref[0]
        xbuf[0:1, :] = jnp.zeros((1, W_GROUP), F32)
        xbuf[1:PREV_ROWS, :] = poolp_ref[0]
    else:
        first = (t == 0).astype(F32)
        prev = prev_ref[0] * (1.0 - first)
        ybuf[0:PREV_ROWS, :] = prev[:, 512:768] * prev[:, 0:256]
        xbuf[0:PREV_ROWS, :] = prev[:, 768:1024]
    ybuf[PREV_ROWS:PREV_ROWS + tt, :] = y
    xbuf[PREV_ROWS:PREV_ROWS + tt, :] = b_in

    cw = convw_ref[...]
    z = cw[0:1, :] * ybuf[PREV_ROWS - 2:PREV_ROWS - 2 + tt, :]
    z = z + cw[1:2, :] * ybuf[PREV_ROWS - 1:PREV_ROWS - 1 + tt, :]
    z = z + cw[2:3, :] * y
    ya = a_b * z
    convs_ref[0] = ybuf[PREV_ROWS + tt - 2:PREV_ROWS + tt, :]

    lane = lax.broadcasted_iota(I32, (tt, W_GROUP), 1)
    row = lax.broadcasted_iota(I32, (tt, W_GROUP), 0)
    pos = (pos0 + t * tt + row).astype(F32)
    run = b_in
    mean = jnp.zeros((tt, W_GROUP), F32)
    for j in range(1, POOL_WINDOWS[-1] + 1):
        if j > 1:
            run = run + xbuf[PREV_ROWS - (j - 1):PREV_ROWS - (j - 1) + tt, :]
        if j in POOL_WINDOWS:
            grp = POOL_WINDOWS.index(j)
            cnt = jnp.minimum(pos + 1.0, float(j))
            mean = jnp.where(lane // POOL_CH == grp, run / cnt, mean)
    dlt = (mean - b_in).astype(BF16)
    yb = _dot(dlt, poolw_ref[...]) * pscale_ref[...]
    pools_ref[0] = xbuf[PREV_ROWS + tt - POOL_BUF:PREV_ROWS + tt, :]

    vn = _rms(c_v, cnorm_ref[...])
    if emit_cv:
        cv_ref[0] = vn
    vnb = vn.astype(BF16)
    lane_c = lax.broadcasted_iota(I32, (chunk, W_GROUP), 1)
    parts = []
    for c in range(tt // chunk):
        vc = vnb[c * chunk:(c + 1) * chunk, :]
        s = cbias_ref[...]
        for hd in range(CHUNK_HEADS):
            sh = _dot(wm_ref[hd], vc)
            s = s + jnp.where(lane_c // (W_GROUP // CHUNK_HEADS) == hd, sh, 0.0)
        parts.append(s)
    s_all = parts[0] if len(parts) == 1 else jnp.concatenate(parts, axis=0)
    yc = c_u * s_all

    mix_ref[0] = jnp.concatenate([ya, yb, yc], axis=-1).astype(BF16)

    xb = xb_ref[0]
    cos, sin = cos_ref[...], sin_ref[...]
    q = _rope(xb[:, 0:256], cos, sin)
    k = _rope(xb[:, 256:512], cos, sin)
    qi = _rope(xb[:, 512:1024], cos, sin)
    v = xb[:, 1024:1280]
    ki = _rope(xb[:, 1280:1408], cos, sin)
    wi = xb[:, 1408:1536] * IDX_W_SCALE
    q_ref[0] = (q * Q_SCALE).astype(BF16)
    k_ref[0] = k
    kb_ref[0] = k.astype(BF16)
    v_ref[0] = v
    if from_state:
        vb_ref[0] = v.astype(BF16)
    else:
        vb_ref[0, 0] = v.T.astype(BF16)
    qi_ref[0] = qi.astype(BF16)
    ki_ref[0] = ki
    lane128 = lax.broadcasted_iota(I32, ki.shape, 1)
    ki2_ref[0] = jnp.where(lane128 < IDX_DIM, ki, pltpu.roll(ki, IDX_DIM, 1)).astype(BF16)
    wi_ref[0] = wi


def mixers(proj, conv_prev, pool_prev, cos, sin, lw, *, pos0, from_state, emit_cv):
    b, t_len, _ = proj.shape
    tt = min(t_len, 256)
    chunk = min(t_len, CHUNK)
    nt = t_len // tt
    in_specs = [pl.BlockSpec((1, tt, HALF), lambda i, t: (i, t, 0)),
                pl.BlockSpec((1, tt, HALF), lambda i, t: (i, t, 1))]
    args = [proj, proj]
    if from_state:
        in_specs += [pl.BlockSpec((1, CONV_W - 1, W_GROUP), lambda i, t: (i, 0, 0)),
                     pl.BlockSpec((1, POOL_BUF, W_GROUP), lambda i, t: (i, 0, 0))]
        args += [conv_prev, pool_prev]
    else:
        per = tt // PREV_ROWS
        in_specs += [pl.BlockSpec((1, PREV_ROWS, HALF), lambda i, t: (i, jnp.maximum(t * per - 1, 0), 0))]
        args += [proj]
    in_specs += [pl.BlockSpec((tt, LANES), lambda i, t: (t, 0)),
                 pl.BlockSpec((tt, LANES), lambda i, t: (t, 0))]
    args += [cos, sin]
    consts = [lw['conv_w'], lw['pool_w_bd'], lw['pool_scale'], lw['chunk_norm'], lw['chunk_wm'], lw['chunk_bias']]
    for c in consts:
        in_specs.append(pl.BlockSpec(c.shape, lambda i, t, nd=c.ndim: (0,) * nd))
    args += consts

    def tok(width, dtype):
        return (pl.BlockSpec((1, tt, width), lambda i, t: (i, t, 0)), jax.ShapeDtypeStruct((b, t_len, width), dtype))

    outs = [tok(768, BF16),
            tok(256, BF16),
            tok(256, F32), tok(256, BF16),
            tok(256, F32),
            tok(256, BF16) if from_state else
            (pl.BlockSpec((1, 1, W_GROUP, tt), lambda i, t: (i, t, 0, 0)),
             jax.ShapeDtypeStruct((b, nt, W_GROUP, tt), BF16)),
            tok(512, BF16),
            tok(LANES, F32),
            tok(LANES, BF16),
            tok(LANES, F32),
            (pl.BlockSpec((1, CONV_W - 1, W_GROUP), lambda i, t: (i, 0, 0)),
             jax.ShapeDtypeStruct((b, CONV_W - 1, W_GROUP), F32)),
            (pl.BlockSpec((1, POOL_BUF, W_GROUP), lambda i, t: (i, 0, 0)),
             jax.ShapeDtypeStruct((b, POOL_BUF, W_GROUP), F32))]
    if emit_cv:
        outs.append(tok(256, F32))
    return pl.pallas_call(
        functools.partial(_mixer_kernel, tt=tt, chunk=chunk, pos0=pos0, from_state=from_state, emit_cv=emit_cv),
        grid=(b, nt),
        in_specs=in_specs,
        out_specs=[o[0] for o in outs],
        out_shape=[o[1] for o in outs],
        scratch_shapes=[pltpu.VMEM((PREV_ROWS + tt, W_GROUP), F32), pltpu.VMEM((PREV_ROWS + tt, W_GROUP), F32)],
        compiler_params=_cparams(("parallel", "arbitrary")),
        name="mixers",
    )(*args)


def _ordered_key(x):
    x = jnp.where(x == 0.0, 0.0, x)
    bits = pltpu.bitcast(x, I32)
    return bits ^ ((bits >> 31) & jnp.int32(0x7FFFFFFF))


NEG_INF_KEY = -2139095041


def _count_rows(key_ref, n_tiles, pred):
    rows = key_ref.shape[1]
    lane = lax.broadcasted_iota(I32, (rows, LANES), 1)
    acc = jnp.zeros((rows, LANES), F32)
    for j in range(n_tiles):
        acc = acc + jnp.where(pred(key_ref[j], j * LANES + lane), 1.0, 0.0)
    return jnp.sum(acc, axis=1, keepdims=True)


def _count_cols(key_ref, n_groups, gsz, pred):
    cols = key_ref.shape[1]
    pos0 = lax.broadcasted_iota(I32, (gsz, cols), 0)

    def body(c, acc):
        blk = key_ref[pl.ds(pl.multiple_of(c * gsz, gsz), gsz), :]
        hit = jnp.where(pred(blk, c * gsz + pos0), 1.0, 0.0)
        return acc + jnp.sum(hit.reshape(gsz // SUBLANES, SUBLANES, cols), axis=0)

    acc = lax.fori_loop(0, n_groups, body, jnp.zeros((SUBLANES, cols), F32))
    return jnp.sum(acc, axis=0, keepdims=True)


def _count_cols16(ref16, n_groups, gsz, pred):
    cols = ref16.shape[1]

    def body(c, acc):
        blk = ref16[pl.ds(pl.multiple_of(c * gsz, gsz), gsz), :].reshape(gsz // PACK16, PACK16, cols)
        hit = jnp.where(pred(blk), jnp.int16(1), jnp.int16(0))
        for j in range(gsz // PACK16):
            acc = acc + hit[j]
        return acc

    acc = lax.fori_loop(0, n_groups, body, jnp.zeros((PACK16, cols), I16))
    return jnp.sum(acc.astype(F32), axis=0, keepdims=True)


def _pack16(t):
    return jnp.broadcast_to(t, (PACK16, t.shape[1])).astype(I16)[None]


def _kth_largest16(count_ge, k, cols):
    t0 = jnp.where(count_ge(_pack16(jnp.zeros((1, cols), I32))) >= k, 0, I16_MIN).astype(I32)

    def bit(i, t):
        cand = t | (jnp.int32(1) << (14 - i))
        return jnp.where(count_ge(_pack16(cand)) >= k, cand, t)

    return lax.fori_loop(0, 15, bit, t0)


def _kth_largest(count, topk):
    thr0 = jnp.where(count(lambda key, pos: key >= 0) >= topk, 0, INT_MIN).astype(I32)

    def thr_bit(i, thr):
        cand = thr | (jnp.int32(1) << (30 - i))
        return jnp.where(count(lambda key, pos: key >= cand) >= topk, cand, thr)

    return lax.fori_loop(0, 31, thr_bit, thr0)


def _tie_bound(count, thr, shape, topk, n_keys):
    need = topk - count(lambda key, pos: key > thr)
    n_eq = count(lambda key, pos: key == thr)
    idx_bits = n_keys.bit_length()
    excess = jnp.max(jnp.where((n_eq > need) & (thr > NEG_INF_KEY), 1.0, 0.0))

    def tie_bound():
        def bound_bit(i, bound):
            cand = bound | (jnp.int32(1) << (idx_bits - 1 - i))
            hits = count(lambda key, pos: (key == thr) & (pos < cand))
            return jnp.where(hits <= need, cand, bound)

        return lax.fori_loop(0, idx_bits, bound_bit, jnp.zeros(shape, I32))

    return lax.cond(excess > 0.0, tie_bound, lambda: jnp.full(shape, 2 ** idx_bits - 1, I32))


def _dsa_prompt_kernel(q_ref, qi_ref, wi_ref, k_ref, vt_ref, ki2_ref, o_ref, key_ref, hi_ref, lo_ref, *acc_refs,
                       tq, kc1, kc, topk, n_keys):
    qb = pl.program_id(1)
    n_chunks = (qb * tq + tq + kc - 1) // kc
    neg_inf = jnp.float32(-jnp.inf)
    q_pos = qb * tq + lax.broadcasted_iota(I32, (kc, tq), 1)
    pos0 = lax.broadcasted_iota(I32, (kc, tq), 0)
    sub = lax.broadcasted_iota(I32, (LANES, tq), 0)

    q_t = q_ref[0].astype(F32).T
    qi_t = qi_ref[0].astype(F32).T
    w_t = wi_ref[0].T

    def head_tile(x_t, h, dim):
        tile = x_t[(h // 2) * LANES:(h // 2 + 1) * LANES]
        return jnp.where((sub // dim) == (h % 2), tile, 0.0).astype(BF16)

    qim = [head_tile(qi_t, h, IDX_DIM) for h in range(IDX_HEADS)]
    qhm = [head_tile(q_t, h, HEAD_DIM) for h in range(ATT_HEADS)]

    q_pos1 = qb * tq + lax.broadcasted_iota(I32, (kc1, tq), 1)
    pos1 = lax.broadcasted_iota(I32, (kc1, tq), 0)

    def score_chunk(c, carry):
        off = pl.multiple_of(c * kc1, kc1)
        ki2 = ki2_ref[0, pl.ds(off, kc1), :]
        acc = jnp.zeros((kc1, tq), F32)
        for h in range(IDX_HEADS):
            acc = acc + w_t[h:h + 1] * jnp.maximum(_dot(ki2, qim[h]), 0.0)
        key = _ordered_key(jnp.where(c * kc1 + pos1 <= q_pos1, acc, neg_inf))
        key_ref[pl.ds(off, kc1), :] = key
        hi_ref[pl.ds(off, kc1), :] = (key >> 16).astype(I16)
        lo_ref[pl.ds(off, kc1), :] = ((key & 0xFFFF) - 2 ** 15).astype(I16)
        return carry

    lax.fori_loop(0, (qb * tq + tq + kc1 - 1) // kc1, score_chunk, 0)

    cnt_hi = functools.partial(_count_cols16, hi_ref, n_chunks, kc)
    t_hi = _kth_largest16(lambda c16: cnt_hi(lambda blk: blk >= c16), float(topk), tq)
    hi16 = _pack16(t_hi)
    need_lo = topk - cnt_hi(lambda blk: blk > hi16)

    def mask_lo(c, carry):
        rows = pl.ds(pl.multiple_of(c * kc, kc), kc)
        hi = hi_ref[rows, :].reshape(kc // PACK16, PACK16, tq)
        lo = lo_ref[rows, :].reshape(kc // PACK16, PACK16, tq)
        lo_ref[rows, :] = jnp.where(hi == hi16, lo, jnp.int16(I16_MIN)).reshape(kc, tq)
        return carry

    lax.fori_loop(0, n_chunks, mask_lo, 0)
    cnt_lo = functools.partial(_count_cols16, lo_ref, n_chunks, kc)
    t_lo = _kth_largest16(lambda c16: cnt_lo(lambda blk: blk >= c16), need_lo, tq)
    thr = (t_hi << 16) + (t_lo + 2 ** 15)
    count = functools.partial(_count_cols, key_ref, n_chunks, kc)
    bound = _tie_bound(count, thr, (1, tq), topk, n_keys)

    for acc_ref in acc_refs:
        acc_ref[...] = jnp.zeros_like(acc_ref)

    def attend_chunk(c, carry):
        m_run, l_run = carry
        off = pl.multiple_of(c * kc, kc)
        key = key_ref[pl.ds(off, kc), :]
        pos = c * kc + pos0
        sel = ((key > thr) | ((key == thr) & (pos < bound))) & (pos <= q_pos)
        logits = []
        for h in range(ATT_HEADS):
            kh = k_ref[0, pl.ds(off, kc), (h // 2) * LANES:(h // 2 + 1) * LANES]
            logits.append(_dot(kh, qhm[h]))
        m_new, l_new, probs, alphas = [], [], [], []
        for h in range(ATT_HEADS):
            logit = jnp.where(sel, logits[h], neg_inf)
            m_h = jnp.maximum(m_run[h], jnp.max(logit, axis=0, keepdims=True))
            m_safe = jnp.where(m_h == neg_inf, 0.0, m_h)
            p = jnp.exp2(logit - m_safe)
            alpha = jnp.exp2(m_run[h] - m_safe)
            l_new.append(alpha * l_run[h] + jnp.sum(p, axis=0, keepdims=True))
            m_new.append(m_h)
            probs.append(p.astype(BF16))
            alphas.append(alpha)
        for h in range(ATT_HEADS):
            pv = _dot(vt_ref[0, c, h * HEAD_DIM:(h + 1) * HEAD_DIM, :], probs[h])
            acc_refs[h][...] = acc_refs[h][...] * alphas[h] + pv
        return tuple(m_new), tuple(l_new)

    init = (tuple(jnp.full((1, tq), neg_inf, F32) for _ in range(ATT_HEADS)),
            tuple(jnp.zeros((1, tq), F32) for _ in range(ATT_HEADS)))
    _, l_fin = lax.fori_loop(0, n_chunks, attend_chunk, init)
    out_t = jnp.concatenate([acc_refs[h][...] * (1.0 / l_fin[h]) for h in range(ATT_HEADS)], axis=0)
    o_ref[0] = out_t.T.astype(o_ref.dtype)


def dsa_prompt(q, qi, wi, k, vt, ki2):
    b, t_len, _ = q.shape
    kc = vt.shape[-1]
    tq = kc
    kc1 = math.gcd(t_len, 2 * kc)
    topk = min(TOPK_MAX, t_len // 4)
    return pl.pallas_call(
        functools.partial(_dsa_prompt_kernel, tq=tq, kc1=kc1, kc=kc, topk=topk, n_keys=t_len),
        grid=(b, t_len // tq),
        in_specs=[pl.BlockSpec((1, tq, W_GROUP), lambda i, j: (i, j, 0)),
                  pl.BlockSpec((1, tq, 2 * W_GROUP), lambda i, j: (i, j, 0)),
                  pl.BlockSpec((1, tq, LANES), lambda i, j: (i, j, 0)),
                  pl.BlockSpec((1, t_len, W_GROUP), lambda i, j: (i, 0, 0)),
                  pl.BlockSpec((1, t_len // kc, W_GROUP, kc), lambda i, j: (i, 0, 0, 0)),
                  pl.BlockSpec((1, t_len, LANES), lambda i, j: (i, 0, 0))],
        out_specs=pl.BlockSpec((1, tq, W_GROUP), lambda i, j: (i, j, 0)),
        out_shape=jax.ShapeDtypeStruct((b, t_len, W_GROUP), BF16),
        scratch_shapes=[pltpu.VMEM((t_len, tq), I32), pltpu.VMEM((t_len, tq), I16), pltpu.VMEM((t_len, tq), I16)]
        + [pltpu.VMEM((HEAD_DIM, tq), F32)] * ATT_HEADS,
        compiler_params=_cparams(("parallel", "arbitrary")),
        name="dsa_prompt",
    )(q, qi, wi, k, vt, ki2)


def _dsa_sample_kernel(*refs, g, t, n_pages, page, topk):
    pt_ref, q_ref, qi_ref, wi_ref, kn_ref, vn_ref, kin_ref = refs[:7]
    n_pg = g * n_pages
    kt_refs, vt_refs, kit_refs = refs[7:7 + n_pg], refs[7 + n_pg:7 + 2 * n_pg], refs[7 + 2 * n_pg:7 + 3 * n_pg]
    o_ref, key_ref = refs[7 + 3 * n_pg], refs[8 + 3 * n_pg]
    del pt_ref
    past = n_pages * page
    s_len = past + LANES
    n_tiles = s_len // LANES
    lane = lax.broadcasted_iota(I32, (t, LANES), 1)
    lane_o = lax.broadcasted_iota(I32, (t, W_GROUP), 1)
    col = lax.broadcasted_iota(I32, (t, s_len), 1)
    causal = col <= past + lax.broadcasted_iota(I32, (t, s_len), 0)
    neg_inf = jnp.float32(-jnp.inf)

    def pad_rows(x):
        return jnp.concatenate([x.astype(F32), jnp.zeros((LANES - t, x.shape[1]), F32)], axis=0).astype(BF16)

    def pages(page_refs, b):
        return jnp.concatenate([page_refs[b * n_pages + p][0, 0] for p in range(n_pages)], axis=1).astype(BF16)

    for b in range(g):
        qi = qi_ref[b].astype(F32)
        lhs = jnp.concatenate(
            [jnp.where((lane // IDX_DIM) == (h % 2), qi[:, (h // 2) * LANES:(h // 2 + 1) * LANES], 0.0)
             for h in range(IDX_HEADS)], axis=0).astype(BF16)
        kit = pages(kit_refs, b)
        dots = jnp.concatenate([_dot(lhs, jnp.concatenate([kit, kit], axis=0)),
                                _dot_nt(lhs, pad_rows(kin_ref[b]))], axis=1)
        wi = wi_ref[b]
        acc = jnp.zeros((t, s_len), F32)
        for h in range(IDX_HEADS):
            acc = acc + wi[:, h:h + 1] * jnp.maximum(dots[h * t:(h + 1) * t], 0.0)
        key = _ordered_key(jnp.where(causal, acc, neg_inf))
        for j in range(n_tiles):
            key_ref[j, b * t:(b + 1) * t, :] = key[:, j * LANES:(j + 1) * LANES]

    count = functools.partial(_count_rows, key_ref, n_tiles)
    thr = _kth_largest(count, topk)
    bound = _tie_bound(count, thr, (g * t, 1), topk, s_len)

    for b in range(g):
        rows = slice(b * t, (b + 1) * t)
        key = jnp.concatenate([key_ref[j, rows, :] for j in range(n_tiles)], axis=1)
        sel = ((key > thr[rows]) | ((key == thr[rows]) & (col < bound[rows]))) & causal
        q = q_ref[b].astype(F32)
        lhs = jnp.concatenate([jnp.where((lane_o // HEAD_DIM) == h, q, 0.0) for h in range(ATT_HEADS)],
                              axis=0).astype(BF16)
        logit = jnp.concatenate([_dot(lhs, pages(kt_refs, b)), _dot_nt(lhs, pad_rows(kn_ref[b]))], axis=1)
        logit = jnp.where(jnp.concatenate([sel] * ATT_HEADS, axis=0), logit, neg_inf)
        p = jnp.exp2(logit - jnp.max(logit, axis=1, keepdims=True))
        pb = p.astype(BF16)
        res = _dot_nt(pb[:, :past], pages(vt_refs, b)) + _dot(pb[:, past:], pad_rows(vn_ref[b]))
        res = res / jnp.sum(p, axis=1, keepdims=True)
        out = jnp.zeros((t, W_GROUP), F32)
        for h in range(ATT_HEADS):
            out = jnp.where((lane_o // HEAD_DIM) == h, res[h * t:(h + 1) * t], out)
        o_ref[b] = out.astype(o_ref.dtype)


def dsa_sample(q, qi, wi, k_new, v_new, ki2_new, kt_pages, vt_pages, kit_pages, page_table, layer, *, g=4):
    b, t_len, _ = q.shape
    n_pages = page_table.shape[1]
    page = kt_pages.shape[-1]
    g = math.gcd(g, b)
    topk = min(TOPK_MAX, (n_pages * page + t_len) // 4)

    def spec(width):
        return pl.BlockSpec((g, t_len, width), lambda i, pt: (i, 0, 0))

    def page_specs(rows):
        return [pl.BlockSpec((1, 1, rows, page),
                             lambda i, pt, s=s, p=p: (pt[(i * g + s) * n_pages + p], layer, 0, 0))
                for s in range(g) for p in range(n_pages)]

    grid_spec = pltpu.PrefetchScalarGridSpec(
        num_scalar_prefetch=1,
        grid=(b // g,),
        in_specs=[spec(W_GROUP), spec(2 * W_GROUP), spec(LANES), spec(W_GROUP), spec(W_GROUP), spec(LANES)]
        + page_specs(W_GROUP) + page_specs(W_GROUP) + page_specs(IDX_DIM),
        out_specs=spec(W_GROUP),
        scratch_shapes=[pltpu.VMEM((n_pages + 1, g * t_len, LANES), I32)],
    )
    n_pg = g * n_pages
    return pl.pallas_call(
        functools.partial(_dsa_sample_kernel, g=g, t=t_len, n_pages=n_pages, page=page, topk=topk),
        grid_spec=grid_spec,
        out_shape=jax.ShapeDtypeStruct((b, t_len, W_GROUP), BF16),
        compiler_params=_cparams(("parallel",)),
        name="dsa_sample",
    )(page_table.reshape(-1), q, qi, wi, k_new, v_new, ki2_new,
      *([kt_pages] * n_pg), *([vt_pages] * n_pg), *([kit_pages] * n_pg))


def _mem_attn_kernel(q_ref, mk_ref, mv_ref, o_ref, *, g):
    for b in range(g):
        q = q_ref[b]
        outs = []
        for h in range(MEM_HEADS):
            sl = slice(h * MEM_HEAD_DIM, (h + 1) * MEM_HEAD_DIM)
            logit = _dot_nt(q[:, sl], mk_ref[b, 0, :, sl].astype(BF16)) * (MEM_HEAD_DIM ** -0.5)
            m = jnp.max(logit, axis=1, keepdims=True)
            p = jnp.exp(logit - m)
            p = p / jnp.sum(p, axis=1, keepdims=True)
            outs.append(_dot(p.astype(BF16), mv_ref[b, 0, :, sl].astype(BF16)))
        o_ref[b] = jnp.concatenate(outs, axis=-1).astype(o_ref.dtype)


def mem_attention(q, mk, mv, layer, *, tq=512, g=1):
    b, t_len, d = q.shape
    m_len = mk.shape[2]
    if t_len >= tq:
        g = 1
    else:
        tq, g = t_len, math.gcd(g, b)
    mem_spec = pl.BlockSpec((g, 1, m_len, d), lambda i, j: (i, layer, 0, 0))
    return pl.pallas_call(
        functools.partial(_mem_attn_kernel, g=g),
        grid=(b // g, t_len // tq),
        in_specs=[pl.BlockSpec((g, tq, d), lambda i, j: (i, j, 0)), mem_spec, mem_spec],
        out_specs=pl.BlockSpec((g, tq, d), lambda i, j: (i, j, 0)),
        out_shape=jax.ShapeDtypeStruct((b, t_len, d), BF16),
        compiler_params=_cparams(("parallel", "arbitrary")),
        name="mem_attention",
    )(q, mk, mv)


def _prep_layer(l, p):
    w = p['w_in'][l]
    cuts = [0]
    for s in (W_GROUP,) * 9 + (IDX_HEADS * IDX_DIM, IDX_DIM, IDX_HEADS):
        cuts.append(cuts[-1] + s)
    piece = lambda i: w[:, cuts[i]:cuts[i + 1]]
    zeros = lambda n: jnp.zeros((D_MODEL, n), w.dtype)
    w_in = jnp.concatenate(
        [piece(i) for i in range(6)] + [piece(6), piece(7), piece(9), piece(8), piece(10), zeros(LANES - IDX_DIM),
                                        piece(11), zeros(LANES - IDX_HEADS)], axis=1).astype(BF16)
    pool_w = p['pool_w'][l]
    pool_bd = jnp.zeros((W_GROUP, W_GROUP), F32)
    for g in range(len(POOL_WINDOWS)):
        pool_bd = pool_bd.at[g * POOL_CH:(g + 1) * POOL_CH, g * POOL_CH:(g + 1) * POOL_CH].set(pool_w[g])
    return {
        'norm_mix': p['norm_mix'][l], 'w_in': w_in,
        'conv_w': p['conv_w'][l], 'pool_w_bd': pool_bd.astype(BF16),
        'pool_scale': p['pool_scale'][l].reshape(1, W_GROUP), 'chunk_norm': p['chunk_norm'][l].reshape(1, W_GROUP),
        'chunk_ws': p['chunk_ws'][l], 'chunk_b': p['chunk_b'][l],
        'w_out_abc': p['w_out'][l][:3 * W_GROUP].astype(BF16), 'w_out_d': p['w_out'][l][3 * W_GROUP:].astype(BF16),
        'norm_mem': p['norm_mem'][l], 'wq_mem': p['wq_mem'][l].astype(BF16), 'wo_mem': p['wo_mem'][l].astype(BF16),
        'norm_ffn': p['norm_ffn'][l],
    }


def _chunk_consts(lw, chunk):
    mask = jnp.tril(jnp.ones((chunk, chunk), dtype=bool))
    wm = jnp.where(mask[None], lw['chunk_ws'][:, :chunk, :chunk], 0).astype(BF16)
    bias = jnp.repeat(lw['chunk_b'][:, :chunk].T, W_GROUP // CHUNK_HEADS, axis=1)
    return wm, bias


def _rope_tables(pos):
    half = HEAD_DIM // 2
    inv = ROPE_THETA ** (-jnp.arange(half, dtype=F32) / half)
    ang = pos.astype(F32)[:, None] * inv[None, :]
    cos, sin = jnp.cos(ang), jnp.sin(ang)
    cos_t = jnp.concatenate([cos, cos] * (LANES // HEAD_DIM), axis=1)
    sin_t = jnp.concatenate([-sin, sin] * (LANES // HEAD_DIM), axis=1)
    return cos_t, sin_t


def _run_trunk(x, pos0, conv_prev, pool_prev, past, mem_k, mem_v, layers, ffn_w, norm_final):
    b, t_len, d = x.shape
    n = b * t_len
    depth = len(layers)
    from_state = past is not None
    pos = pos0 + jnp.arange(t_len, dtype=I32)
    cos_t, sin_t = _rope_tables(pos)
    h = x.reshape(n, d)
    ks, vs, kis, convs, pools, cvs = [], [], [], [], [], []
    for l, lw in enumerate(layers):
        proj = norm_matmul(h, lw['norm_mix'], lw['w_in']).reshape(b, t_len, P_PAD)
        chunk = min(t_len, CHUNK)
        wm, cbias = _chunk_consts(lw, chunk)
        lw_m = dict(lw, chunk_wm=wm, chunk_bias=cbias)
        outs = mixers(proj, conv_prev[:, l] if from_state else None, pool_prev[:, l] if from_state else None,
                      cos_t, sin_t, lw_m, pos0=pos0, from_state=from_state, emit_cv=from_state)
        mix, q, k, kb, v, vb, qi, ki, ki2, wi, conv_s, pool_s = outs[:12]
        if from_state:
            cvs.append(outs[12])
            yd = dsa_sample(q, qi, wi, kb, vb, ki2, *past, l)
        else:
            yd = dsa_prompt(q, qi, wi, kb, vb, ki2)
        h = matmul_res([mix.reshape(n, 3 * W_GROUP), yd.reshape(n, W_GROUP)], [lw['w_out_abc'], lw['w_out_d']], h)
        qm = norm_matmul(h, lw['norm_mem'], lw['wq_mem'], out_dtype=BF16).reshape(b, t_len, d)
        om = mem_attention(qm, mem_k, mem_v, l)
        h = matmul_res([om.reshape(n, d)], [lw['wo_mem']], h)
        last = l == depth - 1
        fw = ffn_w[l]
        if fw['kind'] == 'dense':
            h = ffn_dense(h, lw['norm_ffn'], fw['w1'], fw['w3'], fw['w2'], norm_final, final_norm=last)
        else:
            h = moe_ffn(h, lw['norm_ffn'], fw['router'], fw['w1'], fw['w3'], fw['w2'], norm_final, final_norm=last)
        ks.append(k.reshape(b, t_len, ATT_HEADS, HEAD_DIM))
        vs.append(v.reshape(b, t_len, ATT_HEADS, HEAD_DIM))
        kis.append(ki[:, :, :IDX_DIM])
        convs.append(conv_s)
        pools.append(pool_s)
    st = lambda xs: jnp.stack(xs, axis=1)
    y = h.reshape(b, t_len, d)
    return y, st(ks), st(vs), st(kis), st(convs), st(pools), (st(cvs) if cvs else None)


def kernel(x_prompt, x_sample, mem_prompt, cache_k, cache_v, cache_kidx, cache_mem_k, cache_mem_v, state_conv, state_pool, page_table, norm_mix, w_in, conv_w, pool_w, pool_scale, chunk_norm, chunk_ws, chunk_b, w_out, norm_mem, wq_mem, wk_mem, wv_mem, wo_mem, norm_ffn, w1_dense, w3_dense, w2_dense, router, w1_moe, w3_moe, w2_moe, norm_final):
    p = {
        'norm_mix': norm_mix, 'w_in': w_in, 'conv_w': conv_w, 'pool_w': pool_w, 'pool_scale': pool_scale,
        'chunk_norm': chunk_norm, 'chunk_ws': chunk_ws, 'chunk_b': chunk_b, 'w_out': w_out, 'norm_mem': norm_mem,
        'wq_mem': wq_mem, 'wo_mem': wo_mem, 'norm_ffn': norm_ffn,
    }
    depth = norm_mix.shape[0]
    layers = [_prep_layer(l, p) for l in range(depth)]
    ffn_w = []
    for l in range(depth):
        j = l // 2
        if l % 2 == 0:
            ffn_w.append({'kind': 'dense', 'w1': w1_dense[j].astype(BF16), 'w3': w3_dense[j].astype(BF16),
                          'w2': w2_dense[j].astype(BF16)})
        else:
            router_pad = jnp.concatenate([router[j], jnp.zeros((D_MODEL, LANES - N_EXPERTS), F32)], axis=1)
            ffn_w.append({'kind': 'moe', 'router': router_pad, 'w1': w1_moe[j].astype(BF16),
                          'w3': w3_moe[j].astype(BF16), 'w2': w2_moe[j].astype(BF16)})

    bp, tp, d = x_prompt.shape
    n_mem = mem_prompt.shape[1]
    w_kv = jnp.concatenate([wk_mem[l] for l in range(depth)] + [wv_mem[l] for l in range(depth)], axis=1).astype(BF16)
    mem_kv = norm_matmul(mem_prompt.reshape(bp * n_mem, d), jnp.ones((d,), F32), w_kv, norm=False)
    mem_kv = mem_kv.reshape(bp, n_mem, 2, depth, d)
    mem_k_p = jnp.moveaxis(mem_kv[:, :, 0], 2, 1)
    mem_v_p = jnp.moveaxis(mem_kv[:, :, 1], 2, 1)
    y_p, k_p, v_p, ki_p, conv_p, pool_p, _ = _run_trunk(
        x_prompt, 0, None, None, None, mem_k_p, mem_v_p, layers, ffn_w, norm_final)

    bs, ts, _ = x_sample.shape
    n_pages, page = page_table.shape[1], cache_k.shape[2]
    past_len = n_pages * page

    n_phys = cache_k.shape[0]
    kt_pages = jnp.transpose(cache_k, (0, 1, 3, 4, 2)).reshape(n_phys, depth, W_GROUP, page)
    vt_pages = jnp.transpose(cache_v, (0, 1, 3, 4, 2)).reshape(n_phys, depth, W_GROUP, page)
    kit_pages = jnp.transpose(cache_kidx, (0, 1, 3, 2))
    past = (kt_pages, vt_pages, kit_pages, page_table)
    y_s, k_s, v_s, ki_s, conv_s, pool_s, cv_s = _run_trunk(
        x_sample, past_len, state_conv, state_pool, past, cache_mem_k.reshape(bs, depth, n_mem, d),
        cache_mem_v.reshape(bs, depth, n_mem, d), layers, ffn_w, norm_final)

    mem_shape = (bp, depth, n_mem, MEM_HEADS, MEM_HEAD_DIM)
    return (y_p, y_s, k_p, v_p, ki_p, mem_k_p.reshape(mem_shape), mem_v_p.reshape(mem_shape), conv_p, pool_p,
            k_s, v_s, ki_s, conv_s, pool_s, cv_s)
```

```python
import functools
import math

import jax
import jax.numpy as jnp
from jax import lax
from jax.experimental import pallas as pl
from jax.experimental.pallas import tpu as pltpu

F32 = jnp.float32
BF16 = jnp.bfloat16
I32 = jnp.int32
I16 = jnp.int16

D_MODEL = 1024
W_GROUP = 256
CONV_W = 3
POOL_WINDOWS = (2, 4, 8, 16)
POOL_CH = 64
POOL_BUF = 15
CHUNK = 128
CHUNK_HEADS = 4
ATT_HEADS = 4
HEAD_DIM = 64
IDX_HEADS = 8
IDX_DIM = 64
IDX_W_SCALE = (IDX_HEADS ** -0.5) * (IDX_DIM ** -0.5)
TOPK_MAX = 256
ROPE_THETA = 10000.0
MEM_HEADS = 4
MEM_HEAD_DIM = 256
D_FF = 3584
N_EXPERTS = 8
EPS = 1e-6

LANES = 128
SUBLANES = 8
PACK16 = 16
PREV_ROWS = 16
P_PAD = 3072
HALF = P_PAD // 2
OFF_Q, OFF_K, OFF_QI, OFF_V, OFF_KI, OFF_WI = 1536, 1792, 2048, 2560, 2816, 2944
VMEM_LIMIT = 56 * 1024 * 1024

INT_MIN = -2 ** 31
I16_MIN = -2 ** 15
Q_SCALE = (HEAD_DIM ** -0.5) * math.log2(math.e)


def _cparams(sem):
    return pltpu.CompilerParams(dimension_semantics=sem, vmem_limit_bytes=VMEM_LIMIT)


def _rms(x, g):
    ms = jnp.mean(x * x, axis=-1, keepdims=True)
    return x * lax.rsqrt(ms + EPS) * g


def _dot(a, b):
    return jnp.dot(a, b, preferred_element_type=F32)


def _dot_nt(a, b):
    return lax.dot_general(a, b, (((1,), (1,)), ((), ())), preferred_element_type=F32)


def _norm_matmul_kernel(x_ref, g_ref, w_ref, o_ref, xn_ref, *, norm):
    @pl.when(pl.program_id(1) == 0)
    def _():
        x = x_ref[...]
        if norm:
            x = _rms(x, g_ref[...])
        xn_ref[...] = x.astype(BF16)

    o_ref[...] = _dot(xn_ref[...], w_ref[...]).astype(o_ref.dtype)


def norm_matmul(x, g, w, *, norm=True, out_dtype=F32, tm=1024, tn=1536):
    m, k = x.shape
    n = w.shape[1]
    tm = min(tm, m)
    tn = tn if n % tn == 0 else math.gcd(n, 1024)
    return pl.pallas_call(
        functools.partial(_norm_matmul_kernel, norm=norm),
        grid=(m // tm, n // tn),
        in_specs=[pl.BlockSpec((tm, k), lambda i, j: (i, 0)),
                  pl.BlockSpec((1, k), lambda i, j: (0, 0)),
                  pl.BlockSpec((k, tn), lambda i, j: (0, j))],
        out_specs=pl.BlockSpec((tm, tn), lambda i, j: (i, j)),
        out_shape=jax.ShapeDtypeStruct((m, n), out_dtype),
        scratch_shapes=[pltpu.VMEM((tm, k), BF16)],
        compiler_params=_cparams(("parallel", "arbitrary")),
        name="norm_matmul",
    )(x, g.reshape(1, k), w)


def _matmul_res_kernel(*refs, n_in):
    a_refs, w_refs = refs[:n_in], refs[n_in:2 * n_in]
    r_ref, o_ref = refs[2 * n_in], refs[2 * n_in + 1]
    acc = r_ref[...]
    for a, w in zip(a_refs, w_refs):
        acc = acc + _dot(a[...].astype(BF16), w[...])
    o_ref[...] = acc


def matmul_res(a_list, w_list, res, *, tm=1024):
    m, n = res.shape
    tm = min(tm, m)
    n_in = len(a_list)
    in_specs = [pl.BlockSpec((tm, a.shape[1]), lambda i: (i, 0)) for a in a_list]
    in_specs += [pl.BlockSpec(w.shape, lambda i: (0, 0)) for w in w_list]
    in_specs += [pl.BlockSpec((tm, n), lambda i: (i, 0))]
    return pl.pallas_call(
        functools.partial(_matmul_res_kernel, n_in=n_in),
        grid=(m // tm,),
        in_specs=in_specs,
        out_specs=pl.BlockSpec((tm, n), lambda i: (i, 0)),
        out_shape=jax.ShapeDtypeStruct((m, n), F32),
        compiler_params=_cparams(("parallel",)),
        name="matmul_res",
    )(*a_list, *w_list, res)


def _ffn_kernel(x_ref, g_ref, w1_ref, w3_ref, w2_ref, gf_ref, o_ref, xn_ref, acc_ref, *, final_norm):
    f = pl.program_id(1)

    @pl.when(f == 0)
    def _():
        xn_ref[...] = _rms(x_ref[...], g_ref[...]).astype(BF16)
        acc_ref[...] = jnp.zeros_like(acc_ref)

    xn = xn_ref[...]
    a = _dot(xn, w1_ref[...])
    b = _dot(xn, w3_ref[...])
    hidden = (a * jax.nn.sigmoid(a) * b).astype(BF16)
    acc_ref[...] += _dot(hidden, w2_ref[...])

    @pl.when(f == pl.num_programs(1) - 1)
    def _():
        y = x_ref[...] + acc_ref[...]
        if final_norm:
            y = _rms(y, gf_ref[...])
        o_ref[...] = y


def ffn_dense(x, g, w1, w3, w2, g_final, *, final_norm, tm=1024, tf=896):
    m, d = x.shape
    ff = w1.shape[1]
    tm = min(tm, m)
    return pl.pallas_call(
        functools.partial(_ffn_kernel, final_norm=final_norm),
        grid=(m // tm, ff // tf),
        in_specs=[pl.BlockSpec((tm, d), lambda i, f: (i, 0)),
                  pl.BlockSpec((1, d), lambda i, f: (0, 0)),
                  pl.BlockSpec((d, tf), lambda i, f: (0, f)),
                  pl.BlockSpec((d, tf), lambda i, f: (0, f)),
                  pl.BlockSpec((tf, d), lambda i, f: (f, 0)),
                  pl.BlockSpec((1, d), lambda i, f: (0, 0))],
        out_specs=pl.BlockSpec((tm, d), lambda i, f: (i, 0)),
        out_shape=jax.ShapeDtypeStruct((m, d), F32),
        scratch_shapes=[pltpu.VMEM((tm, d), BF16), pltpu.VMEM((tm, d), F32)],
        compiler_params=_cparams(("parallel", "arbitrary")),
        name="ffn_dense",
    )(x, g.reshape(1, d), w1, w3, w2, g_final.reshape(1, d))


def _router_kernel(x_ref, g_ref, r_ref, hn_ref, e_ref, gate_ref):
    hn = _rms(x_ref[...], g_ref[...])
    hn_ref[...] = hn.astype(BF16)
    logits = jnp.dot(hn, r_ref[...], preferred_element_type=F32, precision=lax.Precision.HIGHEST)
    lane = lax.broadcasted_iota(I32, logits.shape, 1)
    neg = jnp.float32(-jnp.inf)
    l1 = jnp.where(lane < N_EXPERTS, logits, neg)
    m1 = jnp.max(l1, axis=1, keepdims=True)
    i1 = jnp.min(jnp.where(l1 == m1, lane, LANES), axis=1, keepdims=True)
    l2 = jnp.where(lane == i1, neg, l1)
    m2 = jnp.max(l2, axis=1, keepdims=True)
    i2 = jnp.min(jnp.where(l2 == m2, lane, LANES), axis=1, keepdims=True)
    t = jnp.exp(m2 - m1)
    denom = 1.0 + t
    e_ref[...] = jnp.where(lane == 0, i1, jnp.where(lane == 1, i2, 0))
    gate_ref[...] = jnp.where(lane == 0, 1.0 / denom, jnp.where(lane == 1, t / denom, 0.0))


def moe_router(x, g, router_pad, *, tm=512):
    m, d = x.shape
    tm = min(tm, m)
    return pl.pallas_call(
        _router_kernel,
        grid=(m // tm,),
        in_specs=[pl.BlockSpec((tm, d), lambda i: (i, 0)),
                  pl.BlockSpec((1, d), lambda i: (0, 0)),
                  pl.BlockSpec((d, LANES), lambda i: (0, 0))],
        out_specs=[pl.BlockSpec((tm, d), lambda i: (i, 0)),
                   pl.BlockSpec((tm, LANES), lambda i: (i, 0)),
                   pl.BlockSpec((tm, LANES), lambda i: (i, 0))],
        out_shape=[jax.ShapeDtypeStruct((m, d), BF16),
                   jax.ShapeDtypeStruct((m, LANES), I32),
                   jax.ShapeDtypeStruct((m, LANES), F32)],
        compiler_params=_cparams(("parallel",)),
        name="moe_router",
    )(x, g.reshape(1, d), router_pad)


def _expert_kernel(be_ref, nb_ref, xs_ref, gate_ref, w1_ref, w3_ref, w2_ref, o_ref, acc_ref):
    j, f = pl.program_id(0), pl.program_id(1)
    live = j < nb_ref[0]

    @pl.when(f == 0)
    def _():
        acc_ref[...] = jnp.zeros_like(acc_ref)

    @pl.when(live)
    def _():
        xs = xs_ref[...]
        a = _dot(xs, w1_ref[0])
        b = _dot(xs, w3_ref[0])
        hidden = (a * jax.nn.sigmoid(a) * b).astype(BF16)
        acc_ref[...] += _dot(hidden, w2_ref[0])

    @pl.when(f == pl.num_programs(1) - 1)
    def _():
        o_ref[...] = acc_ref[...] * gate_ref[...]


def moe_experts(block_e, n_live, xs, slot_gate, w1, w3, w2, *, bm, tf=896):
    n_slots, d = xs.shape
    ff = w1.shape[2]
    n_blocks = n_slots // bm
    grid_spec = pltpu.PrefetchScalarGridSpec(
        num_scalar_prefetch=2,
        grid=(n_blocks, ff // tf),
        in_specs=[pl.BlockSpec((bm, d), lambda j, f, be, nb: (j, 0)),
                  pl.BlockSpec((bm, 1), lambda j, f, be, nb: (j, 0)),
                  pl.BlockSpec((1, d, tf), lambda j, f, be, nb: (be[j], 0, f)),
                  pl.BlockSpec((1, d, tf), lambda j, f, be, nb: (be[j], 0, f)),
                  pl.BlockSpec((1, tf, d), lambda j, f, be, nb: (be[j], f, 0))],
        out_specs=pl.BlockSpec((bm, d), lambda j, f, be, nb: (j, 0)),
        scratch_shapes=[pltpu.VMEM((bm, d), F32)],
    )
    return pl.pallas_call(
        _expert_kernel,
        grid_spec=grid_spec,
        out_shape=jax.ShapeDtypeStruct((n_slots, d), F32),
        compiler_params=_cparams(("parallel", "arbitrary")),
        name="moe_experts",
    )(block_e, n_live, xs, slot_gate, w1, w3, w2)


def _combine_kernel(x_ref, a_ref, b_ref, gf_ref, o_ref, *, final_norm):
    y = x_ref[...] + (a_ref[...] + b_ref[...])
    if final_norm:
        y = _rms(y, gf_ref[...])
    o_ref[...] = y


def moe_combine(x, ya, yb, g_final, *, final_norm, tm=512):
    m, d = x.shape
    tm = min(tm, m)
    spec = pl.BlockSpec((tm, d), lambda i: (i, 0))
    return pl.pallas_call(
        functools.partial(_combine_kernel, final_norm=final_norm),
        grid=(m // tm,),
        in_specs=[spec, spec, spec, pl.BlockSpec((1, d), lambda i: (0, 0))],
        out_specs=spec,
        out_shape=jax.ShapeDtypeStruct((m, d), F32),
        compiler_params=_cparams(("parallel",)),
        name="moe_combine",
    )(x, ya, yb, g_final.reshape(1, d))


def _compact_kernel(ib_ref, ic_ref, first_ref, total_ref, tok_ref, hn_ref, o_ref, acc_ref, *, tc):
    i = pl.program_id(0)

    @pl.when(first_ref[i] == 1)
    def _():
        acc_ref[...] = jnp.zeros_like(acc_ref)

    @pl.when(i < total_ref[0])
    def _():
        local = tok_ref[...] - ic_ref[i] * tc
        lane = lax.broadcasted_iota(I32, (tok_ref.shape[0], tc), 1)
        onehot = jnp.where(local == lane, 1.0, 0.0).astype(BF16)
        acc_ref[...] += _dot(onehot, hn_ref[...])

    o_ref[...] = acc_ref[...].astype(o_ref.dtype)


def moe_compact(item_block, item_chunk, item_first, n_items, slot_tok, hn, *, bm, tc):
    n_slots = slot_tok.shape[0]
    n, d = hn.shape
    n_max = item_block.shape[0]
    grid_spec = pltpu.PrefetchScalarGridSpec(
        num_scalar_prefetch=4,
        grid=(n_max,),
        in_specs=[pl.BlockSpec((bm, 1), lambda i, ib, ic, fr, tot: (ib[i], 0)),
                  pl.BlockSpec((tc, d), lambda i, ib, ic, fr, tot: (ic[i], 0))],
        out_specs=pl.BlockSpec((bm, d), lambda i, ib, ic, fr, tot: (ib[i], 0)),
        scratch_shapes=[pltpu.VMEM((bm, d), F32)],
    )
    return pl.pallas_call(
        functools.partial(_compact_kernel, tc=tc),
        grid_spec=grid_spec,
        out_shape=jax.ShapeDtypeStruct((n_slots, d), BF16),
        compiler_params=_cparams(("arbitrary",)),
        name="moe_compact",
    )(item_block, item_chunk, item_first, n_items, slot_tok.reshape(n_slots, 1), hn)


def moe_ffn(x, g, router_pad, w1, w3, w2, g_final, *, final_norm):
    n, d = x.shape
    bm = 512 if n >= 4096 else 128
    tc = min(n, 512)
    hn, e12, g12 = moe_router(x, g, router_pad)
    e_flat = e12[:, :2].reshape(-1)
    g_flat = g12[:, :2].reshape(-1)
    n_assign = 2 * n
    order = jnp.argsort(e_flat, stable=True).astype(I32)
    rank = jnp.argsort(order).astype(I32)
    counts = jnp.sum(e_flat[:, None] == jnp.arange(N_EXPERTS, dtype=I32)[None, :], axis=0).astype(I32)
    padded = (counts + bm - 1) // bm * bm
    start = jnp.cumsum(counts) - counts
    pend = jnp.cumsum(padded)
    pstart = pend - padded
    n_blocks = -(-n_assign // bm) + N_EXPERTS
    n_slots = n_blocks * bm
    block_e = jnp.minimum(
        jnp.searchsorted(pend, jnp.arange(n_blocks, dtype=I32) * bm, side='right'), N_EXPERTS - 1).astype(I32)
    n_live = (pend[-1] // bm).astype(I32).reshape(1)
    slot_e = jnp.repeat(block_e, bm)
    slot_idx = jnp.arange(n_slots, dtype=I32) - pstart[slot_e]
    slot_live = slot_idx < counts[slot_e]
    slot_src = order[jnp.clip(start[slot_e] + slot_idx, 0, n_assign - 1)]
    slot_tok = jnp.where(slot_live, slot_src // 2, -1)
    slot_gate = jnp.where(slot_live, g_flat[slot_src], 0.0)
    assign_slot = pstart[e_flat] + rank - start[e_flat]
    tok_b = slot_tok.reshape(n_blocks, bm)
    c_lo = jnp.maximum(tok_b[:, 0], 0) // tc
    c_hi = jnp.maximum(jnp.max(tok_b, axis=1), 0) // tc
    n_items_b = c_hi - c_lo + 1
    item_end = jnp.cumsum(n_items_b)
    total = item_end[-1]
    n_items_max = n_blocks + N_EXPERTS * (n // tc)
    item_id = jnp.arange(n_items_max, dtype=I32)
    item = jnp.minimum(item_id, total - 1)
    item_block = jnp.searchsorted(item_end, item, side='right').astype(I32)
    item_off = item - (item_end - n_items_b)[item_block]
    item_chunk = (c_lo[item_block] + item_off).astype(I32)
    item_first = ((item_off == 0) & (item_id < total)).astype(I32)
    xs = moe_compact(item_block, item_chunk, item_first, total.astype(I32).reshape(1), slot_tok, hn, bm=bm, tc=tc)
    ys = moe_experts(block_e, n_live, xs, slot_gate.reshape(n_slots, 1), w1, w3, w2, bm=bm)
    pair = assign_slot.reshape(n, 2)
    return moe_combine(x, ys[pair[:, 0]], ys[pair[:, 1]], g_final, final_norm=final_norm)


def _swap_half_heads(x):
    n = x.shape[-1]
    lane = lax.broadcasted_iota(I32, x.shape, x.ndim - 1)
    fwd = pltpu.roll(x, n - HEAD_DIM // 2, x.ndim - 1)
    bwd = pltpu.roll(x, HEAD_DIM // 2, x.ndim - 1)
    return jnp.where(lane % HEAD_DIM < HEAD_DIM // 2, fwd, bwd)


def _rope(x, cos, sin):
    reps = x.shape[-1] // LANES
    if reps > 1:
        cos = jnp.concatenate([cos] * reps, axis=-1)
        sin = jnp.concatenate([sin] * reps, axis=-1)
    return x * cos + _swap_half_heads(x) * sin


def _mixer_kernel(*refs, tt, chunk, pos0, from_state, emit_cv):
    it = iter(refs)
    xa_ref, xb_ref = next(it), next(it)
    if from_state:
        convp_ref, poolp_ref = next(it), next(it)
    else:
        prev_ref = next(it)
    cos_ref, sin_ref = next(it), next(it)
    convw_ref, poolw_ref, pscale_ref, cnorm_ref, wm_ref, cbias_ref = (next(it) for _ in range(6))
    mix_ref, q_ref, k_ref, kb_ref, v_ref, vb_ref, qi_ref, ki_ref, ki2_ref, wi_ref, convs_ref, pools_ref = (
        next(it) for _ in range(12))
    cv_ref = next(it) if emit_cv else None
    ybuf, xbuf = next(it), next(it)

    t = pl.program_id(1)
    xa = xa_ref[0]
    a_in, a_b, a_c = xa[:, 0:256], xa[:, 256:512], xa[:, 512:768]
    b_in, c_u, c_v = xa[:, 768:1024], xa[:, 1024:1280], xa[:, 1280:1536]

    y = a_c * a_in
    if from_state:
        ybuf[PREV_ROWS - 2:PREV_ROWS, :] = convp_ref[0]
        xbuf[0:1, :] = jnp.zeros((1, W_GROUP), F32)
        xbuf[1:PREV_ROWS, :] = poolp_ref[0]
    else:
        first = (t == 0).astype(F32)
        prev = prev_ref[0] * (1.0 - first)
        ybuf[0:PREV_ROWS, :] = prev[:, 512:768] * prev[:, 0:256]
        xbuf[0:PREV_ROWS, :] = prev[:, 768:1024]
    ybuf[PREV_ROWS:PREV_ROWS + tt, :] = y
    xbuf[PREV_ROWS:PREV_ROWS + tt, :] = b_in

    cw = convw_ref[...]
    z = cw[0:1, :] * ybuf[PREV_ROWS - 2:PREV_ROWS - 2 + tt, :]
    z = z + cw[1:2, :] * ybuf[PREV_ROWS - 1:PREV_ROWS - 1 + tt, :]
    z = z + cw[2:3, :] * y
    ya = a_b * z
    convs_ref[0] = ybuf[PREV_ROWS + tt - 2:PREV_ROWS + tt, :]

    lane = lax.broadcasted_iota(I32, (tt, W_GROUP), 1)
    row = lax.broadcasted_iota(I32, (tt, W_GROUP), 0)
    pos = (pos0 + t * tt + row).astype(F32)
    run = b_in
    mean = jnp.zeros((tt, W_GROUP), F32)
    for j in range(1, POOL_WINDOWS[-1] + 1):
        if j > 1:
            run = run + xbuf[PREV_ROWS - (j - 1):PREV_ROWS - (j - 1) + tt, :]
        if j in POOL_WINDOWS:
            grp = POOL_WINDOWS.index(j)
            cnt = jnp.minimum(pos + 1.0, float(j))
            mean = jnp.where(lane // POOL_CH == grp, run / cnt, mean)
    dlt = (mean - b_in).astype(BF16)
    yb = _dot(dlt, poolw_ref[...]) * pscale_ref[...]
    pools_ref[0] = xbuf[PREV_ROWS + tt - POOL_BUF:PREV_ROWS + tt, :]

    vn = _rms(c_v, cnorm_ref[...])
    if emit_cv:
        cv_ref[0] = vn
    vnb = vn.astype(BF16)
    lane_c = lax.broadcasted_iota(I32, (chunk, W_GROUP), 1)
    parts = []
    for c in range(tt // chunk):
        vc = vnb[c * chunk:(c + 1) * chunk, :]
        s = cbias_ref[...]
        for hd in range(CHUNK_HEADS):
            sh = _dot(wm_ref[hd], vc)
            s = s + jnp.where(lane_c // (W_GROUP // CHUNK_HEADS) == hd, sh, 0.0)
        parts.append(s)
    s_all = parts[0] if len(parts) == 1 else jnp.concatenate(parts, axis=0)
    yc = c_u * s_all

    mix_ref[0] = jnp.concatenate([ya, yb, yc], axis=-1).astype(BF16)

    xb = xb_ref[0]
    cos, sin = cos_ref[...], sin_ref[...]
    q = _rope(xb[:, 0:256], cos, sin)
    k = _rope(xb[:, 256:512], cos, sin)
    qi = _rope(xb[:, 512:1024], cos, sin)
    v = xb[:, 1024:1280]
    ki = _rope(xb[:, 1280:1408], cos, sin)
    wi = xb[:, 1408:1536] * IDX_W_SCALE
    q_ref[0] = (q * Q_SCALE).astype(BF16)
    k_ref[0] = k
    kb_ref[0] = k.astype(BF16)
    v_ref[0] = v
    if from_state:
        vb_ref[0] = v.astype(BF16)
    else:
        vb_ref[0, 0] = v.T.astype(BF16)
    qi_ref[0] = qi.astype(BF16)
    ki_ref[0] = ki
    lane128 = lax.broadcasted_iota(I32, ki.shape, 1)
    ki2_ref[0] = jnp.where(lane128 < IDX_DIM, ki, pltpu.roll(ki, IDX_DIM, 1)).astype(BF16)
    wi_ref[0] = wi


def mixers(proj, conv_prev, pool_prev, cos, sin, lw, *, pos0, from_state, emit_cv):
    b, t_len, _ = proj.shape
    tt = min(t_len, 256)
    chunk = min(t_len, CHUNK)
    nt = t_len // tt
    in_specs = [pl.BlockSpec((1, tt, HALF), lambda i, t: (i, t, 0)),
                pl.BlockSpec((1, tt, HALF), lambda i, t: (i, t, 1))]
    args = [proj, proj]
    if from_state:
        in_specs += [pl.BlockSpec((1, CONV_W - 1, W_GROUP), lambda i, t: (i, 0, 0)),
                     pl.BlockSpec((1, POOL_BUF, W_GROUP), lambda i, t: (i, 0, 0))]
        args += [conv_prev, pool_prev]
    else:
        per = tt // PREV_ROWS
        in_specs += [pl.BlockSpec((1, PREV_ROWS, HALF), lambda i, t: (i, jnp.maximum(t * per - 1, 0), 0))]
        args += [proj]
    in_specs += [pl.BlockSpec((tt, LANES), lambda i, t: (t, 0)),
                 pl.BlockSpec((tt, LANES), lambda i, t: (t, 0))]
    args += [cos, sin]
    consts = [lw['conv_w'], lw['pool_w_bd'], lw['pool_scale'], lw['chunk_norm'], lw['chunk_wm'], lw['chunk_bias']]
    for c in consts:
        in_specs.append(pl.BlockSpec(c.shape, lambda i, t, nd=c.ndim: (0,) * nd))
    args += consts

    def tok(width, dtype):
        return (pl.BlockSpec((1, tt, width), lambda i, t: (i, t, 0)), jax.ShapeDtypeStruct((b, t_len, width), dtype))

    outs = [tok(768, BF16),
            tok(256, BF16),
            tok(256, F32), tok(256, BF16),
            tok(256, F32),
            tok(256, BF16) if from_state else
            (pl.BlockSpec((1, 1, W_GROUP, tt), lambda i, t: (i, t, 0, 0)),
             jax.ShapeDtypeStruct((b, nt, W_GROUP, tt), BF16)),
            tok(512, BF16),
            tok(LANES, F32),
            tok(LANES, BF16),
            tok(LANES, F32),
            (pl.BlockSpec((1, CONV_W - 1, W_GROUP), lambda i, t: (i, 0, 0)),
             jax.ShapeDtypeStruct((b, CONV_W - 1, W_GROUP), F32)),
            (pl.BlockSpec((1, POOL_BUF, W_GROUP), lambda i, t: (i, 0, 0)),
             jax.ShapeDtypeStruct((b, POOL_BUF, W_GROUP), F32))]
    if emit_cv:
        outs.append(tok(256, F32))
    return pl.pallas_call(
        functools.partial(_mixer_kernel, tt=tt, chunk=chunk, pos0=pos0, from_state=from_state, emit_cv=emit_cv),
        grid=(b, nt),
        in_specs=in_specs,
        out_specs=[o[0] for o in outs],
        out_shape=[o[1] for o in outs],
        scratch_shapes=[pltpu.VMEM((PREV_ROWS + tt, W_GROUP), F32), pltpu.VMEM((PREV_ROWS + tt, W_GROUP), F32)],
        compiler_params=_cparams(("parallel", "arbitrary")),
        name="mixers",
    )(*args)


def _ordered_key(x):
    x = jnp.where(x == 0.0, 0.0, x)
    bits = pltpu.bitcast(x, I32)
    return bits ^ ((bits >> 31) & jnp.int32(0x7FFFFFFF))


NEG_INF_KEY = -2139095041


def _count_rows(key_ref, n_tiles, pred):
    rows = key_ref.shape[1]
    lane = lax.broadcasted_iota(I32, (rows, LANES), 1)
    acc = jnp.zeros((rows, LANES), F32)
    for j in range(n_tiles):
        acc = acc + jnp.where(pred(key_ref[j], j * LANES + lane), 1.0, 0.0)
    return jnp.sum(acc, axis=1, keepdims=True)


def _count_cols(key_ref, n_groups, gsz, pred):
    cols = key_ref.shape[1]
    pos0 = lax.broadcasted_iota(I32, (gsz, cols), 0)

    def body(c, acc):
        blk = key_ref[pl.ds(pl.multiple_of(c * gsz, gsz), gsz), :]
        hit = jnp.where(pred(blk, c * gsz + pos0), 1.0, 0.0)
        return acc + jnp.sum(hit.reshape(gsz // SUBLANES, SUBLANES, cols), axis=0)

    acc = lax.fori_loop(0, n_groups, body, jnp.zeros((SUBLANES, cols), F32))
    return jnp.sum(acc, axis=0, keepdims=True)


def _count_cols16(ref16, n_groups, gsz, pred):
    cols = ref16.shape[1]

    def body(c, acc):
        blk = ref16[pl.ds(pl.multiple_of(c * gsz, gsz), gsz), :].reshape(gsz // PACK16, PACK16, cols)
        hit = jnp.where(pred(blk), jnp.int16(1), jnp.int16(0))
        for j in range(gsz // PACK16):
            acc = acc + hit[j]
        return acc

    acc = lax.fori_loop(0, n_groups, body, jnp.zeros((PACK16, cols), I16))
    return jnp.sum(acc.astype(F32), axis=0, keepdims=True)


def _pack16(t):
    return jnp.broadcast_to(t, (PACK16, t.shape[1])).astype(I16)[None]


def _kth_largest16(count_ge, k, cols):
    t0 = jnp.where(count_ge(_pack16(jnp.zeros((1, cols), I32))) >= k, 0, I16_MIN).astype(I32)

    def bit(i, t):
        cand = t | (jnp.int32(1) << (14 - i))
        return jnp.where(count_ge(_pack16(cand)) >= k, cand, t)

    return lax.fori_loop(0, 15, bit, t0)


def _kth_largest(count, topk):
    def enough(cand):
        return count(lambda key, pos: key >= cand) >= topk

    thr = jnp.where(enough(jnp.int32(0)), 0, INT_MIN).astype(I32)
    top = thr | jnp.int32(1 << 30)
    thr = jnp.where(enough(top), top, thr)

    def two_bits(i, thr):
        lo = 28 - 2 * i
        c1, c2, c3 = (thr | (jnp.int32(v) << lo) for v in (1, 2, 3))
        return jnp.where(enough(c3), c3, jnp.where(enough(c2), c2, jnp.where(enough(c1), c1, thr)))

    return lax.fori_loop(0, 15, two_bits, thr)


def _tie_bound(count, thr, shape, topk, n_keys):
    need = topk - count(lambda key, pos: key > thr)
    n_eq = count(lambda key, pos: key == thr)
    idx_bits = n_keys.bit_length()
    excess = jnp.max(jnp.where((n_eq > need) & (thr > NEG_INF_KEY), 1.0, 0.0))

    def tie_bound():
        def bound_bit(i, bound):
            cand = bound | (jnp.int32(1) << (idx_bits - 1 - i))
            hits = count(lambda key, pos: (key == thr) & (pos < cand))
            return jnp.where(hits <= need, cand, bound)

        return lax.fori_loop(0, idx_bits, bound_bit, jnp.zeros(shape, I32))

    return lax.cond(excess > 0.0, tie_bound, lambda: jnp.full(shape, 2 ** idx_bits - 1, I32))


def _dsa_prompt_kernel(q_ref, qi_ref, wi_ref, k_ref, vt_ref, ki2_ref, o_ref, key_ref, hi_ref, lo_ref, *acc_refs,
                       tq, kc1, kc, topk, n_keys):
    qb = pl.program_id(1)
    n_chunks = (qb * tq + tq + kc - 1) // kc
    neg_inf = jnp.float32(-jnp.inf)
    q_pos = qb * tq + lax.broadcasted_iota(I32, (kc, tq), 1)
    pos0 = lax.broadcasted_iota(I32, (kc, tq), 0)
    sub = lax.broadcasted_iota(I32, (LANES, tq), 0)

    q_t = q_ref[0].astype(F32).T
    qi_t = qi_ref[0].astype(F32).T
    w_t = wi_ref[0].T

    def head_tile(x_t, h, dim):
        tile = x_t[(h // 2) * LANES:(h // 2 + 1) * LANES]
        return jnp.where((sub // dim) == (h % 2), tile, 0.0).astype(BF16)

    qim = [head_tile(qi_t, h, IDX_DIM) for h in range(IDX_HEADS)]
    qhm = [head_tile(q_t, h, HEAD_DIM) for h in range(ATT_HEADS)]

    q_pos1 = qb * tq + lax.broadcasted_iota(I32, (kc1, tq), 1)
    pos1 = lax.broadcasted_iota(I32, (kc1, tq), 0)

    def score_chunk(c, carry):
        off = pl.multiple_of(c * kc1, kc1)
        ki2 = ki2_ref[0, pl.ds(off, kc1), :]
        acc = jnp.zeros((kc1, tq), F32)
        for h in range(IDX_HEADS):
            acc = acc + w_t[h:h + 1] * jnp.maximum(_dot(ki2, qim[h]), 0.0)
        key = _ordered_key(jnp.where(c * kc1 + pos1 <= q_pos1, acc, neg_inf))
        key_ref[pl.ds(off, kc1), :] = key
        hi_ref[pl.ds(off, kc1), :] = (key >> 16).astype(I16)
        lo_ref[pl.ds(off, kc1), :] = ((key & 0xFFFF) - 2 ** 15).astype(I16)
        return carry

    n_chunks1 = (qb * tq + tq + kc1 - 1) // kc1
    lax.fori_loop(0, n_chunks1, score_chunk, 0)

    cnt_hi = functools.partial(_count_cols16, hi_ref, n_chunks1, kc1)
    t_hi = _kth_largest16(lambda c16: cnt_hi(lambda blk: blk >= c16), float(topk), tq)
    hi16 = _pack16(t_hi)
    need_lo = topk - cnt_hi(lambda blk: blk > hi16)

    def mask_lo(c, carry):
        rows = pl.ds(pl.multiple_of(c * kc1, kc1), kc1)
        hi = hi_ref[rows, :].reshape(kc1 // PACK16, PACK16, tq)
        lo = lo_ref[rows, :].reshape(kc1 // PACK16, PACK16, tq)
        lo_ref[rows, :] = jnp.where(hi == hi16, lo, jnp.int16(I16_MIN)).reshape(kc1, tq)
        return carry

    lax.fori_loop(0, n_chunks1, mask_lo, 0)
    cnt_lo = functools.partial(_count_cols16, lo_ref, n_chunks1, kc1)
    t_lo = _kth_largest16(lambda c16: cnt_lo(lambda blk: blk >= c16), need_lo, tq)
    thr = (t_hi << 16) + (t_lo + 2 ** 15)
    count = functools.partial(_count_cols, key_ref, n_chunks1, kc1)
    bound = _tie_bound(count, thr, (1, tq), topk, n_keys)

    for acc_ref in acc_refs:
        acc_ref[...] = jnp.zeros_like(acc_ref)

    def attend_chunk(c, carry):
        m_run, l_run = carry
        off = pl.multiple_of(c * kc, kc)
        key = key_ref[pl.ds(off, kc), :]
        pos = c * kc + pos0
        sel = ((key > thr) | ((key == thr) & (pos < bound))) & (pos <= q_pos)
        logits = []
        for h in range(ATT_HEADS):
            kh = k_ref[0, pl.ds(off, kc), (h // 2) * LANES:(h // 2 + 1) * LANES]
            logits.append(_dot(kh, qhm[h]))
        m_new, l_new, probs, alphas = [], [], [], []
        for h in range(ATT_HEADS):
            logit = jnp.where(sel, logits[h], neg_inf)
            m_h = jnp.maximum(m_run[h], jnp.max(logit, axis=0, keepdims=True))
            m_safe = jnp.where(m_h == neg_inf, 0.0, m_h)
            p = jnp.exp2(logit - m_safe)
            alpha = jnp.exp2(m_run[h] - m_safe)
            l_new.append(alpha * l_run[h] + jnp.sum(p, axis=0, keepdims=True))
            m_new.append(m_h)
            probs.append(p.astype(BF16))
            alphas.append(alpha)
        for h in range(ATT_HEADS):
            pv = _dot(vt_ref[0, c, h * HEAD_DIM:(h + 1) * HEAD_DIM, :], probs[h])
            acc_refs[h][...] = acc_refs[h][...] * alphas[h] + pv
        return tuple(m_new), tuple(l_new)

    init = (tuple(jnp.full((1, tq), neg_inf, F32) for _ in range(ATT_HEADS)),
            tuple(jnp.zeros((1, tq), F32) for _ in range(ATT_HEADS)))
    _, l_fin = lax.fori_loop(0, n_chunks, attend_chunk, init)
    out_t = jnp.concatenate([acc_refs[h][...] * (1.0 / l_fin[h]) for h in range(ATT_HEADS)], axis=0)
    o_ref[0] = out_t.T.astype(o_ref.dtype)


def dsa_prompt(q, qi, wi, k, vt, ki2):
    b, t_len, _ = q.shape
    kc = vt.shape[-1]
    tq = kc
    kc1 = math.gcd(t_len, 2 * kc)
    topk = min(TOPK_MAX, t_len // 4)
    return pl.pallas_call(
        functools.partial(_dsa_prompt_kernel, tq=tq, kc1=kc1, kc=kc, topk=topk, n_keys=t_len),
        grid=(b, t_len // tq),
        in_specs=[pl.BlockSpec((1, tq, W_GROUP), lambda i, j: (i, j, 0)),
                  pl.BlockSpec((1, tq, 2 * W_GROUP), lambda i, j: (i, j, 0)),
                  pl.BlockSpec((1, tq, LANES), lambda i, j: (i, j, 0)),
                  pl.BlockSpec((1, t_len, W_GROUP), lambda i, j: (i, 0, 0)),
                  pl.BlockSpec((1, t_len // kc, W_GROUP, kc), lambda i, j: (i, 0, 0, 0)),
                  pl.BlockSpec((1, t_len, LANES), lambda i, j: (i, 0, 0))],
        out_specs=pl.BlockSpec((1, tq, W_GROUP), lambda i, j: (i, j, 0)),
        out_shape=jax.ShapeDtypeStruct((b, t_len, W_GROUP), BF16),
        scratch_shapes=[pltpu.VMEM((t_len, tq), I32), pltpu.VMEM((t_len, tq), I16), pltpu.VMEM((t_len, tq), I16)]
        + [pltpu.VMEM((HEAD_DIM, tq), F32)] * ATT_HEADS,
        compiler_params=_cparams(("parallel", "arbitrary")),
        name="dsa_prompt",
    )(q, qi, wi, k, vt, ki2)


def _dsa_sample_kernel(*refs, g, t, n_pages, page, topk):
    pt_ref, q_ref, qi_ref, wi_ref, kn_ref, vn_ref, kin_ref = refs[:7]
    n_pg = g * n_pages
    kt_refs, vt_refs, kit_refs = refs[7:7 + n_pg], refs[7 + n_pg:7 + 2 * n_pg], refs[7 + 2 * n_pg:7 + 3 * n_pg]
    o_ref, key_ref = refs[7 + 3 * n_pg], refs[8 + 3 * n_pg]
    del pt_ref
    past = n_pages * page
    s_len = past + LANES
    n_tiles = s_len // LANES
    lane = lax.broadcasted_iota(I32, (t, LANES), 1)
    lane_o = lax.broadcasted_iota(I32, (t, W_GROUP), 1)
    col = lax.broadcasted_iota(I32, (t, s_len), 1)
    causal = col <= past + lax.broadcasted_iota(I32, (t, s_len), 0)
    neg_inf = jnp.float32(-jnp.inf)

    def pad_rows(x):
        return jnp.concatenate([x.astype(F32), jnp.zeros((LANES - t, x.shape[1]), F32)], axis=0).astype(BF16)

    def pages(page_refs, b):
        return jnp.concatenate([page_refs[b * n_pages + p][0, 0] for p in range(n_pages)], axis=1).astype(BF16)

    for b in range(g):
        qi = qi_ref[b].astype(F32)
        lhs = jnp.concatenate(
            [jnp.where((lane // IDX_DIM) == (h % 2), qi[:, (h // 2) * LANES:(h // 2 + 1) * LANES], 0.0)
             for h in range(IDX_HEADS)], axis=0).astype(BF16)
        kit = pages(kit_refs, b)
        dots = jnp.concatenate([_dot(lhs, jnp.concatenate([kit, kit], axis=0)),
                                _dot_nt(lhs, pad_rows(kin_ref[b]))], axis=1)
        wi = wi_ref[b]
        acc = jnp.zeros((t, s_len), F32)
        for h in range(IDX_HEADS):
            acc = acc + wi[:, h:h + 1] * jnp.maximum(dots[h * t:(h + 1) * t], 0.0)
        key = _ordered_key(jnp.where(causal, acc, neg_inf))
        for j in range(n_tiles):
            key_ref[j, b * t:(b + 1) * t, :] = key[:, j * LANES:(j + 1) * LANES]

    count = functools.partial(_count_rows, key_ref, n_tiles)
    thr = _kth_largest(count, topk)
    bound = _tie_bound(count, thr, (g * t, 1), topk, s_len)

    for b in range(g):
        rows = slice(b * t, (b + 1) * t)
        key = jnp.concatenate([key_ref[j, rows, :] for j in range(n_tiles)], axis=1)
        sel = ((key > thr[rows]) | ((key == thr[rows]) & (col < bound[rows]))) & causal
        q = q_ref[b].astype(F32)
        lhs = jnp.concatenate([jnp.where((lane_o // HEAD_DIM) == h, q, 0.0) for h in range(ATT_HEADS)],
                              axis=0).astype(BF16)
        logit = jnp.concatenate([_dot(lhs, pages(kt_refs, b)), _dot_nt(lhs, pad_rows(kn_ref[b]))], axis=1)
        logit = jnp.where(jnp.concatenate([sel] * ATT_HEADS, axis=0), logit, neg_inf)
        p = jnp.exp2(logit - jnp.max(logit, axis=1, keepdims=True))
        pb = p.astype(BF16)
        res = _dot_nt(pb[:, :past], pages(vt_refs, b)) + _dot(pb[:, past:], pad_rows(vn_ref[b]))
        res = res / jnp.sum(p, axis=1, keepdims=True)
        out = jnp.zeros((t, W_GROUP), F32)
        for h in range(ATT_HEADS):
            out = jnp.where((lane_o // HEAD_DIM) == h, res[h * t:(h + 1) * t], out)
        o_ref[b] = out.astype(o_ref.dtype)


def dsa_sample(q, qi, wi, k_new, v_new, ki2_new, kt_pages, vt_pages, kit_pages, page_table, layer, *, g=4):
    b, t_len, _ = q.shape
    n_pages = page_table.shape[1]
    page = kt_pages.shape[-1]
    g = math.gcd(g, b)
    topk = min(TOPK_MAX, (n_pages * page + t_len) // 4)

    def spec(width):
        return pl.BlockSpec((g, t_len, width), lambda i, pt: (i, 0, 0))

    def page_specs(rows):
        return [pl.BlockSpec((1, 1, rows, page),
                             lambda i, pt, s=s, p=p: (pt[(i * g + s) * n_pages + p], layer, 0, 0))
                for s in range(g) for p in range(n_pages)]

    grid_spec = pltpu.PrefetchScalarGridSpec(
        num_scalar_prefetch=1,
        grid=(b // g,),
        in_specs=[spec(W_GROUP), spec(2 * W_GROUP), spec(LANES), spec(W_GROUP), spec(W_GROUP), spec(LANES)]
        + page_specs(W_GROUP) + page_specs(W_GROUP) + page_specs(IDX_DIM),
        out_specs=spec(W_GROUP),
        scratch_shapes=[pltpu.VMEM((n_pages + 1, g * t_len, LANES), I32)],
    )
    n_pg = g * n_pages
    return pl.pallas_call(
        functools.partial(_dsa_sample_kernel, g=g, t=t_len, n_pages=n_pages, page=page, topk=topk),
        grid_spec=grid_spec,
        out_shape=jax.ShapeDtypeStruct((b, t_len, W_GROUP), BF16),
        compiler_params=_cparams(("parallel",)),
        name="dsa_sample",
    )(page_table.reshape(-1), q, qi, wi, k_new, v_new, ki2_new,
      *([kt_pages] * n_pg), *([vt_pages] * n_pg), *([kit_pages] * n_pg))


def _mem_attn_kernel(q_ref, mk_ref, mv_ref, o_ref, *, g):
    for b in range(g):
        q = q_ref[b]
        outs = []
        for h in range(MEM_HEADS):
            sl = slice(h * MEM_HEAD_DIM, (h + 1) * MEM_HEAD_DIM)
            logit = _dot_nt(q[:, sl], mk_ref[b, 0, :, sl].astype(BF16)) * (MEM_HEAD_DIM ** -0.5)
            m = jnp.max(logit, axis=1, keepdims=True)
            p = jnp.exp(logit - m)
            p = p / jnp.sum(p, axis=1, keepdims=True)
            outs.append(_dot(p.astype(BF16), mv_ref[b, 0, :, sl].astype(BF16)))
        o_ref[b] = jnp.concatenate(outs, axis=-1).astype(o_ref.dtype)


def mem_attention(q, mk, mv, layer, *, tq=512, g=1):
    b, t_len, d = q.shape
    m_len = mk.shape[2]
    if t_len >= tq:
        g = 1
    else:
        tq, g = t_len, math.gcd(g, b)
    mem_spec = pl.BlockSpec((g, 1, m_len, d), lambda i, j: (i, layer, 0, 0))
    return pl.pallas_call(
        functools.partial(_mem_attn_kernel, g=g),
        grid=(b // g, t_len // tq),
        in_specs=[pl.BlockSpec((g, tq, d), lambda i, j: (i, j, 0)), mem_spec, mem_spec],
        out_specs=pl.BlockSpec((g, tq, d), lambda i, j: (i, j, 0)),
        out_shape=jax.ShapeDtypeStruct((b, t_len, d), BF16),
        compiler_params=_cparams(("parallel", "arbitrary")),
        name="mem_attention",
    )(q, mk, mv)


def _prep_layer(l, p):
    w = p['w_in'][l]
    cuts = [0]
    for s in (W_GROUP,) * 9 + (IDX_HEADS * IDX_DIM, IDX_DIM, IDX_HEADS):
        cuts.append(cuts[-1] + s)
    piece = lambda i: w[:, cuts[i]:cuts[i + 1]]
    zeros = lambda n: jnp.zeros((D_MODEL, n), w.dtype)
    w_in = jnp.concatenate(
        [piece(i) for i in range(6)] + [piece(6), piece(7), piece(9), piece(8), piece(10), zeros(LANES - IDX_DIM),
                                        piece(11), zeros(LANES - IDX_HEADS)], axis=1).astype(BF16)
    pool_w = p['pool_w'][l]
    pool_bd = jnp.zeros((W_GROUP, W_GROUP), F32)
    for g in range(len(POOL_WINDOWS)):
        pool_bd = pool_bd.at[g * POOL_CH:(g + 1) * POOL_CH, g * POOL_CH:(g + 1) * POOL_CH].set(pool_w[g])
    return {
        'norm_mix': p['norm_mix'][l], 'w_in': w_in,
        'conv_w': p['conv_w'][l], 'pool_w_bd': pool_bd.astype(BF16),
        'pool_scale': p['pool_scale'][l].reshape(1, W_GROUP), 'chunk_norm': p['chunk_norm'][l].reshape(1, W_GROUP),
        'chunk_ws': p['chunk_ws'][l], 'chunk_b': p['chunk_b'][l],
        'w_out_abc': p['w_out'][l][:3 * W_GROUP].astype(BF16), 'w_out_d': p['w_out'][l][3 * W_GROUP:].astype(BF16),
        'norm_mem': p['norm_mem'][l], 'wq_mem': p['wq_mem'][l].astype(BF16), 'wo_mem': p['wo_mem'][l].astype(BF16),
        'norm_ffn': p['norm_ffn'][l],
    }


def _chunk_consts(lw, chunk):
    mask = jnp.tril(jnp.ones((chunk, chunk), dtype=bool))
    wm = jnp.where(mask[None], lw['chunk_ws'][:, :chunk, :chunk], 0).astype(BF16)
    bias = jnp.repeat(lw['chunk_b'][:, :chunk].T, W_GROUP // CHUNK_HEADS, axis=1)
    return wm, bias


def _rope_tables(pos):
    half = HEAD_DIM // 2
    inv = ROPE_THETA ** (-jnp.arange(half, dtype=F32) / half)
    ang = pos.astype(F32)[:, None] * inv[None, :]
    cos, sin = jnp.cos(ang), jnp.sin(ang)
    cos_t = jnp.concatenate([cos, cos] * (LANES // HEAD_DIM), axis=1)
    sin_t = jnp.concatenate([-sin, sin] * (LANES // HEAD_DIM), axis=1)
    return cos_t, sin_t


def _run_trunk(x, pos0, conv_prev, pool_prev, past, mem_k, mem_v, layers, ffn_w, norm_final):
    b, t_len, d = x.shape
    n = b * t_len
    depth = len(layers)
    from_state = past is not None
    pos = pos0 + jnp.arange(t_len, dtype=I32)
    cos_t, sin_t = _rope_tables(pos)
    h = x.reshape(n, d)
    ks, vs, kis, convs, pools, cvs = [], [], [], [], [], []
    for l, lw in enumerate(layers):
        proj = norm_matmul(h, lw['norm_mix'], lw['w_in']).reshape(b, t_len, P_PAD)
        chunk = min(t_len, CHUNK)
        wm, cbias = _chunk_consts(lw, chunk)
        lw_m = dict(lw, chunk_wm=wm, chunk_bias=cbias)
        outs = mixers(proj, conv_prev[:, l] if from_state else None, pool_prev[:, l] if from_state else None,
                      cos_t, sin_t, lw_m, pos0=pos0, from_state=from_state, emit_cv=from_state)
        mix, q, k, kb, v, vb, qi, ki, ki2, wi, conv_s, pool_s = outs[:12]
        if from_state:
            cvs.append(outs[12])
            yd = dsa_sample(q, qi, wi, kb, vb, ki2, *past, l)
        else:
            yd = dsa_prompt(q, qi, wi, kb, vb, ki2)
        h = matmul_res([mix.reshape(n, 3 * W_GROUP), yd.reshape(n, W_GROUP)], [lw['w_out_abc'], lw['w_out_d']], h)
        qm = norm_matmul(h, lw['norm_mem'], lw['wq_mem'], out_dtype=BF16).reshape(b, t_len, d)
        om = mem_attention(qm, mem_k, mem_v, l)
        h = matmul_res([om.reshape(n, d)], [lw['wo_mem']], h)
        last = l == depth - 1
        fw = ffn_w[l]
        if fw['kind'] == 'dense':
            h = ffn_dense(h, lw['norm_ffn'], fw['w1'], fw['w3'], fw['w2'], norm_final, final_norm=last)
        else:
            h = moe_ffn(h, lw['norm_ffn'], fw['router'], fw['w1'], fw['w3'], fw['w2'], norm_final, final_norm=last)
        ks.append(k.reshape(b, t_len, ATT_HEADS, HEAD_DIM))
        vs.append(v.reshape(b, t_len, ATT_HEADS, HEAD_DIM))
        kis.append(ki[:, :, :IDX_DIM])
        convs.append(conv_s)
        pools.append(pool_s)
    st = lambda xs: jnp.stack(xs, axis=1)
    y = h.reshape(b, t_len, d)
    return y, st(ks), st(vs), st(kis), st(convs), st(pools), (st(cvs) if cvs else None)


def kernel(x_prompt, x_sample, mem_prompt, cache_k, cache_v, cache_kidx, cache_mem_k, cache_mem_v, state_conv, state_pool, page_table, norm_mix, w_in, conv_w, pool_w, pool_scale, chunk_norm, chunk_ws, chunk_b, w_out, norm_mem, wq_mem, wk_mem, wv_mem, wo_mem, norm_ffn, w1_dense, w3_dense, w2_dense, router, w1_moe, w3_moe, w2_moe, norm_final):
    p = {
        'norm_mix': norm_mix, 'w_in': w_in, 'conv_w': conv_w, 'pool_w': pool_w, 'pool_scale': pool_scale,
        'chunk_norm': chunk_norm, 'chunk_ws': chunk_ws, 'chunk_b': chunk_b, 'w_out': w_out, 'norm_mem': norm_mem,
        'wq_mem': wq_mem, 'wo_mem': wo_mem, 'norm_ffn': norm_ffn,
    }
    depth = norm_mix.shape[0]
    layers = [_prep_layer(l, p) for l in range(depth)]
    ffn_w = []
    for l in range(depth):
        j = l // 2
        if l % 2 == 0:
            ffn_w.append({'kind': 'dense', 'w1': w1_dense[j].astype(BF16), 'w3': w3_dense[j].astype(BF16),
                          'w2': w2_dense[j].astype(BF16)})
        else:
            router_pad = jnp.concatenate([router[j], jnp.zeros((D_MODEL, LANES - N_EXPERTS), F32)], axis=1)
            ffn_w.append({'kind': 'moe', 'router': router_pad, 'w1': w1_moe[j].astype(BF16),
                          'w3': w3_moe[j].astype(BF16), 'w2': w2_moe[j].astype(BF16)})

    bp, tp, d = x_prompt.shape
    n_mem = mem_prompt.shape[1]
    w_kv = jnp.concatenate([wk_mem[l] for l in range(depth)] + [wv_mem[l] for l in range(depth)], axis=1).astype(BF16)
    mem_kv = norm_matmul(mem_prompt.reshape(bp * n_mem, d), jnp.ones((d,), F32), w_kv, norm=False)
    mem_kv = mem_kv.reshape(bp, n_mem, 2, depth, d)
    mem_k_p = jnp.moveaxis(mem_kv[:, :, 0], 2, 1)
    mem_v_p = jnp.moveaxis(mem_kv[:, :, 1], 2, 1)
    y_p, k_p, v_p, ki_p, conv_p, pool_p, _ = _run_trunk(
        x_prompt, 0, None, None, None, mem_k_p, mem_v_p, layers, ffn_w, norm_final)

    bs, ts, _ = x_sample.shape
    n_pages, page = page_table.shape[1], cache_k.shape[2]
    past_len = n_pages * page

    n_phys = cache_k.shape[0]
    kt_pages = jnp.transpose(cache_k, (0, 1, 3, 4, 2)).reshape(n_phys, depth, W_GROUP, page)
    vt_pages = jnp.transpose(cache_v, (0, 1, 3, 4, 2)).reshape(n_phys, depth, W_GROUP, page)
    kit_pages = jnp.transpose(cache_kidx, (0, 1, 3, 2))
    past = (kt_pages, vt_pages, kit_pages, page_table)
    y_s, k_s, v_s, ki_s, conv_s, pool_s, cv_s = _run_trunk(
        x_sample, past_len, state_conv, state_pool, past, cache_mem_k.astype(BF16).reshape(bs, depth, n_mem, d),
        cache_mem_v.astype(BF16).reshape(bs, depth, n_mem, d), layers, ffn_w, norm_final)

    mem_shape = (bp, depth, n_mem, MEM_HEADS, MEM_HEAD_DIM)
    return (y_p, y_s, k_p, v_p, ki_p, mem_k_p.reshape(mem_shape), mem_v_p.reshape(mem_shape), conv_p, pool_p,
            k_s, v_s, ki_s, conv_s, pool_s, cv_s)
```

```python
import functools
import math

import jax
import jax.numpy as jnp
from jax import lax
from jax.experimental import pallas as pl
from jax.experimental.pallas import tpu as pltpu

F32 = jnp.float32
BF16 = jnp.bfloat16
I32 = jnp.int32

D_MODEL = 1024
W_GROUP = 256
CONV_W = 3
POOL_WINDOWS = (2, 4, 8, 16)
POOL_CH = 64
POOL_BUF = 15
CHUNK = 128
CHUNK_HEADS = 4
ATT_HEADS = 4
HEAD_DIM = 64
IDX_HEADS = 8
IDX_DIM = 64
IDX_W_SCALE = (IDX_HEADS ** -0.5) * (IDX_DIM ** -0.5)
TOPK_MAX = 256
ROPE_THETA = 10000.0
MEM_HEADS = 4
MEM_HEAD_DIM = 256
D_FF = 3584
N_EXPERTS = 8
EPS = 1e-6

LANES = 128
SUBLANES = 8
PREV_ROWS = 16
P_PAD = 3072
HALF = P_PAD // 2
OFF_Q, OFF_K, OFF_QI, OFF_V, OFF_KI, OFF_WI = 1536, 1792, 2048, 2560, 2816, 2944
VMEM_LIMIT = 56 * 1024 * 1024

INT_MIN = -2 ** 31
Q_SCALE = (HEAD_DIM ** -0.5) * math.log2(math.e)


def _cparams(sem):
    return pltpu.CompilerParams(dimension_semantics=sem, vmem_limit_bytes=VMEM_LIMIT)


def _rms(x, g):
    ms = jnp.mean(x * x, axis=-1, keepdims=True)
    return x * lax.rsqrt(ms + EPS) * g


def _dot(a, b):
    return jnp.dot(a, b, preferred_element_type=F32)


def _dot_nt(a, b):
    return lax.dot_general(a, b, (((1,), (1,)), ((), ())), preferred_element_type=F32)


def _norm_matmul_kernel(x_ref, g_ref, w_ref, o_ref, xn_ref, *, norm):
    @pl.when(pl.program_id(1) == 0)
    def _():
        x = x_ref[...]
        if norm:
            x = _rms(x, g_ref[...])
        xn_ref[...] = x.astype(BF16)

    o_ref[...] = _dot(xn_ref[...], w_ref[...]).astype(o_ref.dtype)


def norm_matmul(x, g, w, *, norm=True, out_dtype=F32, tm=1024, tn=1536):
    m, k = x.shape
    n = w.shape[1]
    tm = min(tm, m)
    tn = tn if n % tn == 0 else math.gcd(n, 1024)
    return pl.pallas_call(
        functools.partial(_norm_matmul_kernel, norm=norm),
        grid=(m // tm, n // tn),
        in_specs=[pl.BlockSpec((tm, k), lambda i, j: (i, 0)),
                  pl.BlockSpec((1, k), lambda i, j: (0, 0)),
                  pl.BlockSpec((k, tn), lambda i, j: (0, j))],
        out_specs=pl.BlockSpec((tm, tn), lambda i, j: (i, j)),
        out_shape=jax.ShapeDtypeStruct((m, n), out_dtype),
        scratch_shapes=[pltpu.VMEM((tm, k), BF16)],
        compiler_params=_cparams(("parallel", "arbitrary")),
        name="norm_matmul",
    )(x, g.reshape(1, k), w)


def _matmul_res_kernel(*refs, n_in):
    a_refs, w_refs = refs[:n_in], refs[n_in:2 * n_in]
    r_ref, o_ref = refs[2 * n_in], refs[2 * n_in + 1]
    acc = r_ref[...]
    for a, w in zip(a_refs, w_refs):
        acc = acc + _dot(a[...].astype(BF16), w[...])
    o_ref[...] = acc


def matmul_res(a_list, w_list, res, *, tm=1024):
    m, n = res.shape
    tm = min(tm, m)
    n_in = len(a_list)
    in_specs = [pl.BlockSpec((tm, a.shape[1]), lambda i: (i, 0)) for a in a_list]
    in_specs += [pl.BlockSpec(w.shape, lambda i: (0, 0)) for w in w_list]
    in_specs += [pl.BlockSpec((tm, n), lambda i: (i, 0))]
    return pl.pallas_call(
        functools.partial(_matmul_res_kernel, n_in=n_in),
        grid=(m // tm,),
        in_specs=in_specs,
        out_specs=pl.BlockSpec((tm, n), lambda i: (i, 0)),
        out_shape=jax.ShapeDtypeStruct((m, n), F32),
        compiler_params=_cparams(("parallel",)),
        name="matmul_res",
    )(*a_list, *w_list, res)


def _ffn_kernel(x_ref, g_ref, w1_ref, w3_ref, w2_ref, gf_ref, o_ref, xn_ref, acc_ref, *, final_norm):
    f = pl.program_id(1)

    @pl.when(f == 0)
    def _():
        xn_ref[...] = _rms(x_ref[...], g_ref[...]).astype(BF16)
        acc_ref[...] = jnp.zeros_like(acc_ref)

    xn = xn_ref[...]
    a = _dot(xn, w1_ref[...])
    b = _dot(xn, w3_ref[...])
    hidden = (a * jax.nn.sigmoid(a) * b).astype(BF16)
    acc_ref[...] += _dot(hidden, w2_ref[...])

    @pl.when(f == pl.num_programs(1) - 1)
    def _():
        y = x_ref[...] + acc_ref[...]
        if final_norm:
            y = _rms(y, gf_ref[...])
        o_ref[...] = y


def ffn_dense(x, g, w1, w3, w2, g_final, *, final_norm, tm=1024, tf=896):
    m, d = x.shape
    ff = w1.shape[1]
    tm = min(tm, m)
    return pl.pallas_call(
        functools.partial(_ffn_kernel, final_norm=final_norm),
        grid=(m // tm, ff // tf),
        in_specs=[pl.BlockSpec((tm, d), lambda i, f: (i, 0)),
                  pl.BlockSpec((1, d), lambda i, f: (0, 0)),
                  pl.BlockSpec((d, tf), lambda i, f: (0, f)),
                  pl.BlockSpec((d, tf), lambda i, f: (0, f)),
                  pl.BlockSpec((tf, d), lambda i, f: (f, 0)),
                  pl.BlockSpec((1, d), lambda i, f: (0, 0))],
        out_specs=pl.BlockSpec((tm, d), lambda i, f: (i, 0)),
        out_shape=jax.ShapeDtypeStruct((m, d), F32),
        scratch_shapes=[pltpu.VMEM((tm, d), BF16), pltpu.VMEM((tm, d), F32)],
        compiler_params=_cparams(("parallel", "arbitrary")),
        name="ffn_dense",
    )(x, g.reshape(1, d), w1, w3, w2, g_final.reshape(1, d))


def _router_kernel(x_ref, g_ref, r_ref, hn_ref, e_ref, gate_ref):
    hn = _rms(x_ref[...], g_ref[...])
    hn_ref[...] = hn.astype(BF16)
    logits = jnp.dot(hn, r_ref[...], preferred_element_type=F32, precision=lax.Precision.HIGHEST)
    lane = lax.broadcasted_iota(I32, logits.shape, 1)
    neg = jnp.float32(-jnp.inf)
    l1 = jnp.where(lane < N_EXPERTS, logits, neg)
    m1 = jnp.max(l1, axis=1, keepdims=True)
    i1 = jnp.min(jnp.where(l1 == m1, lane, LANES), axis=1, keepdims=True)
    l2 = jnp.where(lane == i1, neg, l1)
    m2 = jnp.max(l2, axis=1, keepdims=True)
    i2 = jnp.min(jnp.where(l2 == m2, lane, LANES), axis=1, keepdims=True)
    t = jnp.exp(m2 - m1)
    denom = 1.0 + t
    e_ref[...] = jnp.where(lane == 0, i1, jnp.where(lane == 1, i2, 0))
    gate_ref[...] = jnp.where(lane == 0, 1.0 / denom, jnp.where(lane == 1, t / denom, 0.0))


def moe_router(x, g, router_pad, *, tm=512):
    m, d = x.shape
    tm = min(tm, m)
    return pl.pallas_call(
        _router_kernel,
        grid=(m // tm,),
        in_specs=[pl.BlockSpec((tm, d), lambda i: (i, 0)),
                  pl.BlockSpec((1, d), lambda i: (0, 0)),
                  pl.BlockSpec((d, LANES), lambda i: (0, 0))],
        out_specs=[pl.BlockSpec((tm, d), lambda i: (i, 0)),
                   pl.BlockSpec((tm, LANES), lambda i: (i, 0)),
                   pl.BlockSpec((tm, LANES), lambda i: (i, 0))],
        out_shape=[jax.ShapeDtypeStruct((m, d), BF16),
                   jax.ShapeDtypeStruct((m, LANES), I32),
                   jax.ShapeDtypeStruct((m, LANES), F32)],
        compiler_params=_cparams(("parallel",)),
        name="moe_router",
    )(x, g.reshape(1, d), router_pad)


def _expert_kernel(be_ref, nb_ref, xs_ref, gate_ref, w1_ref, w3_ref, w2_ref, o_ref, acc_ref):
    j, f = pl.program_id(0), pl.program_id(1)
    live = j < nb_ref[0]

    @pl.when(f == 0)
    def _():
        acc_ref[...] = jnp.zeros_like(acc_ref)

    @pl.when(live)
    def _():
        xs = xs_ref[...]
        a = _dot(xs, w1_ref[0])
        b = _dot(xs, w3_ref[0])
        hidden = (a * jax.nn.sigmoid(a) * b).astype(BF16)
        acc_ref[...] += _dot(hidden, w2_ref[0])

    @pl.when(f == pl.num_programs(1) - 1)
    def _():
        o_ref[...] = acc_ref[...] * gate_ref[...]


def moe_experts(block_e, n_live, xs, slot_gate, w1, w3, w2, *, bm, tf=896):
    n_slots, d = xs.shape
    ff = w1.shape[2]
    n_blocks = n_slots // bm
    grid_spec = pltpu.PrefetchScalarGridSpec(
        num_scalar_prefetch=2,
        grid=(n_blocks, ff // tf),
        in_specs=[pl.BlockSpec((bm, d), lambda j, f, be, nb: (j, 0)),
                  pl.BlockSpec((bm, 1), lambda j, f, be, nb: (j, 0)),
                  pl.BlockSpec((1, d, tf), lambda j, f, be, nb: (be[j], 0, f)),
                  pl.BlockSpec((1, d, tf), lambda j, f, be, nb: (be[j], 0, f)),
                  pl.BlockSpec((1, tf, d), lambda j, f, be, nb: (be[j], f, 0))],
        out_specs=pl.BlockSpec((bm, d), lambda j, f, be, nb: (j, 0)),
        scratch_shapes=[pltpu.VMEM((bm, d), F32)],
    )
    return pl.pallas_call(
        _expert_kernel,
        grid_spec=grid_spec,
        out_shape=jax.ShapeDtypeStruct((n_slots, d), F32),
        compiler_params=_cparams(("parallel", "arbitrary")),
        name="moe_experts",
    )(block_e, n_live, xs, slot_gate, w1, w3, w2)


def _combine_kernel(x_ref, a_ref, b_ref, gf_ref, o_ref, *, final_norm):
    y = x_ref[...] + (a_ref[...] + b_ref[...])
    if final_norm:
        y = _rms(y, gf_ref[...])
    o_ref[...] = y


def moe_combine(x, ya, yb, g_final, *, final_norm, tm=512):
    m, d = x.shape
    tm = min(tm, m)
    spec = pl.BlockSpec((tm, d), lambda i: (i, 0))
    return pl.pallas_call(
        functools.partial(_combine_kernel, final_norm=final_norm),
        grid=(m // tm,),
        in_specs=[spec, spec, spec, pl.BlockSpec((1, d), lambda i: (0, 0))],
        out_specs=spec,
        out_shape=jax.ShapeDtypeStruct((m, d), F32),
        compiler_params=_cparams(("parallel",)),
        name="moe_combine",
    )(x, ya, yb, g_final.reshape(1, d))


def _compact_kernel(ib_ref, ic_ref, first_ref, total_ref, tok_ref, hn_ref, o_ref, acc_ref, *, tc):
    i = pl.program_id(0)

    @pl.when(first_ref[i] == 1)
    def _():
        acc_ref[...] = jnp.zeros_like(acc_ref)

    @pl.when(i < total_ref[0])
    def _():
        local = tok_ref[...] - ic_ref[i] * tc
        lane = lax.broadcasted_iota(I32, (tok_ref.shape[0], tc), 1)
        onehot = jnp.where(local == lane, 1.0, 0.0).astype(BF16)
        acc_ref[...] += _dot(onehot, hn_ref[...])

    o_ref[...] = acc_ref[...].astype(o_ref.dtype)


def moe_compact(item_block, item_chunk, item_first, n_items, slot_tok, hn, *, bm, tc):
    n_slots = slot_tok.shape[0]
    n, d = hn.shape
    n_max = item_block.shape[0]
    grid_spec = pltpu.PrefetchScalarGridSpec(
        num_scalar_prefetch=4,
        grid=(n_max,),
        in_specs=[pl.BlockSpec((bm, 1), lambda i, ib, ic, fr, tot: (ib[i], 0)),
                  pl.BlockSpec((tc, d), lambda i, ib, ic, fr, tot: (ic[i], 0))],
        out_specs=pl.BlockSpec((bm, d), lambda i, ib, ic, fr, tot: (ib[i], 0)),
        scratch_shapes=[pltpu.VMEM((bm, d), F32)],
    )
    return pl.pallas_call(
        functools.partial(_compact_kernel, tc=tc),
        grid_spec=grid_spec,
        out_shape=jax.ShapeDtypeStruct((n_slots, d), BF16),
        compiler_params=_cparams(("arbitrary",)),
        name="moe_compact",
    )(item_block, item_chunk, item_first, n_items, slot_tok.reshape(n_slots, 1), hn)


def moe_ffn(x, g, router_pad, w1, w3, w2, g_final, *, final_norm):
    n, d = x.shape
    bm = 512 if n >= 4096 else 128
    tc = min(n, 512)
    hn, e12, g12 = moe_router(x, g, router_pad)
    e_flat = e12[:, :2].reshape(-1)
    g_flat = g12[:, :2].reshape(-1)
    n_assign = 2 * n
    order = jnp.argsort(e_flat, stable=True).astype(I32)
    rank = jnp.argsort(order).astype(I32)
    counts = jnp.sum(e_flat[:, None] == jnp.arange(N_EXPERTS, dtype=I32)[None, :], axis=0).astype(I32)
    padded = (counts + bm - 1) // bm * bm
    start = jnp.cumsum(counts) - counts
    pend = jnp.cumsum(padded)
    pstart = pend - padded
    n_blocks = -(-n_assign // bm) + N_EXPERTS
    n_slots = n_blocks * bm
    block_start = jnp.arange(n_blocks, dtype=I32) * bm
    block_e = jnp.minimum(jnp.sum(pend[None, :] <= block_start[:, None], axis=1), N_EXPERTS - 1).astype(I32)
    n_live = (pend[-1] // bm).astype(I32).reshape(1)
    slot_e = jnp.repeat(block_e, bm)
    slot_idx = jnp.arange(n_slots, dtype=I32) - pstart[slot_e]
    slot_live = slot_idx < counts[slot_e]
    slot_src = order[jnp.clip(start[slot_e] + slot_idx, 0, n_assign - 1)]
    slot_tok = jnp.where(slot_live, slot_src // 2, -1)
    slot_gate = jnp.where(slot_live, g_flat[slot_src], 0.0)
    assign_slot = pstart[e_flat] + rank - start[e_flat]
    tok_b = slot_tok.reshape(n_blocks, bm)
    c_lo = jnp.maximum(tok_b[:, 0], 0) // tc
    c_hi = jnp.maximum(jnp.max(tok_b, axis=1), 0) // tc
    n_items_b = c_hi - c_lo + 1
    item_end = jnp.cumsum(n_items_b)
    total = item_end[-1]
    n_items_max = n_blocks + N_EXPERTS * (n // tc)
    item_id = jnp.arange(n_items_max, dtype=I32)
    item = jnp.minimum(item_id, total - 1)
    item_block = jnp.sum(item_end[None, :] <= item[:, None], axis=1).astype(I32)
    item_off = item - (item_end - n_items_b)[item_block]
    item_chunk = (c_lo[item_block] + item_off).astype(I32)
    item_first = ((item_off == 0) & (item_id < total)).astype(I32)
    xs = moe_compact(item_block, item_chunk, item_first, total.astype(I32).reshape(1), slot_tok, hn, bm=bm, tc=tc)
    ys = moe_experts(block_e, n_live, xs, slot_gate.reshape(n_slots, 1), w1, w3, w2, bm=bm)
    pair = assign_slot.reshape(n, 2)
    return moe_combine(x, ys[pair[:, 0]], ys[pair[:, 1]], g_final, final_norm=final_norm)


def _swap_half_heads(x):
    n = x.shape[-1]
    lane = lax.broadcasted_iota(I32, x.shape, x.ndim - 1)
    fwd = pltpu.roll(x, n - HEAD_DIM // 2, x.ndim - 1)
    bwd = pltpu.roll(x, HEAD_DIM // 2, x.ndim - 1)
    return jnp.where(lane % HEAD_DIM < HEAD_DIM // 2, fwd, bwd)


def _rope(x, cos, sin):
    reps = x.shape[-1] // LANES
    if reps > 1:
        cos = jnp.concatenate([cos] * reps, axis=-1)
        sin = jnp.concatenate([sin] * reps, axis=-1)
    return x * cos + _swap_half_heads(x) * sin


def _mixer_kernel(*refs, tt, chunk, pos0, from_state, emit_cv):
    it = iter(refs)
    xa_ref, xb_ref = next(it), next(it)
    if from_state:
        convp_ref, poolp_ref = next(it), next(it)
    else:
        prev_ref = next(it)
    cos_ref, sin_ref = next(it), next(it)
    convw_ref, poolw_ref, pscale_ref, cnorm_ref, wm_ref, cbias_ref = (next(it) for _ in range(6))
    mix_ref, q_ref, k_ref, kb_ref, v_ref, vb_ref, qi_ref, ki_ref, ki2_ref, wi_ref, convs_ref, pools_ref = (
        next(it) for _ in range(12))
    cv_ref = next(it) if emit_cv else None
    ybuf, xbuf = next(it), next(it)

    t = pl.program_id(1)
    xa = xa_ref[0]
    a_in, a_b, a_c = xa[:, 0:256], xa[:, 256:512], xa[:, 512:768]
    b_in, c_u, c_v = xa[:, 768:1024], xa[:, 1024:1280], xa[:, 1280:1536]

    y = a_c * a_in
    if from_state:
        ybuf[PREV_ROWS - 2:PREV_ROWS, :] = convp_ref[0]
        xbuf[0:1, :] = jnp.zeros((1, W_GROUP), F32)
        xbuf[1:PREV_ROWS, :] = poolp_ref[0]
    else:
        first = (t == 0).astype(F32)
        prev = prev_ref[0] * (1.0 - first)
        ybuf[0:PREV_ROWS, :] = prev[:, 512:768] * prev[:, 0:256]
        xbuf[0:PREV_ROWS, :] = prev[:, 768:1024]
    ybuf[PREV_ROWS:PREV_ROWS + tt, :] = y
    xbuf[PREV_ROWS:PREV_ROWS + tt, :] = b_in

    cw = convw_ref[...]
    z = cw[0:1, :] * ybuf[PREV_ROWS - 2:PREV_ROWS - 2 + tt, :]
    z = z + cw[1:2, :] * ybuf[PREV_ROWS - 1:PREV_ROWS - 1 + tt, :]
    z = z + cw[2:3, :] * y
    ya = a_b * z
    convs_ref[0] = ybuf[PREV_ROWS + tt - 2:PREV_ROWS + tt, :]

    lane = lax.broadcasted_iota(I32, (tt, W_GROUP), 1)
    row = lax.broadcasted_iota(I32, (tt, W_GROUP), 0)
    pos = (pos0 + t * tt + row).astype(F32)
    run = b_in
    mean = jnp.zeros((tt, W_GROUP), F32)
    for j in range(1, POOL_WINDOWS[-1] + 1):
        if j > 1:
            run = run + xbuf[PREV_ROWS - (j - 1):PREV_ROWS - (j - 1) + tt, :]
        if j in POOL_WINDOWS:
            grp = POOL_WINDOWS.index(j)
            cnt = jnp.minimum(pos + 1.0, float(j))
            mean = jnp.where(lane // POOL_CH == grp, run / cnt, mean)
    dlt = (mean - b_in).astype(BF16)
    yb = _dot(dlt, poolw_ref[...]) * pscale_ref[...]
    pools_ref[0] = xbuf[PREV_ROWS + tt - POOL_BUF:PREV_ROWS + tt, :]

    vn = _rms(c_v, cnorm_ref[...])
    if emit_cv:
        cv_ref[0] = vn
    vnb = vn.astype(BF16)
    lane_c = lax.broadcasted_iota(I32, (chunk, W_GROUP), 1)
    parts = []
    for c in range(tt // chunk):
        vc = vnb[c * chunk:(c + 1) * chunk, :]
        s = cbias_ref[...]
        for hd in range(CHUNK_HEADS):
            sh = _dot(wm_ref[hd], vc)
            s = s + jnp.where(lane_c // (W_GROUP // CHUNK_HEADS) == hd, sh, 0.0)
        parts.append(s)
    s_all = parts[0] if len(parts) == 1 else jnp.concatenate(parts, axis=0)
    yc = c_u * s_all

    mix_ref[0] = jnp.concatenate([ya, yb, yc], axis=-1).astype(BF16)

    xb = xb_ref[0]
    cos, sin = cos_ref[...], sin_ref[...]
    q = _rope(xb[:, 0:256], cos, sin)
    k = _rope(xb[:, 256:512], cos, sin)
    qi = _rope(xb[:, 512:1024], cos, sin)
    v = xb[:, 1024:1280]
    ki = _rope(xb[:, 1280:1408], cos, sin)
    wi = xb[:, 1408:1536] * IDX_W_SCALE
    q_ref[0] = (q * Q_SCALE).astype(BF16)
    k_ref[0] = k
    kb_ref[0] = k.astype(BF16)
    v_ref[0] = v
    if from_state:
        vb_ref[0] = v.astype(BF16)
    else:
        vb_ref[0, 0] = v.T.astype(BF16)
    qi_ref[0] = qi.astype(BF16)
    ki_ref[0] = ki
    lane128 = lax.broadcasted_iota(I32, ki.shape, 1)
    ki2_ref[0] = jnp.where(lane128 < IDX_DIM, ki, pltpu.roll(ki, IDX_DIM, 1)).astype(BF16)
    wi_ref[0] = wi


def mixers(proj, conv_prev, pool_prev, cos, sin, lw, *, pos0, from_state, emit_cv):
    b, t_len, _ = proj.shape
    tt = min(t_len, 256)
    chunk = min(t_len, CHUNK)
    nt = t_len // tt
    in_specs = [pl.BlockSpec((1, tt, HALF), lambda i, t: (i, t, 0)),
                pl.BlockSpec((1, tt, HALF), lambda i, t: (i, t, 1))]
    args = [proj, proj]
    if from_state:
        in_specs += [pl.BlockSpec((1, CONV_W - 1, W_GROUP), lambda i, t: (i, 0, 0)),
                     pl.BlockSpec((1, POOL_BUF, W_GROUP), lambda i, t: (i, 0, 0))]
        args += [conv_prev, pool_prev]
    else:
        per = tt // PREV_ROWS
        in_specs += [pl.BlockSpec((1, PREV_ROWS, HALF), lambda i, t: (i, jnp.maximum(t * per - 1, 0), 0))]
        args += [proj]
    in_specs += [pl.BlockSpec((tt, LANES), lambda i, t: (t, 0)),
                 pl.BlockSpec((tt, LANES), lambda i, t: (t, 0))]
    args += [cos, sin]
    consts = [lw['conv_w'], lw['pool_w_bd'], lw['pool_scale'], lw['chunk_norm'], lw['chunk_wm'], lw['chunk_bias']]
    for c in consts:
        in_specs.append(pl.BlockSpec(c.shape, lambda i, t, nd=c.ndim: (0,) * nd))
    args += consts

    def tok(width, dtype):
        return (pl.BlockSpec((1, tt, width), lambda i, t: (i, t, 0)), jax.ShapeDtypeStruct((b, t_len, width), dtype))

    outs = [tok(768, BF16),
            tok(256, BF16),
            tok(256, F32), tok(256, BF16),
            tok(256, F32),
            tok(256, BF16) if from_state else
            (pl.BlockSpec((1, 1, W_GROUP, tt), lambda i, t: (i, t, 0, 0)),
             jax.ShapeDtypeStruct((b, nt, W_GROUP, tt), BF16)),
            tok(512, BF16),
            tok(LANES, F32),
            tok(LANES, BF16),
            tok(LANES, F32),
            (pl.BlockSpec((1, CONV_W - 1, W_GROUP), lambda i, t: (i, 0, 0)),
             jax.ShapeDtypeStruct((b, CONV_W - 1, W_GROUP), F32)),
            (pl.BlockSpec((1, POOL_BUF, W_GROUP), lambda i, t: (i, 0, 0)),
             jax.ShapeDtypeStruct((b, POOL_BUF, W_GROUP), F32))]
    if emit_cv:
        outs.append(tok(256, F32))
    return pl.pallas_call(
        functools.partial(_mixer_kernel, tt=tt, chunk=chunk, pos0=pos0, from_state=from_state, emit_cv=emit_cv),
        grid=(b, nt),
        in_specs=in_specs,
        out_specs=[o[0] for o in outs],
        out_shape=[o[1] for o in outs],
        scratch_shapes=[pltpu.VMEM((PREV_ROWS + tt, W_GROUP), F32), pltpu.VMEM((PREV_ROWS + tt, W_GROUP), F32)],
        compiler_params=_cparams(("parallel", "arbitrary")),
        name="mixers",
    )(*args)


def _ordered_f32(key):
    return pltpu.bitcast(key ^ ((key >> 31) & jnp.int32(0x7FFFFFFF)), F32)


def _count_rows(score_ref, n_tiles, pred):
    rows = score_ref.shape[1]
    lane = lax.broadcasted_iota(I32, (rows, LANES), 1)
    acc = jnp.zeros((rows, LANES), F32)
    for j in range(n_tiles):
        acc = acc + jnp.where(pred(score_ref[j], j * LANES + lane), 1.0, 0.0)
    return jnp.sum(acc, axis=1, keepdims=True)


def _count_cols(score_ref, n_groups, gsz, pred):
    cols = score_ref.shape[1]
    pos0 = lax.broadcasted_iota(I32, (gsz, cols), 0)

    def body(c, acc):
        blk = score_ref[pl.ds(pl.multiple_of(c * gsz, gsz), gsz), :]
        hit = jnp.where(pred(blk, c * gsz + pos0), 1.0, 0.0)
        return acc + jnp.sum(hit.reshape(gsz // SUBLANES, SUBLANES, cols), axis=0)

    acc = lax.fori_loop(0, n_groups, body, jnp.zeros((SUBLANES, cols), F32))
    return jnp.sum(acc, axis=0, keepdims=True)


def _kth_largest(count, topk, two_bits_per_trip):
    def enough(key):
        cand = _ordered_f32(key)
        return count(lambda s, pos: s >= cand) >= topk

    key = jnp.where(count(lambda s, pos: s >= 0.0) >= topk, 0, INT_MIN).astype(I32)
    if two_bits_per_trip:
        top = key | jnp.int32(1 << 30)
        key = jnp.where(enough(top), top, key)

        def two_bits(i, key):
            lo = 28 - 2 * i
            c1, c2, c3 = (key | (jnp.int32(v) << lo) for v in (1, 2, 3))
            return jnp.where(enough(c3), c3, jnp.where(enough(c2), c2, jnp.where(enough(c1), c1, key)))

        key = lax.fori_loop(0, 15, two_bits, key)
    else:
        def one_bit(i, key):
            cand = key | (jnp.int32(1) << (30 - i))
            return jnp.where(enough(cand), cand, key)

        key = lax.fori_loop(0, 31, one_bit, key)
    n_valid = count(lambda s, pos: s > -jnp.inf)
    return jnp.where(n_valid < topk, -jnp.inf, _ordered_f32(key))


def _tie_bound(count, thr, shape, topk, n_keys):
    need = topk - count(lambda s, pos: s > thr)
    n_eq = count(lambda s, pos: s == thr)
    idx_bits = n_keys.bit_length()
    excess = jnp.max(jnp.where((n_eq > need) & (thr > -jnp.inf), 1.0, 0.0))

    def tie_bound():
        def bound_bit(i, bound):
            cand = bound | (jnp.int32(1) << (idx_bits - 1 - i))
            hits = count(lambda s, pos: (s == thr) & (pos < cand))
            return jnp.where(hits <= need, cand, bound)

        return lax.fori_loop(0, idx_bits, bound_bit, jnp.zeros(shape, I32))

    return lax.cond(excess > 0.0, tie_bound, lambda: jnp.full(shape, 2 ** idx_bits - 1, I32))


def _dsa_prompt_kernel(q_ref, qi_ref, wi_ref, k_ref, vt_ref, ki2_ref, o_ref, score_ref, *scratch,
                       tq, kc1, kc, topk, n_keys):
    acc_refs, logit_refs = scratch[:ATT_HEADS], scratch[ATT_HEADS:]
    qb = pl.program_id(1)
    n_chunks = (qb * tq + tq + kc - 1) // kc
    neg_inf = jnp.float32(-jnp.inf)
    q_pos = qb * tq + lax.broadcasted_iota(I32, (kc, tq), 1)
    pos0 = lax.broadcasted_iota(I32, (kc, tq), 0)
    sub = lax.broadcasted_iota(I32, (LANES, tq), 0)

    q_t = q_ref[0].astype(F32).T
    qi_t = qi_ref[0].astype(F32).T
    w_t = wi_ref[0].T

    def head_tile(x_t, h, dim):
        tile = x_t[(h // 2) * LANES:(h // 2 + 1) * LANES]
        return jnp.where((sub // dim) == (h % 2), tile, 0.0).astype(BF16)

    qim = [head_tile(qi_t, h, IDX_DIM) for h in range(IDX_HEADS)]
    qhm = [head_tile(q_t, h, HEAD_DIM) for h in range(ATT_HEADS)]

    q_pos1 = qb * tq + lax.broadcasted_iota(I32, (kc1, tq), 1)
    pos1 = lax.broadcasted_iota(I32, (kc1, tq), 0)

    def score_chunk(c, carry):
        off = pl.multiple_of(c * kc1, kc1)
        ki2 = ki2_ref[0, pl.ds(off, kc1), :]
        acc = jnp.zeros((kc1, tq), F32)
        for h in range(IDX_HEADS):
            acc = acc + w_t[h:h + 1] * jnp.maximum(_dot(ki2, qim[h]), 0.0)
        score_ref[pl.ds(off, kc1), :] = jnp.where(c * kc1 + pos1 <= q_pos1, acc, neg_inf)
        return carry

    lax.fori_loop(0, (qb * tq + tq + kc1 - 1) // kc1, score_chunk, 0)

    count = functools.partial(_count_cols, score_ref, n_chunks, kc)
    thr = _kth_largest(count, topk, two_bits_per_trip=False)
    bound = _tie_bound(count, thr, (1, tq), topk, n_keys)

    for acc_ref in acc_refs:
        acc_ref[...] = jnp.zeros_like(acc_ref)

    logit_a, logit_b = logit_refs[:ATT_HEADS], logit_refs[ATT_HEADS:]
    last = n_chunks - 1

    def score_matmuls(c, dst):
        off = pl.multiple_of(c * kc, kc)
        for h in range(ATT_HEADS):
            dst[h][...] = _dot(k_ref[0, pl.ds(off, kc), (h // 2) * LANES:(h // 2 + 1) * LANES], qhm[h])

    def softmax_pv(c, live, src, m_run, l_run):
        pos = c * kc + jnp.where(live, 0, n_keys) + pos0
        score = score_ref[pl.ds(pl.multiple_of(c * kc, kc), kc), :]
        sel = ((score > thr) | ((score == thr) & (pos < bound))) & (pos <= q_pos)
        m_new, l_new, probs, alphas = [], [], [], []
        for h in range(ATT_HEADS):
            logit = jnp.where(sel, src[h][...], neg_inf)
            m_h = jnp.maximum(m_run[h], jnp.max(logit, axis=0, keepdims=True))
            m_safe = jnp.where(m_h == neg_inf, 0.0, m_h)
            p = jnp.exp2(logit - m_safe)
            alpha = jnp.exp2(m_run[h] - m_safe)
            l_new.append(alpha * l_run[h] + jnp.sum(p, axis=0, keepdims=True))
            m_new.append(m_h)
            probs.append(p.astype(BF16))
            alphas.append(alpha)
        for h in range(ATT_HEADS):
            pv = _dot(vt_ref[0, c, h * HEAD_DIM:(h + 1) * HEAD_DIM, :], probs[h])
            acc_refs[h][...] = acc_refs[h][...] * alphas[h] + pv
        return tuple(m_new), tuple(l_new)

    def attend_pair(i, carry):
        m_run, l_run = carry
        c0 = 2 * i
        c1 = jnp.minimum(c0 + 1, last)
        score_matmuls(c1, logit_b)
        m_run, l_run = softmax_pv(c0, True, logit_a, m_run, l_run)
        score_matmuls(jnp.minimum(c0 + 2, last), logit_a)
        return softmax_pv(c1, c0 + 1 <= last, logit_b, m_run, l_run)

    init = (tuple(jnp.full((1, tq), neg_inf, F32) for _ in range(ATT_HEADS)),
            tuple(jnp.zeros((1, tq), F32) for _ in range(ATT_HEADS)))
    score_matmuls(0, logit_a)
    _, l_fin = lax.fori_loop(0, (n_chunks + 1) // 2, attend_pair, init)
    out_t = jnp.concatenate([acc_refs[h][...] * (1.0 / l_fin[h]) for h in range(ATT_HEADS)], axis=0)
    o_ref[0] = out_t.T.astype(o_ref.dtype)


def dsa_prompt(q, qi, wi, k, vt, ki2):
    b, t_len, _ = q.shape
    kc = vt.shape[-1]
    tq = kc
    kc1 = math.gcd(t_len, 2 * kc)
    topk = min(TOPK_MAX, t_len // 4)
    return pl.pallas_call(
        functools.partial(_dsa_prompt_kernel, tq=tq, kc1=kc1, kc=kc, topk=topk, n_keys=t_len),
        grid=(b, t_len // tq),
        in_specs=[pl.BlockSpec((1, tq, W_GROUP), lambda i, j: (i, j, 0)),
                  pl.BlockSpec((1, tq, 2 * W_GROUP), lambda i, j: (i, j, 0)),
                  pl.BlockSpec((1, tq, LANES), lambda i, j: (i, j, 0)),
                  pl.BlockSpec((1, t_len, W_GROUP), lambda i, j: (i, 0, 0)),
                  pl.BlockSpec((1, t_len // kc, W_GROUP, kc), lambda i, j: (i, 0, 0, 0)),
                  pl.BlockSpec((1, t_len, LANES), lambda i, j: (i, 0, 0))],
        out_specs=pl.BlockSpec((1, tq, W_GROUP), lambda i, j: (i, j, 0)),
        out_shape=jax.ShapeDtypeStruct((b, t_len, W_GROUP), BF16),
        scratch_shapes=[pltpu.VMEM((t_len, tq), F32)] + [pltpu.VMEM((HEAD_DIM, tq), F32)] * ATT_HEADS + [pltpu.VMEM((kc, tq), F32)] * (2 * ATT_HEADS),
        compiler_params=_cparams(("parallel", "arbitrary")),
        name="dsa_prompt",
    )(q, qi, wi, k, vt, ki2)


def _dsa_sample_kernel(*refs, g, t, n_pages, page, topk):
    pt_ref, q_ref, qi_ref, wi_ref, kn_ref, vn_ref, kin_ref = refs[:7]
    n_pg = g * n_pages
    kt_refs, vt_refs, kit_refs = refs[7:7 + n_pg], refs[7 + n_pg:7 + 2 * n_pg], refs[7 + 2 * n_pg:7 + 3 * n_pg]
    o_ref, score_ref = refs[7 + 3 * n_pg], refs[8 + 3 * n_pg]
    del pt_ref
    past = n_pages * page
    s_len = past + LANES
    n_tiles = s_len // LANES
    lane = lax.broadcasted_iota(I32, (t, LANES), 1)
    lane_o = lax.broadcasted_iota(I32, (t, W_GROUP), 1)
    col = lax.broadcasted_iota(I32, (t, s_len), 1)
    causal = col <= past + lax.broadcasted_iota(I32, (t, s_len), 0)
    neg_inf = jnp.float32(-jnp.inf)

    def pad_rows(x):
        return jnp.concatenate([x.astype(F32), jnp.zeros((LANES - t, x.shape[1]), F32)], axis=0).astype(BF16)

    def pages(page_refs, b):
        return jnp.concatenate([page_refs[b * n_pages + p][0, 0] for p in range(n_pages)], axis=1).astype(BF16)

    for b in range(g):
        qi = qi_ref[b].astype(F32)
        lhs = jnp.concatenate(
            [jnp.where((lane // IDX_DIM) == (h % 2), qi[:, (h // 2) * LANES:(h // 2 + 1) * LANES], 0.0)
             for h in range(IDX_HEADS)], axis=0).astype(BF16)
        kit = pages(kit_refs, b)
        dots = jnp.concatenate([_dot(lhs, jnp.concatenate([kit, kit], axis=0)),
                                _dot_nt(lhs, pad_rows(kin_ref[b]))], axis=1)
        wi = wi_ref[b]
        acc = jnp.zeros((t, s_len), F32)
        for h in range(IDX_HEADS):
            acc = acc + wi[:, h:h + 1] * jnp.maximum(dots[h * t:(h + 1) * t], 0.0)
        score = jnp.where(causal, acc, neg_inf)
        for j in range(n_tiles):
            score_ref[j, b * t:(b + 1) * t, :] = score[:, j * LANES:(j + 1) * LANES]

    count = functools.partial(_count_rows, score_ref, n_tiles)
    thr = _kth_largest(count, topk, two_bits_per_trip=True)
    bound = _tie_bound(count, thr, (g * t, 1), topk, s_len)

    for b in range(g):
        rows = slice(b * t, (b + 1) * t)
        score = jnp.concatenate([score_ref[j, rows, :] for j in range(n_tiles)], axis=1)
        sel = ((score > thr[rows]) | ((score == thr[rows]) & (col < bound[rows]))) & causal
        q = q_ref[b].astype(F32)
        lhs = jnp.concatenate([jnp.where((lane_o // HEAD_DIM) == h, q, 0.0) for h in range(ATT_HEADS)],
                              axis=0).astype(BF16)
        logit = jnp.concatenate([_dot(lhs, pages(kt_refs, b)), _dot_nt(lhs, pad_rows(kn_ref[b]))], axis=1)
        logit = jnp.where(jnp.concatenate([sel] * ATT_HEADS, axis=0), logit, neg_inf)
        p = jnp.exp2(logit - jnp.max(logit, axis=1, keepdims=True))
        pb = p.astype(BF16)
        res = _dot_nt(pb[:, :past], pages(vt_refs, b)) + _dot(pb[:, past:], pad_rows(vn_ref[b]))
        res = res / jnp.sum(p, axis=1, keepdims=True)
        out = jnp.zeros((t, W_GROUP), F32)
        for h in range(ATT_HEADS):
            out = jnp.where((lane_o // HEAD_DIM) == h, res[h * t:(h + 1) * t], out)
        o_ref[b] = out.astype(o_ref.dtype)


def dsa_sample(q, qi, wi, k_new, v_new, ki2_new, kt_pages, vt_pages, kit_pages, page_table, layer, *, g=4):
    b, t_len, _ = q.shape
    n_pages = page_table.shape[1]
    page = kt_pages.shape[-1]
    g = math.gcd(g, b)
    topk = min(TOPK_MAX, (n_pages * page + t_len) // 4)

    def spec(width):
        return pl.BlockSpec((g, t_len, width), lambda i, pt: (i, 0, 0))

    def page_specs(rows):
        return [pl.BlockSpec((1, 1, rows, page),
                             lambda i, pt, s=s, p=p: (pt[(i * g + s) * n_pages + p], layer, 0, 0))
                for s in range(g) for p in range(n_pages)]

    grid_spec = pltpu.PrefetchScalarGridSpec(
        num_scalar_prefetch=1,
        grid=(b // g,),
        in_specs=[spec(W_GROUP), spec(2 * W_GROUP), spec(LANES), spec(W_GROUP), spec(W_GROUP), spec(LANES)]
        + page_specs(W_GROUP) + page_specs(W_GROUP) + page_specs(IDX_DIM),
        out_specs=spec(W_GROUP),
        scratch_shapes=[pltpu.VMEM((n_pages + 1, g * t_len, LANES), F32)],
    )
    n_pg = g * n_pages
    return pl.pallas_call(
        functools.partial(_dsa_sample_kernel, g=g, t=t_len, n_pages=n_pages, page=page, topk=topk),
        grid_spec=grid_spec,
        out_shape=jax.ShapeDtypeStruct((b, t_len, W_GROUP), BF16),
        compiler_params=_cparams(("parallel",)),
        name="dsa_sample",
    )(page_table.reshape(-1), q, qi, wi, k_new, v_new, ki2_new,
      *([kt_pages] * n_pg), *([vt_pages] * n_pg), *([kit_pages] * n_pg))


def _mem_attn_kernel(q_ref, mk_ref, mv_ref, o_ref, *, g):
    for b in range(g):
        q = q_ref[b]
        outs = []
        for h in range(MEM_HEADS):
            sl = slice(h * MEM_HEAD_DIM, (h + 1) * MEM_HEAD_DIM)
            logit = _dot_nt(q[:, sl], mk_ref[b, 0, :, sl].astype(BF16)) * (MEM_HEAD_DIM ** -0.5)
            m = jnp.max(logit, axis=1, keepdims=True)
            p = jnp.exp(logit - m)
            p = p / jnp.sum(p, axis=1, keepdims=True)
            outs.append(_dot(p.astype(BF16), mv_ref[b, 0, :, sl].astype(BF16)))
        o_ref[b] = jnp.concatenate(outs, axis=-1).astype(o_ref.dtype)


def mem_attention(q, mk, mv, layer, *, tq=512, g=1):
    b, t_len, d = q.shape
    m_len = mk.shape[2]
    if t_len >= tq:
        g = 1
    else:
        tq, g = t_len, math.gcd(g, b)
    mem_spec = pl.BlockSpec((g, 1, m_len, d), lambda i, j: (i, layer, 0, 0))
    return pl.pallas_call(
        functools.partial(_mem_attn_kernel, g=g),
        grid=(b // g, t_len // tq),
        in_specs=[pl.BlockSpec((g, tq, d), lambda i, j: (i, j, 0)), mem_spec, mem_spec],
        out_specs=pl.BlockSpec((g, tq, d), lambda i, j: (i, j, 0)),
        out_shape=jax.ShapeDtypeStruct((b, t_len, d), BF16),
        compiler_params=_cparams(("parallel", "arbitrary")),
        name="mem_attention",
    )(q, mk, mv)


def _prep_layer(l, p):
    w = p['w_in'][l]
    cuts = [0]
    for s in (W_GROUP,) * 9 + (IDX_HEADS * IDX_DIM, IDX_DIM, IDX_HEADS):
        cuts.append(cuts[-1] + s)
    piece = lambda i: w[:, cuts[i]:cuts[i + 1]]
    zeros = lambda n: jnp.zeros((D_MODEL, n), w.dtype)
    w_in = jnp.concatenate(
        [piece(i) for i in range(6)] + [piece(6), piece(7), piece(9), piece(8), piece(10), zeros(LANES - IDX_DIM),
                                        piece(11), zeros(LANES - IDX_HEADS)], axis=1).astype(BF16)
    pool_w = p['pool_w'][l]
    pool_bd = jnp.zeros((W_GROUP, W_GROUP), F32)
    for g in range(len(POOL_WINDOWS)):
        pool_bd = pool_bd.at[g * POOL_CH:(g + 1) * POOL_CH, g * POOL_CH:(g + 1) * POOL_CH].set(pool_w[g])
    return {
        'norm_mix': p['norm_mix'][l], 'w_in': w_in,
        'conv_w': p['conv_w'][l], 'pool_w_bd': pool_bd.astype(BF16),
        'pool_scale': p['pool_scale'][l].reshape(1, W_GROUP), 'chunk_norm': p['chunk_norm'][l].reshape(1, W_GROUP),
        'chunk_ws': p['chunk_ws'][l], 'chunk_b': p['chunk_b'][l],
        'w_out_abc': p['w_out'][l][:3 * W_GROUP].astype(BF16), 'w_out_d': p['w_out'][l][3 * W_GROUP:].astype(BF16),
        'norm_mem': p['norm_mem'][l], 'wq_mem': p['wq_mem'][l].astype(BF16), 'wo_mem': p['wo_mem'][l].astype(BF16),
        'norm_ffn': p['norm_ffn'][l],
    }


def _chunk_consts(lw, chunk):
    mask = jnp.tril(jnp.ones((chunk, chunk), dtype=bool))
    wm = jnp.where(mask[None], lw['chunk_ws'][:, :chunk, :chunk], 0).astype(BF16)
    bias = jnp.repeat(lw['chunk_b'][:, :chunk].T, W_GROUP // CHUNK_HEADS, axis=1)
    return wm, bias


def _rope_tables(pos):
    half = HEAD_DIM // 2
    inv = ROPE_THETA ** (-jnp.arange(half, dtype=F32) / half)
    ang = pos.astype(F32)[:, None] * inv[None, :]
    cos, sin = jnp.cos(ang), jnp.sin(ang)
    cos_t = jnp.concatenate([cos, cos] * (LANES // HEAD_DIM), axis=1)
    sin_t = jnp.concatenate([-sin, sin] * (LANES // HEAD_DIM), axis=1)
    return cos_t, sin_t


def _run_trunk(x, pos0, conv_prev, pool_prev, past, mem_k, mem_v, layers, ffn_w, norm_final):
    b, t_len, d = x.shape
    n = b * t_len
    depth = len(layers)
    from_state = past is not None
    pos = pos0 + jnp.arange(t_len, dtype=I32)
    cos_t, sin_t = _rope_tables(pos)
    h = x.reshape(n, d)
    ks, vs, kis, convs, pools, cvs = [], [], [], [], [], []
    for l, lw in enumerate(layers):
        proj = norm_matmul(h, lw['norm_mix'], lw['w_in']).reshape(b, t_len, P_PAD)
        chunk = min(t_len, CHUNK)
        wm, cbias = _chunk_consts(lw, chunk)
        lw_m = dict(lw, chunk_wm=wm, chunk_bias=cbias)
        outs = mixers(proj, conv_prev[:, l] if from_state else None, pool_prev[:, l] if from_state else None,
                      cos_t, sin_t, lw_m, pos0=pos0, from_state=from_state, emit_cv=from_state)
        mix, q, k, kb, v, vb, qi, ki, ki2, wi, conv_s, pool_s = outs[:12]
        if from_state:
            cvs.append(outs[12])
            yd = dsa_sample(q, qi, wi, kb, vb, ki2, *past, l)
        else:
            yd = dsa_prompt(q, qi, wi, kb, vb, ki2)
        h = matmul_res([mix.reshape(n, 3 * W_GROUP), yd.reshape(n, W_GROUP)], [lw['w_out_abc'], lw['w_out_d']], h)
        qm = norm_matmul(h, lw['norm_mem'], lw['wq_mem'], out_dtype=BF16).reshape(b, t_len, d)
        om = mem_attention(qm, mem_k, mem_v, l)
        h = matmul_res([om.reshape(n, d)], [lw['wo_mem']], h)
        last = l == depth - 1
        fw = ffn_w[l]
        if fw['kind'] == 'dense':
            h = ffn_dense(h, lw['norm_ffn'], fw['w1'], fw['w3'], fw['w2'], norm_final, final_norm=last)
        else:
            h = moe_ffn(h, lw['norm_ffn'], fw['router'], fw['w1'], fw['w3'], fw['w2'], norm_final, final_norm=last)
        ks.append(k.reshape(b, t_len, ATT_HEADS, HEAD_DIM))
        vs.append(v.reshape(b, t_len, ATT_HEADS, HEAD_DIM))
        kis.append(ki[:, :, :IDX_DIM])
        convs.append(conv_s)
        pools.append(pool_s)
    st = lambda xs: jnp.stack(xs, axis=1)
    y = h.reshape(b, t_len, d)
    return y, st(ks), st(vs), st(kis), st(convs), st(pools), (st(cvs) if cvs else None)


def kernel(x_prompt, x_sample, mem_prompt, cache_k, cache_v, cache_kidx, cache_mem_k, cache_mem_v, state_conv, state_pool, page_table, norm_mix, w_in, conv_w, pool_w, pool_scale, chunk_norm, chunk_ws, chunk_b, w_out, norm_mem, wq_mem, wk_mem, wv_mem, wo_mem, norm_ffn, w1_dense, w3_dense, w2_dense, router, w1_moe, w3_moe, w2_moe, norm_final):
    p = {
        'norm_mix': norm_mix, 'w_in': w_in, 'conv_w': conv_w, 'pool_w': pool_w, 'pool_scale': pool_scale,
        'chunk_norm': chunk_norm, 'chunk_ws': chunk_ws, 'chunk_b': chunk_b, 'w_out': w_out, 'norm_mem': norm_mem,
        'wq_mem': wq_mem, 'wo_mem': wo_mem, 'norm_ffn': norm_ffn,
    }
    depth = norm_mix.shape[0]
    layers = [_prep_layer(l, p) for l in range(depth)]
    ffn_w = []
    for l in range(depth):
        j = l // 2
        if l % 2 == 0:
            ffn_w.append({'kind': 'dense', 'w1': w1_dense[j].astype(BF16), 'w3': w3_dense[j].astype(BF16),
                          'w2': w2_dense[j].astype(BF16)})
        else:
            router_pad = jnp.concatenate([router[j], jnp.zeros((D_MODEL, LANES - N_EXPERTS), F32)], axis=1)
            ffn_w.append({'kind': 'moe', 'router': router_pad, 'w1': w1_moe[j].astype(BF16),
                          'w3': w3_moe[j].astype(BF16), 'w2': w2_moe[j].astype(BF16)})

    bp, tp, d = x_prompt.shape
    n_mem = mem_prompt.shape[1]
    w_kv = jnp.concatenate([wk_mem[l] for l in range(depth)] + [wv_mem[l] for l in range(depth)], axis=1).astype(BF16)
    mem_kv = norm_matmul(mem_prompt.reshape(bp * n_mem, d), jnp.ones((d,), F32), w_kv, norm=False)
    mem_kv = mem_kv.reshape(bp, n_mem, 2, depth, d)
    mem_k_p = jnp.moveaxis(mem_kv[:, :, 0], 2, 1)
    mem_v_p = jnp.moveaxis(mem_kv[:, :, 1], 2, 1)
    y_p, k_p, v_p, ki_p, conv_p, pool_p, _ = _run_trunk(
        x_prompt, 0, None, None, None, mem_k_p, mem_v_p, layers, ffn_w, norm_final)

    bs, ts, _ = x_sample.shape
    n_pages, page = page_table.shape[1], cache_k.shape[2]
    past_len = n_pages * page

    n_phys = cache_k.shape[0]
    kt_pages = jnp.transpose(cache_k, (0, 1, 3, 4, 2)).reshape(n_phys, depth, W_GROUP, page)
    vt_pages = jnp.transpose(cache_v, (0, 1, 3, 4, 2)).reshape(n_phys, depth, W_GROUP, page)
    kit_pages = jnp.transpose(cache_kidx, (0, 1, 3, 2))
    past = (kt_pages, vt_pages, kit_pages, page_table)
    y_s, k_s, v_s, ki_s, conv_s, pool_s, cv_s = _run_trunk(
        x_sample, past_len, state_conv, state_pool, past, cache_mem_k.reshape(bs, depth, n_mem, d),
        cache_mem_v.reshape(bs, depth, n_mem, d), layers, ffn_w, norm_final)

    mem_shape = (bp, depth, n_mem, MEM_HEADS, MEM_HEAD_DIM)
    return (y_p, y_s, k_p, v_p, ki_p, mem_k_p.reshape(mem_shape), mem_v_p.reshape(mem_shape), conv_p, pool_p,
            k_s, v_s, ki_s, conv_s, pool_s, cv_s)
```

```python
import functools
import math

import jax
import jax.numpy as jnp
from jax import lax
from jax.experimental import pallas as pl
from jax.experimental.pallas import tpu as pltpu

F32 = jnp.float32
BF16 = jnp.bfloat16
I32 = jnp.int32

D_MODEL = 1024
W_GROUP = 256
CONV_W = 3
POOL_WINDOWS = (2, 4, 8, 16)
POOL_CH = 64
POOL_BUF = 15
CHUNK = 128
CHUNK_HEADS = 4
ATT_HEADS = 4
HEAD_DIM = 64
IDX_HEADS = 8
IDX_DIM = 64
IDX_W_SCALE = (IDX_HEADS ** -0.5) * (IDX_DIM ** -0.5)
TOPK_MAX = 256
ROPE_THETA = 10000.0
MEM_HEADS = 4
MEM_HEAD_DIM = 256
D_FF = 3584
N_EXPERTS = 8
EPS = 1e-6

LANES = 128
SUBLANES = 8
PREV_ROWS = 16
P_PAD = 3072
HALF = P_PAD // 2
OFF_Q, OFF_K, OFF_QI, OFF_V, OFF_KI, OFF_WI = 1536, 1792, 2048, 2560, 2816, 2944
VMEM_LIMIT = 56 * 1024 * 1024

INT_MIN = -2 ** 31
Q_SCALE = (HEAD_DIM ** -0.5) * math.log2(math.e)


def _cparams(sem):
    return pltpu.CompilerParams(dimension_semantics=sem, vmem_limit_bytes=VMEM_LIMIT)


def _rms(x, g):
    ms = jnp.mean(x * x, axis=-1, keepdims=True)
    return x * lax.rsqrt(ms + EPS) * g


def _dot(a, b):
    return jnp.dot(a, b, preferred_element_type=F32)


def _dot_nt(a, b):
    return lax.dot_general(a, b, (((1,), (1,)), ((), ())), preferred_element_type=F32)


def _norm_matmul_kernel(x_ref, g_ref, w_ref, o_ref, xn_ref, *, norm):
    @pl.when(pl.program_id(1) == 0)
    def _():
        x = x_ref[...]
        if norm:
            x = _rms(x, g_ref[...])
        xn_ref[...] = x.astype(BF16)

    o_ref[...] = _dot(xn_ref[...], w_ref[...]).astype(o_ref.dtype)


def norm_matmul(x, g, w, *, norm=True, out_dtype=F32, tm=1024, tn=1536):
    m, k = x.shape
    n = w.shape[1]
    tm = min(tm, m)
    tn = tn if n % tn == 0 else math.gcd(n, 1024)
    return pl.pallas_call(
        functools.partial(_norm_matmul_kernel, norm=norm),
        grid=(m // tm, n // tn),
        in_specs=[pl.BlockSpec((tm, k), lambda i, j: (i, 0)),
                  pl.BlockSpec((1, k), lambda i, j: (0, 0)),
                  pl.BlockSpec((k, tn), lambda i, j: (0, j))],
        out_specs=pl.BlockSpec((tm, tn), lambda i, j: (i, j)),
        out_shape=jax.ShapeDtypeStruct((m, n), out_dtype),
        scratch_shapes=[pltpu.VMEM((tm, k), BF16)],
        compiler_params=_cparams(("parallel", "arbitrary")),
        name="norm_matmul",
    )(x, g.reshape(1, k), w)


def _matmul_res_kernel(*refs, n_in):
    a_refs, w_refs = refs[:n_in], refs[n_in:2 * n_in]
    r_ref, o_ref = refs[2 * n_in], refs[2 * n_in + 1]
    acc = r_ref[...]
    for a, w in zip(a_refs, w_refs):
        acc = acc + _dot(a[...].astype(BF16), w[...])
    o_ref[...] = acc


def matmul_res(a_list, w_list, res, *, tm=1024):
    m, n = res.shape
    tm = min(tm, m)
    n_in = len(a_list)
    in_specs = [pl.BlockSpec((tm, a.shape[1]), lambda i: (i, 0)) for a in a_list]
    in_specs += [pl.BlockSpec(w.shape, lambda i: (0, 0)) for w in w_list]
    in_specs += [pl.BlockSpec((tm, n), lambda i: (i, 0))]
    return pl.pallas_call(
        functools.partial(_matmul_res_kernel, n_in=n_in),
        grid=(m // tm,),
        in_specs=in_specs,
        out_specs=pl.BlockSpec((tm, n), lambda i: (i, 0)),
        out_shape=jax.ShapeDtypeStruct((m, n), F32),
        compiler_params=_cparams(("parallel",)),
        name="matmul_res",
    )(*a_list, *w_list, res)


def _ffn_kernel(x_ref, g_ref, w1_ref, w3_ref, w2_ref, gf_ref, o_ref, xn_ref, acc_ref, *, final_norm):
    f = pl.program_id(1)

    @pl.when(f == 0)
    def _():
        xn_ref[...] = _rms(x_ref[...], g_ref[...]).astype(BF16)
        acc_ref[...] = jnp.zeros_like(acc_ref)

    xn = xn_ref[...]
    a = _dot(xn, w1_ref[...])
    b = _dot(xn, w3_ref[...])
    hidden = (a * jax.nn.sigmoid(a) * b).astype(BF16)
    acc_ref[...] += _dot(hidden, w2_ref[...])

    @pl.when(f == pl.num_programs(1) - 1)
    def _():
        y = x_ref[...] + acc_ref[...]
        if final_norm:
            y = _rms(y, gf_ref[...])
        o_ref[...] = y


def ffn_dense(x, g, w1, w3, w2, g_final, *, final_norm, tm=1024, tf=896):
    m, d = x.shape
    ff = w1.shape[1]
    tm = min(tm, m)
    return pl.pallas_call(
        functools.partial(_ffn_kernel, final_norm=final_norm),
        grid=(m // tm, ff // tf),
        in_specs=[pl.BlockSpec((tm, d), lambda i, f: (i, 0)),
                  pl.BlockSpec((1, d), lambda i, f: (0, 0)),
                  pl.BlockSpec((d, tf), lambda i, f: (0, f)),
                  pl.BlockSpec((d, tf), lambda i, f: (0, f)),
                  pl.BlockSpec((tf, d), lambda i, f: (f, 0)),
                  pl.BlockSpec((1, d), lambda i, f: (0, 0))],
        out_specs=pl.BlockSpec((tm, d), lambda i, f: (i, 0)),
        out_shape=jax.ShapeDtypeStruct((m, d), F32),
        scratch_shapes=[pltpu.VMEM((tm, d), BF16), pltpu.VMEM((tm, d), F32)],
        compiler_params=_cparams(("parallel", "arbitrary")),
        name="ffn_dense",
    )(x, g.reshape(1, d), w1, w3, w2, g_final.reshape(1, d))


def _router_kernel(x_ref, g_ref, r_ref, hn_ref, e_ref, gate_ref):
    hn = _rms(x_ref[...], g_ref[...])
    hn_ref[...] = hn.astype(BF16)
    logits = jnp.dot(hn, r_ref[...], preferred_element_type=F32, precision=lax.Precision.HIGHEST)
    lane = lax.broadcasted_iota(I32, logits.shape, 1)
    neg = jnp.float32(-jnp.inf)
    l1 = jnp.where(lane < N_EXPERTS, logits, neg)
    m1 = jnp.max(l1, axis=1, keepdims=True)
    i1 = jnp.min(jnp.where(l1 == m1, lane, LANES), axis=1, keepdims=True)
    l2 = jnp.where(lane == i1, neg, l1)
    m2 = jnp.max(l2, axis=1, keepdims=True)
    i2 = jnp.min(jnp.where(l2 == m2, lane, LANES), axis=1, keepdims=True)
    t = jnp.exp(m2 - m1)
    denom = 1.0 + t
    e_ref[...] = jnp.where(lane == 0, i1, jnp.where(lane == 1, i2, 0))
    gate_ref[...] = jnp.where(lane == 0, 1.0 / denom, jnp.where(lane == 1, t / denom, 0.0))


def moe_router(x, g, router_pad, *, tm=512):
    m, d = x.shape
    tm = min(tm, m)
    return pl.pallas_call(
        _router_kernel,
        grid=(m // tm,),
        in_specs=[pl.BlockSpec((tm, d), lambda i: (i, 0)),
                  pl.BlockSpec((1, d), lambda i: (0, 0)),
                  pl.BlockSpec((d, LANES), lambda i: (0, 0))],
        out_specs=[pl.BlockSpec((tm, d), lambda i: (i, 0)),
                   pl.BlockSpec((tm, LANES), lambda i: (i, 0)),
                   pl.BlockSpec((tm, LANES), lambda i: (i, 0))],
        out_shape=[jax.ShapeDtypeStruct((m, d), BF16),
                   jax.ShapeDtypeStruct((m, LANES), I32),
                   jax.ShapeDtypeStruct((m, LANES), F32)],
        compiler_params=_cparams(("parallel",)),
        name="moe_router",
    )(x, g.reshape(1, d), router_pad)


def _expert_kernel(be_ref, nb_ref, xs_ref, gate_ref, w1_ref, w3_ref, w2_ref, o_ref, acc_ref):
    j, f = pl.program_id(0), pl.program_id(1)
    live = j < nb_ref[0]

    @pl.when(f == 0)
    def _():
        acc_ref[...] = jnp.zeros_like(acc_ref)

    @pl.when(live)
    def _():
        xs = xs_ref[...]
        a = _dot(xs, w1_ref[0])
        b = _dot(xs, w3_ref[0])
        hidden = (a * jax.nn.sigmoid(a) * b).astype(BF16)
        acc_ref[...] += _dot(hidden, w2_ref[0])

    @pl.when(f == pl.num_programs(1) - 1)
    def _():
        o_ref[...] = acc_ref[...] * gate_ref[...]


def moe_experts(block_e, n_live, xs, slot_gate, w1, w3, w2, *, bm, tf=896):
    n_slots, d = xs.shape
    ff = w1.shape[2]
    n_blocks = n_slots // bm
    grid_spec = pltpu.PrefetchScalarGridSpec(
        num_scalar_prefetch=2,
        grid=(n_blocks, ff // tf),
        in_specs=[pl.BlockSpec((bm, d), lambda j, f, be, nb: (j, 0)),
                  pl.BlockSpec((bm, 1), lambda j, f, be, nb: (j, 0)),
                  pl.BlockSpec((1, d, tf), lambda j, f, be, nb: (be[j], 0, f)),
                  pl.BlockSpec((1, d, tf), lambda j, f, be, nb: (be[j], 0, f)),
                  pl.BlockSpec((1, tf, d), lambda j, f, be, nb: (be[j], f, 0))],
        out_specs=pl.BlockSpec((bm, d), lambda j, f, be, nb: (j, 0)),
        scratch_shapes=[pltpu.VMEM((bm, d), F32)],
    )
    return pl.pallas_call(
        _expert_kernel,
        grid_spec=grid_spec,
        out_shape=jax.ShapeDtypeStruct((n_slots, d), F32),
        compiler_params=_cparams(("parallel", "arbitrary")),
        name="moe_experts",
    )(block_e, n_live, xs, slot_gate, w1, w3, w2)


def _combine_kernel(x_ref, a_ref, b_ref, gf_ref, o_ref, *, final_norm):
    y = x_ref[...] + (a_ref[...] + b_ref[...])
    if final_norm:
        y = _rms(y, gf_ref[...])
    o_ref[...] = y


def moe_combine(x, ya, yb, g_final, *, final_norm, tm=512):
    m, d = x.shape
    tm = min(tm, m)
    spec = pl.BlockSpec((tm, d), lambda i: (i, 0))
    return pl.pallas_call(
        functools.partial(_combine_kernel, final_norm=final_norm),
        grid=(m // tm,),
        in_specs=[spec, spec, spec, pl.BlockSpec((1, d), lambda i: (0, 0))],
        out_specs=spec,
        out_shape=jax.ShapeDtypeStruct((m, d), F32),
        compiler_params=_cparams(("parallel",)),
        name="moe_combine",
    )(x, ya, yb, g_final.reshape(1, d))


def _compact_kernel(ib_ref, ic_ref, first_ref, total_ref, tok_ref, hn_ref, o_ref, acc_ref, *, tc):
    i = pl.program_id(0)

    @pl.when(first_ref[i] == 1)
    def _():
        acc_ref[...] = jnp.zeros_like(acc_ref)

    @pl.when(i < total_ref[0])
    def _():
        local = tok_ref[...] - ic_ref[i] * tc
        lane = lax.broadcasted_iota(I32, (tok_ref.shape[0], tc), 1)
        onehot = jnp.where(local == lane, 1.0, 0.0).astype(BF16)
        acc_ref[...] += _dot(onehot, hn_ref[...])

    o_ref[...] = acc_ref[...].astype(o_ref.dtype)


def moe_compact(item_block, item_chunk, item_first, n_items, slot_tok, hn, *, bm, tc):
    n_slots = slot_tok.shape[0]
    n, d = hn.shape
    n_max = item_block.shape[0]
    grid_spec = pltpu.PrefetchScalarGridSpec(
        num_scalar_prefetch=4,
        grid=(n_max,),
        in_specs=[pl.BlockSpec((bm, 1), lambda i, ib, ic, fr, tot: (ib[i], 0)),
                  pl.BlockSpec((tc, d), lambda i, ib, ic, fr, tot: (ic[i], 0))],
        out_specs=pl.BlockSpec((bm, d), lambda i, ib, ic, fr, tot: (ib[i], 0)),
        scratch_shapes=[pltpu.VMEM((bm, d), F32)],
    )
    return pl.pallas_call(
        functools.partial(_compact_kernel, tc=tc),
        grid_spec=grid_spec,
        out_shape=jax.ShapeDtypeStruct((n_slots, d), BF16),
        compiler_params=_cparams(("arbitrary",)),
        name="moe_compact",
    )(item_block, item_chunk, item_first, n_items, slot_tok.reshape(n_slots, 1), hn)


def moe_ffn(x, g, router_pad, w1, w3, w2, g_final, *, final_norm):
    n, d = x.shape
    bm = 512 if n >= 4096 else 128
    tc = min(n, 512)
    hn, e12, g12 = moe_router(x, g, router_pad)
    e_flat = e12[:, :2].reshape(-1)
    g_flat = g12[:, :2].reshape(-1)
    n_assign = 2 * n
    order = jnp.argsort(e_flat, stable=True).astype(I32)
    rank = jnp.argsort(order).astype(I32)
    counts = jnp.sum(e_flat[:, None] == jnp.arange(N_EXPERTS, dtype=I32)[None, :], axis=0).astype(I32)
    padded = (counts + bm - 1) // bm * bm
    start = jnp.cumsum(counts) - counts
    pend = jnp.cumsum(padded)
    pstart = pend - padded
    n_blocks = -(-n_assign // bm) + N_EXPERTS
    n_slots = n_blocks * bm
    block_start = jnp.arange(n_blocks, dtype=I32) * bm
    block_e = jnp.minimum(jnp.sum(pend[None, :] <= block_start[:, None], axis=1), N_EXPERTS - 1).astype(I32)
    n_live = (pend[-1] // bm).astype(I32).reshape(1)
    slot_e = jnp.repeat(block_e, bm)
    slot_idx = jnp.arange(n_slots, dtype=I32) - pstart[slot_e]
    slot_live = slot_idx < counts[slot_e]
    slot_src = order[jnp.clip(start[slot_e] + slot_idx, 0, n_assign - 1)]
    slot_tok = jnp.where(slot_live, slot_src // 2, -1)
    slot_gate = jnp.where(slot_live, g_flat[slot_src], 0.0)
    assign_slot = pstart[e_flat] + rank - start[e_flat]
    tok_b = slot_tok.reshape(n_blocks, bm)
    c_lo = jnp.maximum(tok_b[:, 0], 0) // tc
    c_hi = jnp.maximum(jnp.max(tok_b, axis=1), 0) // tc
    n_items_b = c_hi - c_lo + 1
    item_end = jnp.cumsum(n_items_b)
    total = item_end[-1]
    n_items_max = n_blocks + N_EXPERTS * (n // tc)
    item_id = jnp.arange(n_items_max, dtype=I32)
    item = jnp.minimum(item_id, total - 1)
    item_block = jnp.sum(item_end[None, :] <= item[:, None], axis=1).astype(I32)
    item_off = item - (item_end - n_items_b)[item_block]
    item_chunk = (c_lo[item_block] + item_off).astype(I32)
    item_first = ((item_off == 0) & (item_id < total)).astype(I32)
    xs = moe_compact(item_block, item_chunk, item_first, total.astype(I32).reshape(1), slot_tok, hn, bm=bm, tc=tc)
    ys = moe_experts(block_e, n_live, xs, slot_gate.reshape(n_slots, 1), w1, w3, w2, bm=bm)
    pair = assign_slot.reshape(n, 2)
    return moe_combine(x, ys[pair[:, 0]], ys[pair[:, 1]], g_final, final_norm=final_norm)


def _swap_half_heads(x):
    n = x.shape[-1]
    lane = lax.broadcasted_iota(I32, x.shape, x.ndim - 1)
    fwd = pltpu.roll(x, n - HEAD_DIM // 2, x.ndim - 1)
    bwd = pltpu.roll(x, HEAD_DIM // 2, x.ndim - 1)
    return jnp.where(lane % HEAD_DIM < HEAD_DIM // 2, fwd, bwd)


def _rope(x, cos, sin):
    reps = x.shape[-1] // LANES
    if reps > 1:
        cos = jnp.concatenate([cos] * reps, axis=-1)
        sin = jnp.concatenate([sin] * reps, axis=-1)
    return x * cos + _swap_half_heads(x) * sin


def _mixer_kernel(*refs, tt, chunk, pos0, from_state, emit_cv):
    it = iter(refs)
    xa_ref, xb_ref = next(it), next(it)
    if from_state:
        convp_ref, poolp_ref = next(it), next(it)
    else:
        prev_ref = next(it)
    cos_ref, sin_ref = next(it), next(it)
    convw_ref, poolw_ref, pscale_ref, cnorm_ref, wm_ref, cbias_ref = (next(it) for _ in range(6))
    mix_ref, q_ref, k_ref, kb_ref, v_ref, vb_ref, qi_ref, ki_ref, ki2_ref, wi_ref, convs_ref, pools_ref = (
        next(it) for _ in range(12))
    cv_ref = next(it) if emit_cv else None
    ybuf, xbuf = next(it), next(it)

    t = pl.program_id(1)
    xa = xa_ref[0]
    a_in, a_b, a_c = xa[:, 0:256], xa[:, 256:512], xa[:, 512:768]
    b_in, c_u, c_v = xa[:, 768:1024], xa[:, 1024:1280], xa[:, 1280:1536]

    y = a_c * a_in
    if from_state:
        ybuf[PREV_ROWS - 2:PREV_ROWS, :] = convp_ref[0]
        xbuf[0:1, :] = jnp.zeros((1, W_GROUP), F32)
        xbuf[1:PREV_ROWS, :] = poolp_ref[0]
    else:
        first = (t == 0).astype(F32)
        prev = prev_ref[0] * (1.0 - first)
        ybuf[0:PREV_ROWS, :] = prev[:, 512:768] * prev[:, 0:256]
        xbuf[0:PREV_ROWS, :] = prev[:, 768:1024]
    ybuf[PREV_ROWS:PREV_ROWS + tt, :] = y
    xbuf[PREV_ROWS:PREV_ROWS + tt, :] = b_in

    cw = convw_ref[...]
    z = cw[0:1, :] * ybuf[PREV_ROWS - 2:PREV_ROWS - 2 + tt, :]
    z = z + cw[1:2, :] * ybuf[PREV_ROWS - 1:PREV_ROWS - 1 + tt, :]
    z = z + cw[2:3, :] * y
    ya = a_b * z
    convs_ref[0] = ybuf[PREV_ROWS + tt - 2:PREV_ROWS + tt, :]

    lane = lax.broadcasted_iota(I32, (tt, W_GROUP), 1)
    row = lax.broadcasted_iota(I32, (tt, W_GROUP), 0)
    pos = (pos0 + t * tt + row).astype(F32)
    run = b_in
    mean = jnp.zeros((tt, W_GROUP), F32)
    for j in range(1, POOL_WINDOWS[-1] + 1):
        if j > 1:
            run = run + xbuf[PREV_ROWS - (j - 1):PREV_ROWS - (j - 1) + tt, :]
        if j in POOL_WINDOWS:
            grp = POOL_WINDOWS.index(j)
            cnt = jnp.minimum(pos + 1.0, float(j))
            mean = jnp.where(lane // POOL_CH == grp, run / cnt, mean)
    dlt = (mean - b_in).astype(BF16)
    yb = _dot(dlt, poolw_ref[...]) * pscale_ref[...]
    pools_ref[0] = xbuf[PREV_ROWS + tt - POOL_BUF:PREV_ROWS + tt, :]

    vn = _rms(c_v, cnorm_ref[...])
    if emit_cv:
        cv_ref[0] = vn
    vnb = vn.astype(BF16)
    lane_c = lax.broadcasted_iota(I32, (chunk, W_GROUP), 1)
    parts = []
    for c in range(tt // chunk):
        vc = vnb[c * chunk:(c + 1) * chunk, :]
        s = cbias_ref[...]
        for hd in range(CHUNK_HEADS):
            sh = _dot(wm_ref[hd], vc)
            s = s + jnp.where(lane_c // (W_GROUP // CHUNK_HEADS) == hd, sh, 0.0)
        parts.append(s)
    s_all = parts[0] if len(parts) == 1 else jnp.concatenate(parts, axis=0)
    yc = c_u * s_all

    mix_ref[0] = jnp.concatenate([ya, yb, yc], axis=-1).astype(BF16)

    xb = xb_ref[0]
    cos, sin = cos_ref[...], sin_ref[...]
    q = _rope(xb[:, 0:256], cos, sin)
    k = _rope(xb[:, 256:512], cos, sin)
    qi = _rope(xb[:, 512:1024], cos, sin)
    v = xb[:, 1024:1280]
    ki = _rope(xb[:, 1280:1408], cos, sin)
    wi = xb[:, 1408:1536] * IDX_W_SCALE
    q_ref[0] = (q * Q_SCALE).astype(BF16)
    k_ref[0] = k
    kb_ref[0] = k.astype(BF16)
    v_ref[0] = v
    if from_state:
        vb_ref[0] = v.astype(BF16)
    else:
        vb_ref[0, 0] = v.T.astype(BF16)
    qi_ref[0] = qi.astype(BF16)
    ki_ref[0] = ki
    lane128 = lax.broadcasted_iota(I32, ki.shape, 1)
    ki2_ref[0] = jnp.where(lane128 < IDX_DIM, ki, pltpu.roll(ki, IDX_DIM, 1)).astype(BF16)
    wi_ref[0] = wi


def mixers(proj, conv_prev, pool_prev, cos, sin, lw, *, pos0, from_state, emit_cv):
    b, t_len, _ = proj.shape
    tt = min(t_len, 256)
    chunk = min(t_len, CHUNK)
    nt = t_len // tt
    in_specs = [pl.BlockSpec((1, tt, HALF), lambda i, t: (i, t, 0)),
                pl.BlockSpec((1, tt, HALF), lambda i, t: (i, t, 1))]
    args = [proj, proj]
    if from_state:
        in_specs += [pl.BlockSpec((1, CONV_W - 1, W_GROUP), lambda i, t: (i, 0, 0)),
                     pl.BlockSpec((1, POOL_BUF, W_GROUP), lambda i, t: (i, 0, 0))]
        args += [conv_prev, pool_prev]
    else:
        per = tt // PREV_ROWS
        in_specs += [pl.BlockSpec((1, PREV_ROWS, HALF), lambda i, t: (i, jnp.maximum(t * per - 1, 0), 0))]
        args += [proj]
    in_specs += [pl.BlockSpec((tt, LANES), lambda i, t: (t, 0)),
                 pl.BlockSpec((tt, LANES), lambda i, t: (t, 0))]
    args += [cos, sin]
    consts = [lw['conv_w'], lw['pool_w_bd'], lw['pool_scale'], lw['chunk_norm'], lw['chunk_wm'], lw['chunk_bias']]
    for c in consts:
        in_specs.append(pl.BlockSpec(c.shape, lambda i, t, nd=c.ndim: (0,) * nd))
    args += consts

    def tok(width, dtype):
        return (pl.BlockSpec((1, tt, width), lambda i, t: (i, t, 0)), jax.ShapeDtypeStruct((b, t_len, width), dtype))

    outs = [tok(768, BF16),
            tok(256, BF16),
            tok(256, F32), tok(256, BF16),
            tok(256, F32),
            tok(256, BF16) if from_state else
            (pl.BlockSpec((1, 1, W_GROUP, tt), lambda i, t: (i, t, 0, 0)),
             jax.ShapeDtypeStruct((b, nt, W_GROUP, tt), BF16)),
            tok(512, BF16),
            tok(LANES, F32),
            tok(LANES, BF16),
            tok(LANES, F32),
            (pl.BlockSpec((1, CONV_W - 1, W_GROUP), lambda i, t: (i, 0, 0)),
             jax.ShapeDtypeStruct((b, CONV_W - 1, W_GROUP), F32)),
            (pl.BlockSpec((1, POOL_BUF, W_GROUP), lambda i, t: (i, 0, 0)),
             jax.ShapeDtypeStruct((b, POOL_BUF, W_GROUP), F32))]
    if emit_cv:
        outs.append(tok(256, F32))
    return pl.pallas_call(
        functools.partial(_mixer_kernel, tt=tt, chunk=chunk, pos0=pos0, from_state=from_state, emit_cv=emit_cv),
        grid=(b, nt),
        in_specs=in_specs,
        out_specs=[o[0] for o in outs],
        out_shape=[o[1] for o in outs],
        scratch_shapes=[pltpu.VMEM((PREV_ROWS + tt, W_GROUP), F32), pltpu.VMEM((PREV_ROWS + tt, W_GROUP), F32)],
        compiler_params=_cparams(("parallel", "arbitrary")),
        name="mixers",
    )(*args)


def _ordered_f32(key):
    return pltpu.bitcast(key ^ ((key >> 31) & jnp.int32(0x7FFFFFFF)), F32)


def _count_rows(score_ref, n_tiles, pred):
    rows = score_ref.shape[1]
    lane = lax.broadcasted_iota(I32, (rows, LANES), 1)
    acc = jnp.zeros((rows, LANES), F32)
    for j in range(n_tiles):
        acc = acc + jnp.where(pred(score_ref[j], j * LANES + lane), 1.0, 0.0)
    return jnp.sum(acc, axis=1, keepdims=True)


def _count_cols(score_ref, n_groups, gsz, pred):
    cols = score_ref.shape[1]
    pos0 = lax.broadcasted_iota(I32, (gsz, cols), 0)

    def body(c, acc):
        blk = score_ref[pl.ds(pl.multiple_of(c * gsz, gsz), gsz), :]
        hit = jnp.where(pred(blk, c * gsz + pos0), 1.0, 0.0)
        return acc + jnp.sum(hit.reshape(gsz // SUBLANES, SUBLANES, cols), axis=0)

    acc = lax.fori_loop(0, n_groups, body, jnp.zeros((SUBLANES, cols), F32))
    return jnp.sum(acc, axis=0, keepdims=True)


def _kth_largest(count, topk, two_bits_per_trip):
    def enough(key):
        cand = _ordered_f32(key)
        return count(lambda s, pos: s >= cand) >= topk

    key = jnp.where(count(lambda s, pos: s >= 0.0) >= topk, 0, INT_MIN).astype(I32)
    if two_bits_per_trip:
        top = key | jnp.int32(1 << 30)
        key = jnp.where(enough(top), top, key)

        def two_bits(i, key):
            lo = 28 - 2 * i
            c1, c2, c3 = (key | (jnp.int32(v) << lo) for v in (1, 2, 3))
            return jnp.where(enough(c3), c3, jnp.where(enough(c2), c2, jnp.where(enough(c1), c1, key)))

        key = lax.fori_loop(0, 15, two_bits, key)
    else:
        def one_bit(i, key):
            cand = key | (jnp.int32(1) << (30 - i))
            return jnp.where(enough(cand), cand, key)

        key = lax.fori_loop(0, 31, one_bit, key)
    n_valid = count(lambda s, pos: s > -jnp.inf)
    return jnp.where(n_valid < topk, -jnp.inf, _ordered_f32(key))


def _tie_bound(count, thr, shape, topk, n_keys):
    need = topk - count(lambda s, pos: s > thr)
    n_eq = count(lambda s, pos: s == thr)
    idx_bits = n_keys.bit_length()
    excess = jnp.max(jnp.where((n_eq > need) & (thr > -jnp.inf), 1.0, 0.0))

    def tie_bound():
        def bound_bit(i, bound):
            cand = bound | (jnp.int32(1) << (idx_bits - 1 - i))
            hits = count(lambda s, pos: (s == thr) & (pos < cand))
            return jnp.where(hits <= need, cand, bound)

        return lax.fori_loop(0, idx_bits, bound_bit, jnp.zeros(shape, I32))

    return lax.cond(excess > 0.0, tie_bound, lambda: jnp.full(shape, 2 ** idx_bits - 1, I32))


def _dsa_prompt_kernel(q_ref, qi_ref, wi_ref, k_ref, vt_ref, ki2_ref, o_ref, score_ref, *scratch,
                       tq, kc1, kc, topk, n_keys):
    acc_refs, logit_refs = scratch[:ATT_HEADS], scratch[ATT_HEADS:]
    qb = pl.program_id(1)
    n_chunks = (qb * tq + tq + kc - 1) // kc
    neg_inf = jnp.float32(-jnp.inf)
    q_pos = qb * tq + lax.broadcasted_iota(I32, (kc, tq), 1)
    pos0 = lax.broadcasted_iota(I32, (kc, tq), 0)
    sub = lax.broadcasted_iota(I32, (LANES, tq), 0)

    q_t = q_ref[0].astype(F32).T
    qi_t = qi_ref[0].astype(F32).T
    w_t = wi_ref[0].T

    def head_tile(x_t, h, dim):
        tile = x_t[(h // 2) * LANES:(h // 2 + 1) * LANES]
        return jnp.where((sub // dim) == (h % 2), tile, 0.0).astype(BF16)

    qim = [head_tile(qi_t, h, IDX_DIM) for h in range(IDX_HEADS)]
    qhm = [head_tile(q_t, h, HEAD_DIM) for h in range(ATT_HEADS)]

    q_pos1 = qb * tq + lax.broadcasted_iota(I32, (kc1, tq), 1)
    pos1 = lax.broadcasted_iota(I32, (kc1, tq), 0)

    def score_chunk(c, carry):
        off = pl.multiple_of(c * kc1, kc1)
        ki2 = ki2_ref[0, pl.ds(off, kc1), :]
        acc = jnp.zeros((kc1, tq), F32)
        for h in range(IDX_HEADS):
            acc = acc + w_t[h:h + 1] * jnp.maximum(_dot(ki2, qim[h]), 0.0)
        score_ref[pl.ds(off, kc1), :] = jnp.where(c * kc1 + pos1 <= q_pos1, acc, neg_inf)
        return carry

    lax.fori_loop(0, (qb * tq + tq + kc1 - 1) // kc1, score_chunk, 0)

    count = functools.partial(_count_cols, score_ref, n_chunks, kc)
    thr = _kth_largest(count, topk, two_bits_per_trip=False)
    bound = _tie_bound(count, thr, (1, tq), topk, n_keys)

    for acc_ref in acc_refs:
        acc_ref[...] = jnp.zeros_like(acc_ref)

    logit_a, logit_b = logit_refs[:ATT_HEADS], logit_refs[ATT_HEADS:]
    last = n_chunks - 1

    def score_matmuls(c, dst):
        off = pl.multiple_of(c * kc, kc)
        for h in range(ATT_HEADS):
            dst[h][...] = _dot(k_ref[0, pl.ds(off, kc), (h // 2) * LANES:(h // 2 + 1) * LANES], qhm[h])

    def softmax_pv(c, live, src, m_run, l_run):
        pos = c * kc + jnp.where(live, 0, n_keys) + pos0
        score = score_ref[pl.ds(pl.multiple_of(c * kc, kc), kc), :]
        sel = ((score > thr) | ((score == thr) & (pos < bound))) & (pos <= q_pos)
        m_new, l_new, probs, alphas = [], [], [], []
        for h in range(ATT_HEADS):
            logit = jnp.where(sel, src[h][...], neg_inf)
            m_h = jnp.maximum(m_run[h], jnp.max(logit, axis=0, keepdims=True))
            m_safe = jnp.where(m_h == neg_inf, 0.0, m_h)
            p = jnp.exp2(logit - m_safe)
            alpha = jnp.exp2(m_run[h] - m_safe)
            l_new.append(alpha * l_run[h] + jnp.sum(p, axis=0, keepdims=True))
            m_new.append(m_h)
            probs.append(p.astype(BF16))
            alphas.append(alpha)
        for h in range(ATT_HEADS):
            pv = _dot(vt_ref[0, c, h * HEAD_DIM:(h + 1) * HEAD_DIM, :], probs[h])
            acc_refs[h][...] = acc_refs[h][...] * alphas[h] + pv
        return tuple(m_new), tuple(l_new)

    def attend_pair(i, carry):
        m_run, l_run = carry
        c0 = 2 * i
        c1 = jnp.minimum(c0 + 1, last)
        score_matmuls(c1, logit_b)
        m_run, l_run = softmax_pv(c0, True, logit_a, m_run, l_run)
        score_matmuls(jnp.minimum(c0 + 2, last), logit_a)
        return softmax_pv(c1, c0 + 1 <= last, logit_b, m_run, l_run)

    init = (tuple(jnp.full((1, tq), neg_inf, F32) for _ in range(ATT_HEADS)),
            tuple(jnp.zeros((1, tq), F32) for _ in range(ATT_HEADS)))
    score_matmuls(0, logit_a)
    _, l_fin = lax.fori_loop(0, (n_chunks + 1) // 2, attend_pair, init)
    out_t = jnp.concatenate([acc_refs[h][...] * (1.0 / l_fin[h]) for h in range(ATT_HEADS)], axis=0)
    o_ref[0] = out_t.T.astype(o_ref.dtype)


def dsa_prompt(q, qi, wi, k, vt, ki2):
    b, t_len, _ = q.shape
    kc = vt.shape[-1]
    tq = kc
    kc1 = math.gcd(t_len, 2 * kc)
    topk = min(TOPK_MAX, t_len // 4)
    return pl.pallas_call(
        functools.partial(_dsa_prompt_kernel, tq=tq, kc1=kc1, kc=kc, topk=topk, n_keys=t_len),
        grid=(b, t_len // tq),
        in_specs=[pl.BlockSpec((1, tq, W_GROUP), lambda i, j: (i, j, 0)),
                  pl.BlockSpec((1, tq, 2 * W_GROUP), lambda i, j: (i, j, 0)),
                  pl.BlockSpec((1, tq, LANES), lambda i, j: (i, j, 0)),
                  pl.BlockSpec((1, t_len, W_GROUP), lambda i, j: (i, 0, 0)),
                  pl.BlockSpec((1, t_len // kc, W_GROUP, kc), lambda i, j: (i, 0, 0, 0)),
                  pl.BlockSpec((1, t_len, LANES), lambda i, j: (i, 0, 0))],
        out_specs=pl.BlockSpec((1, tq, W_GROUP), lambda i, j: (i, j, 0)),
        out_shape=jax.ShapeDtypeStruct((b, t_len, W_GROUP), BF16),
        scratch_shapes=[pltpu.VMEM((t_len, tq), F32)] + [pltpu.VMEM((HEAD_DIM, tq), F32)] * ATT_HEADS + [pltpu.VMEM((kc, tq), F32)] * (2 * ATT_HEADS),
        compiler_params=_cparams(("parallel", "arbitrary")),
        name="dsa_prompt",
    )(q, qi, wi, k, vt, ki2)


def _dsa_sample_kernel(*refs, g, t, n_pages, page, topk):
    pt_ref, q_ref, qi_ref, wi_ref, kn_ref, vn_ref, kin_ref = refs[:7]
    n_pg = g * n_pages
    kt_refs, vt_refs, kit_refs = refs[7:7 + n_pg], refs[7 + n_pg:7 + 2 * n_pg], refs[7 + 2 * n_pg:7 + 3 * n_pg]
    o_ref, score_ref = refs[7 + 3 * n_pg], refs[8 + 3 * n_pg]
    del pt_ref
    past = n_pages * page
    s_len = past + LANES
    n_tiles = s_len // LANES
    lane = lax.broadcasted_iota(I32, (t, LANES), 1)
    lane_o = lax.broadcasted_iota(I32, (t, W_GROUP), 1)
    col = lax.broadcasted_iota(I32, (t, s_len), 1)
    causal = col <= past + lax.broadcasted_iota(I32, (t, s_len), 0)
    neg_inf = jnp.float32(-jnp.inf)

    def pad_rows(x):
        return jnp.concatenate([x.astype(F32), jnp.zeros((LANES - t, x.shape[1]), F32)], axis=0).astype(BF16)

    def pages(page_refs, b):
        return jnp.concatenate([page_refs[b * n_pages + p][0, 0] for p in range(n_pages)], axis=1).astype(BF16)

    for b in range(g):
        qi = qi_ref[b].astype(F32)
        lhs = jnp.concatenate(
            [jnp.where((lane // IDX_DIM) == (h % 2), qi[:, (h // 2) * LANES:(h // 2 + 1) * LANES], 0.0)
             for h in range(IDX_HEADS)], axis=0).astype(BF16)
        kit = pages(kit_refs, b)
        dots = jnp.concatenate([_dot(lhs, jnp.concatenate([kit, kit], axis=0)),
                                _dot_nt(lhs, pad_rows(kin_ref[b]))], axis=1)
        wi = wi_ref[b]
        acc = jnp.zeros((t, s_len), F32)
        for h in range(IDX_HEADS):
            acc = acc + wi[:, h:h + 1] * jnp.maximum(dots[h * t:(h + 1) * t], 0.0)
        score = jnp.where(causal, acc, neg_inf)
        for j in range(n_tiles):
            score_ref[j, b * t:(b + 1) * t, :] = score[:, j * LANES:(j + 1) * LANES]

    count = functools.partial(_count_rows, score_ref, n_tiles)
    thr = _kth_largest(count, topk, two_bits_per_trip=True)
    bound = _tie_bound(count, thr, (g * t, 1), topk, s_len)

    for b in range(g):
        rows = slice(b * t, (b + 1) * t)
        score = jnp.concatenate([score_ref[j, rows, :] for j in range(n_tiles)], axis=1)
        sel = ((score > thr[rows]) | ((score == thr[rows]) & (col < bound[rows]))) & causal
        q = q_ref[b].astype(F32)
        lhs = jnp.concatenate([jnp.where((lane_o // HEAD_DIM) == h, q, 0.0) for h in range(ATT_HEADS)],
                              axis=0).astype(BF16)
        logit = jnp.concatenate([_dot(lhs, pages(kt_refs, b)), _dot_nt(lhs, pad_rows(kn_ref[b]))], axis=1)
        logit = jnp.where(jnp.concatenate([sel] * ATT_HEADS, axis=0), logit, neg_inf)
        p = jnp.exp2(logit - jnp.max(logit, axis=1, keepdims=True))
        pb = p.astype(BF16)
        res = _dot_nt(pb[:, :past], pages(vt_refs, b)) + _dot(pb[:, past:], pad_rows(vn_ref[b]))
        res = res / jnp.sum(p, axis=1, keepdims=True)
        out = jnp.zeros((t, W_GROUP), F32)
        for h in range(ATT_HEADS):
            out = jnp.where((lane_o // HEAD_DIM) == h, res[h * t:(h + 1) * t], out)
        o_ref[b] = out.astype(o_ref.dtype)


def dsa_sample(q, qi, wi, k_new, v_new, ki2_new, kt_pages, vt_pages, kit_pages, page_table, layer, *, g=4):
    b, t_len, _ = q.shape
    n_pages = page_table.shape[1]
    page = kt_pages.shape[-1]
    g = math.gcd(g, b)
    topk = min(TOPK_MAX, (n_pages * page + t_len) // 4)

    def spec(width):
        return pl.BlockSpec((g, t_len, width), lambda i, pt: (i, 0, 0))

    def page_specs(rows):
        return [pl.BlockSpec((1, 1, rows, page),
                             lambda i, pt, s=s, p=p: (pt[(i * g + s) * n_pages + p], layer, 0, 0))
                for s in range(g) for p in range(n_pages)]

    grid_spec = pltpu.PrefetchScalarGridSpec(
        num_scalar_prefetch=1,
        grid=(b // g,),
        in_specs=[spec(W_GROUP), spec(2 * W_GROUP), spec(LANES), spec(W_GROUP), spec(W_GROUP), spec(LANES)]
        + page_specs(W_GROUP) + page_specs(W_GROUP) + page_specs(IDX_DIM),
        out_specs=spec(W_GROUP),
        scratch_shapes=[pltpu.VMEM((n_pages + 1, g * t_len, LANES), F32)],
    )
    n_pg = g * n_pages
    return pl.pallas_call(
        functools.partial(_dsa_sample_kernel, g=g, t=t_len, n_pages=n_pages, page=page, topk=topk),
        grid_spec=grid_spec,
        out_shape=jax.ShapeDtypeStruct((b, t_len, W_GROUP), BF16),
        compiler_params=_cparams(("parallel",)),
        name="dsa_sample",
    )(page_table.reshape(-1), q, qi, wi, k_new, v_new, ki2_new,
      *([kt_pages] * n_pg), *([vt_pages] * n_pg), *([kit_pages] * n_pg))


def _mem_attn_kernel(q_ref, mk_ref, mv_ref, o_ref, *, g):
    for b in range(g):
        q = q_ref[b]
        outs = []
        for h in range(MEM_HEADS):
            sl = slice(h * MEM_HEAD_DIM, (h + 1) * MEM_HEAD_DIM)
            logit = _dot_nt(q[:, sl], mk_ref[b, 0, :, sl].astype(BF16)) * (MEM_HEAD_DIM ** -0.5)
            m = jnp.max(logit, axis=1, keepdims=True)
            p = jnp.exp(logit - m)
            p = p / jnp.sum(p, axis=1, keepdims=True)
            outs.append(_dot(p.astype(BF16), mv_ref[b, 0, :, sl].astype(BF16)))
        o_ref[b] = jnp.concatenate(outs, axis=-1).astype(o_ref.dtype)


def mem_attention(q, mk, mv, layer, *, tq=512, g=1):
    b, t_len, d = q.shape
    m_len = mk.shape[2]
    if t_len >= tq:
        g = 1
    else:
        tq, g = t_len, math.gcd(g, b)
    mem_spec = pl.BlockSpec((g, 1, m_len, d), lambda i, j: (i, layer, 0, 0))
    return pl.pallas_call(
        functools.partial(_mem_attn_kernel, g=g),
        grid=(b // g, t_len // tq),
        in_specs=[pl.BlockSpec((g, tq, d), lambda i, j: (i, j, 0)), mem_spec, mem_spec],
        out_specs=pl.BlockSpec((g, tq, d), lambda i, j: (i, j, 0)),
        out_shape=jax.ShapeDtypeStruct((b, t_len, d), BF16),
        compiler_params=_cparams(("parallel", "arbitrary")),
        name="mem_attention",
    )(q, mk, mv)


def _mem_head_copies(mk_hbm, mv_hbm, kbuf, vbuf, sem, seq, layer, slot):
    copies = []
    for h in range(MEM_HEADS):
        cols = pl.ds(h * MEM_HEAD_DIM, MEM_HEAD_DIM)
        copies.append(pltpu.make_async_copy(mk_hbm.at[seq, layer, :, h, :], kbuf.at[slot, :, cols], sem.at[slot, h]))
        copies.append(pltpu.make_async_copy(mv_hbm.at[seq, layer, :, h, :], vbuf.at[slot, :, cols],
                                            sem.at[slot, MEM_HEADS + h]))
    return copies


def _mem_attn_split_kernel(q_ref, mk_hbm, mv_hbm, o_ref, kbuf, vbuf, sem, *, layer, n_seq):
    i = pl.program_id(0)
    slot = i % 2

    @pl.when(i == 0)
    def _():
        for c in _mem_head_copies(mk_hbm, mv_hbm, kbuf, vbuf, sem, 0, layer, 0):
            c.start()

    @pl.when(i + 1 < n_seq)
    def _():
        for c in _mem_head_copies(mk_hbm, mv_hbm, kbuf, vbuf, sem, i + 1, layer, 1 - slot):
            c.start()

    for c in _mem_head_copies(mk_hbm, mv_hbm, kbuf, vbuf, sem, i, layer, slot):
        c.wait()

    q = q_ref[0]
    outs = []
    for h in range(MEM_HEADS):
        sl = slice(h * MEM_HEAD_DIM, (h + 1) * MEM_HEAD_DIM)
        logit = _dot_nt(q[:, sl], kbuf[slot, :, sl].astype(BF16)) * (MEM_HEAD_DIM ** -0.5)
        m = jnp.max(logit, axis=1, keepdims=True)
        p = jnp.exp(logit - m)
        p = p / jnp.sum(p, axis=1, keepdims=True)
        outs.append(_dot(p.astype(BF16), vbuf[slot, :, sl].astype(BF16)))
    o_ref[0] = jnp.concatenate(outs, axis=-1).astype(o_ref.dtype)


def mem_attention_split(q, mk, mv, layer):
    b, t_len, d = q.shape
    m_len = mk.shape[2]
    return pl.pallas_call(
        functools.partial(_mem_attn_split_kernel, layer=layer, n_seq=b),
        grid=(b,),
        in_specs=[pl.BlockSpec((1, t_len, d), lambda i: (i, 0, 0)),
                  pl.BlockSpec(memory_space=pl.ANY), pl.BlockSpec(memory_space=pl.ANY)],
        out_specs=pl.BlockSpec((1, t_len, d), lambda i: (i, 0, 0)),
        out_shape=jax.ShapeDtypeStruct((b, t_len, d), BF16),
        scratch_shapes=[pltpu.VMEM((2, m_len, d), F32), pltpu.VMEM((2, m_len, d), F32),
                        pltpu.SemaphoreType.DMA((2, 2 * MEM_HEADS))],
        compiler_params=_cparams(("arbitrary",)),
        name="mem_attention_split",
    )(q, mk, mv)


def _prep_layer(l, p):
    w = p['w_in'][l]
    cuts = [0]
    for s in (W_GROUP,) * 9 + (IDX_HEADS * IDX_DIM, IDX_DIM, IDX_HEADS):
        cuts.append(cuts[-1] + s)
    piece = lambda i: w[:, cuts[i]:cuts[i + 1]]
    zeros = lambda n: jnp.zeros((D_MODEL, n), w.dtype)
    w_in = jnp.concatenate(
        [piece(i) for i in range(6)] + [piece(6), piece(7), piece(9), piece(8), piece(10), zeros(LANES - IDX_DIM),
                                        piece(11), zeros(LANES - IDX_HEADS)], axis=1).astype(BF16)
    pool_w = p['pool_w'][l]
    pool_bd = jnp.zeros((W_GROUP, W_GROUP), F32)
    for g in range(len(POOL_WINDOWS)):
        pool_bd = pool_bd.at[g * POOL_CH:(g + 1) * POOL_CH, g * POOL_CH:(g + 1) * POOL_CH].set(pool_w[g])
    return {
        'norm_mix': p['norm_mix'][l], 'w_in': w_in,
        'conv_w': p['conv_w'][l], 'pool_w_bd': pool_bd.astype(BF16),
        'pool_scale': p['pool_scale'][l].reshape(1, W_GROUP), 'chunk_norm': p['chunk_norm'][l].reshape(1, W_GROUP),
        'chunk_ws': p['chunk_ws'][l], 'chunk_b': p['chunk_b'][l],
        'w_out_abc': p['w_out'][l][:3 * W_GROUP].astype(BF16), 'w_out_d': p['w_out'][l][3 * W_GROUP:].astype(BF16),
        'norm_mem': p['norm_mem'][l], 'wq_mem': p['wq_mem'][l].astype(BF16), 'wo_mem': p['wo_mem'][l].astype(BF16),
        'norm_ffn': p['norm_ffn'][l],
    }


def _chunk_consts(lw, chunk):
    mask = jnp.tril(jnp.ones((chunk, chunk), dtype=bool))
    wm = jnp.where(mask[None], lw['chunk_ws'][:, :chunk, :chunk], 0).astype(BF16)
    bias = jnp.repeat(lw['chunk_b'][:, :chunk].T, W_GROUP // CHUNK_HEADS, axis=1)
    return wm, bias


def _rope_tables(pos):
    half = HEAD_DIM // 2
    inv = ROPE_THETA ** (-jnp.arange(half, dtype=F32) / half)
    ang = pos.astype(F32)[:, None] * inv[None, :]
    cos, sin = jnp.cos(ang), jnp.sin(ang)
    cos_t = jnp.concatenate([cos, cos] * (LANES // HEAD_DIM), axis=1)
    sin_t = jnp.concatenate([-sin, sin] * (LANES // HEAD_DIM), axis=1)
    return cos_t, sin_t


def _run_trunk(x, pos0, conv_prev, pool_prev, past, mem_k, mem_v, layers, ffn_w, norm_final):
    b, t_len, d = x.shape
    n = b * t_len
    depth = len(layers)
    from_state = past is not None
    pos = pos0 + jnp.arange(t_len, dtype=I32)
    cos_t, sin_t = _rope_tables(pos)
    h = x.reshape(n, d)
    ks, vs, kis, convs, pools, cvs = [], [], [], [], [], []
    for l, lw in enumerate(layers):
        proj = norm_matmul(h, lw['norm_mix'], lw['w_in']).reshape(b, t_len, P_PAD)
        chunk = min(t_len, CHUNK)
        wm, cbias = _chunk_consts(lw, chunk)
        lw_m = dict(lw, chunk_wm=wm, chunk_bias=cbias)
        outs = mixers(proj, conv_prev[:, l] if from_state else None, pool_prev[:, l] if from_state else None,
                      cos_t, sin_t, lw_m, pos0=pos0, from_state=from_state, emit_cv=from_state)
        mix, q, k, kb, v, vb, qi, ki, ki2, wi, conv_s, pool_s = outs[:12]
        if from_state:
            cvs.append(outs[12])
            yd = dsa_sample(q, qi, wi, kb, vb, ki2, *past, l)
        else:
            yd = dsa_prompt(q, qi, wi, kb, vb, ki2)
        h = matmul_res([mix.reshape(n, 3 * W_GROUP), yd.reshape(n, W_GROUP)], [lw['w_out_abc'], lw['w_out_d']], h)
        qm = norm_matmul(h, lw['norm_mem'], lw['wq_mem'], out_dtype=BF16).reshape(b, t_len, d)
        om = mem_attention(qm, mem_k, mem_v, l) if mem_k.ndim == 4 else mem_attention_split(qm, mem_k, mem_v, l)
        h = matmul_res([om.reshape(n, d)], [lw['wo_mem']], h)
        last = l == depth - 1
        fw = ffn_w[l]
        if fw['kind'] == 'dense':
            h = ffn_dense(h, lw['norm_ffn'], fw['w1'], fw['w3'], fw['w2'], norm_final, final_norm=last)
        else:
            h = moe_ffn(h, lw['norm_ffn'], fw['router'], fw['w1'], fw['w3'], fw['w2'], norm_final, final_norm=last)
        ks.append(k.reshape(b, t_len, ATT_HEADS, HEAD_DIM))
        vs.append(v.reshape(b, t_len, ATT_HEADS, HEAD_DIM))
        kis.append(ki[:, :, :IDX_DIM])
        convs.append(conv_s)
        pools.append(pool_s)
    st = lambda xs: jnp.stack(xs, axis=1)
    y = h.reshape(b, t_len, d)
    return y, st(ks), st(vs), st(kis), st(convs), st(pools), (st(cvs) if cvs else None)


def kernel(x_prompt, x_sample, mem_prompt, cache_k, cache_v, cache_kidx, cache_mem_k, cache_mem_v, state_conv, state_pool, page_table, norm_mix, w_in, conv_w, pool_w, pool_scale, chunk_norm, chunk_ws, chunk_b, w_out, norm_mem, wq_mem, wk_mem, wv_mem, wo_mem, norm_ffn, w1_dense, w3_dense, w2_dense, router, w1_moe, w3_moe, w2_moe, norm_final):
    p = {
        'norm_mix': norm_mix, 'w_in': w_in, 'conv_w': conv_w, 'pool_w': pool_w, 'pool_scale': pool_scale,
        'chunk_norm': chunk_norm, 'chunk_ws': chunk_ws, 'chunk_b': chunk_b, 'w_out': w_out, 'norm_mem': norm_mem,
        'wq_mem': wq_mem, 'wo_mem': wo_mem, 'norm_ffn': norm_ffn,
    }
    depth = norm_mix.shape[0]
    layers = [_prep_layer(l, p) for l in range(depth)]
    ffn_w = []
    for l in range(depth):
        j = l // 2
        if l % 2 == 0:
            ffn_w.append({'kind': 'dense', 'w1': w1_dense[j].astype(BF16), 'w3': w3_dense[j].astype(BF16),
                          'w2': w2_dense[j].astype(BF16)})
        else:
            router_pad = jnp.concatenate([router[j], jnp.zeros((D_MODEL, LANES - N_EXPERTS), F32)], axis=1)
            ffn_w.append({'kind': 'moe', 'router': router_pad, 'w1': w1_moe[j].astype(BF16),
                          'w3': w3_moe[j].astype(BF16), 'w2': w2_moe[j].astype(BF16)})

    n_pages, page = page_table.shape[1], cache_k.shape[2]
    past_len = n_pages * page
    n_phys = cache_k.shape[0]
    kt_pages = jnp.transpose(cache_k, (0, 1, 3, 4, 2)).reshape(n_phys, depth, W_GROUP, page)
    vt_pages = jnp.transpose(cache_v, (0, 1, 3, 4, 2)).reshape(n_phys, depth, W_GROUP, page)
    kit_pages = jnp.transpose(cache_kidx, (0, 1, 3, 2))
    past = (kt_pages, vt_pages, kit_pages, page_table)
    y_s, k_s, v_s, ki_s, conv_s, pool_s, cv_s = _run_trunk(
        x_sample, past_len, state_conv, state_pool, past, cache_mem_k, cache_mem_v, layers, ffn_w, norm_final)

    bp, tp, d = x_prompt.shape
    n_mem = mem_prompt.shape[1]
    w_kv = jnp.concatenate([wk_mem[l] for l in range(depth)] + [wv_mem[l] for l in range(depth)], axis=1).astype(BF16)
    mem_kv = norm_matmul(mem_prompt.reshape(bp * n_mem, d), jnp.ones((d,), F32), w_kv, norm=False)
    mem_kv = mem_kv.reshape(bp, n_mem, 2, depth, d)
    mem_k_p = jnp.moveaxis(mem_kv[:, :, 0], 2, 1)
    mem_v_p = jnp.moveaxis(mem_kv[:, :, 1], 2, 1)
    y_p, k_p, v_p, ki_p, conv_p, pool_p, _ = _run_trunk(
        x_prompt, 0, None, None, None, mem_k_p, mem_v_p, layers, ffn_w, norm_final)

    mem_shape = (bp, depth, n_mem, MEM_HEADS, MEM_HEAD_DIM)
    return (y_p, y_s, k_p, v_p, ki_p, mem_k_p.reshape(mem_shape), mem_v_p.reshape(mem_shape), conv_p, pool_p,
            k_s, v_s, ki_s, conv_s, pool_s, cv_s)
```

```python
import functools
import math

import jax
import jax.numpy as jnp
from jax import lax
from jax.experimental import pallas as pl
from jax.experimental.pallas import tpu as pltpu

F32 = jnp.float32
BF16 = jnp.bfloat16
I32 = jnp.int32

D_MODEL = 1024
W_GROUP = 256
CONV_W = 3
POOL_WINDOWS = (2, 4, 8, 16)
POOL_CH = 64
POOL_BUF = 15
CHUNK = 128
CHUNK_HEADS = 4
ATT_HEADS = 4
HEAD_DIM = 64
IDX_HEADS = 8
IDX_DIM = 64
IDX_W_SCALE = (IDX_HEADS ** -0.5) * (IDX_DIM ** -0.5)
TOPK_MAX = 256
ROPE_THETA = 10000.0
MEM_HEADS = 4
MEM_HEAD_DIM = 256
D_FF = 3584
N_EXPERTS = 8
EPS = 1e-6

LANES = 128
SUBLANES = 8
PREV_ROWS = 16
P_PAD = 3072
HALF = P_PAD // 2
OFF_Q, OFF_K, OFF_QI, OFF_V, OFF_KI, OFF_WI = 1536, 1792, 2048, 2560, 2816, 2944
VMEM_LIMIT = 56 * 1024 * 1024

INT_MIN = -2 ** 31
Q_SCALE = (HEAD_DIM ** -0.5) * math.log2(math.e)


def _cparams(sem):
    return pltpu.CompilerParams(dimension_semantics=sem, vmem_limit_bytes=VMEM_LIMIT)


def _rms(x, g):
    ms = jnp.mean(x * x, axis=-1, keepdims=True)
    return x * lax.rsqrt(ms + EPS) * g


def _dot(a, b):
    return jnp.dot(a, b, preferred_element_type=F32)


def _dot_nt(a, b):
    return lax.dot_general(a, b, (((1,), (1,)), ((), ())), preferred_element_type=F32)


def _norm_matmul_kernel(x_ref, g_ref, w_ref, o_ref, xn_ref, *, norm):
    @pl.when(pl.program_id(1) == 0)
    def _():
        x = x_ref[...]
        if norm:
            x = _rms(x, g_ref[...])
        xn_ref[...] = x.astype(BF16)

    o_ref[...] = _dot(xn_ref[...], w_ref[...]).astype(o_ref.dtype)


def norm_matmul(x, g, w, *, norm=True, out_dtype=F32, tm=1024, tn=1536):
    m, k = x.shape
    n = w.shape[1]
    tm = min(tm, m)
    tn = tn if n % tn == 0 else math.gcd(n, 1024)
    return pl.pallas_call(
        functools.partial(_norm_matmul_kernel, norm=norm),
        grid=(m // tm, n // tn),
        in_specs=[pl.BlockSpec((tm, k), lambda i, j: (i, 0)),
                  pl.BlockSpec((1, k), lambda i, j: (0, 0)),
                  pl.BlockSpec((k, tn), lambda i, j: (0, j))],
        out_specs=pl.BlockSpec((tm, tn), lambda i, j: (i, j)),
        out_shape=jax.ShapeDtypeStruct((m, n), out_dtype),
        scratch_shapes=[pltpu.VMEM((tm, k), BF16)],
        compiler_params=_cparams(("parallel", "arbitrary")),
        name="norm_matmul",
    )(x, g.reshape(1, k), w)


def _matmul_res_kernel(*refs, n_in):
    a_refs, w_refs = refs[:n_in], refs[n_in:2 * n_in]
    r_ref, o_ref = refs[2 * n_in], refs[2 * n_in + 1]
    acc = r_ref[...]
    for a, w in zip(a_refs, w_refs):
        acc = acc + _dot(a[...].astype(BF16), w[...])
    o_ref[...] = acc


def matmul_res(a_list, w_list, res, *, tm=1024):
    m, n = res.shape
    tm = min(tm, m)
    n_in = len(a_list)
    in_specs = [pl.BlockSpec((tm, a.shape[1]), lambda i: (i, 0)) for a in a_list]
    in_specs += [pl.BlockSpec(w.shape, lambda i: (0, 0)) for w in w_list]
    in_specs += [pl.BlockSpec((tm, n), lambda i: (i, 0))]
    return pl.pallas_call(
        functools.partial(_matmul_res_kernel, n_in=n_in),
        grid=(m // tm,),
        in_specs=in_specs,
        out_specs=pl.BlockSpec((tm, n), lambda i: (i, 0)),
        out_shape=jax.ShapeDtypeStruct((m, n), F32),
        compiler_params=_cparams(("parallel",)),
        name="matmul_res",
    )(*a_list, *w_list, res)


def _ffn_kernel(x_ref, g_ref, w1_ref, w3_ref, w2_ref, gf_ref, o_ref, xn_ref, acc_ref, *, final_norm):
    f = pl.program_id(1)

    @pl.when(f == 0)
    def _():
        xn_ref[...] = _rms(x_ref[...], g_ref[...]).astype(BF16)
        acc_ref[...] = jnp.zeros_like(acc_ref)

    xn = xn_ref[...]
    a = _dot(xn, w1_ref[...])
    b = _dot(xn, w3_ref[...])
    hidden = (a * jax.nn.sigmoid(a) * b).astype(BF16)
    acc_ref[...] += _dot(hidden, w2_ref[...])

    @pl.when(f == pl.num_programs(1) - 1)
    def _():
        y = x_ref[...] + acc_ref[...]
        if final_norm:
            y = _rms(y, gf_ref[...])
        o_ref[...] = y


def ffn_dense(x, g, w1, w3, w2, g_final, *, final_norm, tm=1024, tf=896):
    m, d = x.shape
    ff = w1.shape[1]
    tm = min(tm, m)
    return pl.pallas_call(
        functools.partial(_ffn_kernel, final_norm=final_norm),
        grid=(m // tm, ff // tf),
        in_specs=[pl.BlockSpec((tm, d), lambda i, f: (i, 0)),
                  pl.BlockSpec((1, d), lambda i, f: (0, 0)),
                  pl.BlockSpec((d, tf), lambda i, f: (0, f)),
                  pl.BlockSpec((d, tf), lambda i, f: (0, f)),
                  pl.BlockSpec((tf, d), lambda i, f: (f, 0)),
                  pl.BlockSpec((1, d), lambda i, f: (0, 0))],
        out_specs=pl.BlockSpec((tm, d), lambda i, f: (i, 0)),
        out_shape=jax.ShapeDtypeStruct((m, d), F32),
        scratch_shapes=[pltpu.VMEM((tm, d), BF16), pltpu.VMEM((tm, d), F32)],
        compiler_params=_cparams(("parallel", "arbitrary")),
        name="ffn_dense",
    )(x, g.reshape(1, d), w1, w3, w2, g_final.reshape(1, d))


def _router_kernel(x_ref, g_ref, r_ref, hn_ref, e_ref, gate_ref):
    hn = _rms(x_ref[...], g_ref[...])
    hn_ref[...] = hn.astype(BF16)
    logits = jnp.dot(hn, r_ref[...], preferred_element_type=F32, precision=lax.Precision.HIGHEST)
    lane = lax.broadcasted_iota(I32, logits.shape, 1)
    neg = jnp.float32(-jnp.inf)
    l1 = jnp.where(lane < N_EXPERTS, logits, neg)
    m1 = jnp.max(l1, axis=1, keepdims=True)
    i1 = jnp.min(jnp.where(l1 == m1, lane, LANES), axis=1, keepdims=True)
    l2 = jnp.where(lane == i1, neg, l1)
    m2 = jnp.max(l2, axis=1, keepdims=True)
    i2 = jnp.min(jnp.where(l2 == m2, lane, LANES), axis=1, keepdims=True)
    t = jnp.exp(m2 - m1)
    denom = 1.0 + t
    e_ref[...] = jnp.where(lane == 0, i1, jnp.where(lane == 1, i2, 0))
    gate_ref[...] = jnp.where(lane == 0, 1.0 / denom, jnp.where(lane == 1, t / denom, 0.0))


def moe_router(x, g, router_pad, *, tm=512):
    m, d = x.shape
    tm = min(tm, m)
    return pl.pallas_call(
        _router_kernel,
        grid=(m // tm,),
        in_specs=[pl.BlockSpec((tm, d), lambda i: (i, 0)),
                  pl.BlockSpec((1, d), lambda i: (0, 0)),
                  pl.BlockSpec((d, LANES), lambda i: (0, 0))],
        out_specs=[pl.BlockSpec((tm, d), lambda i: (i, 0)),
                   pl.BlockSpec((tm, LANES), lambda i: (i, 0)),
                   pl.BlockSpec((tm, LANES), lambda i: (i, 0))],
        out_shape=[jax.ShapeDtypeStruct((m, d), BF16),
                   jax.ShapeDtypeStruct((m, LANES), I32),
                   jax.ShapeDtypeStruct((m, LANES), F32)],
        compiler_params=_cparams(("parallel",)),
        name="moe_router",
    )(x, g.reshape(1, d), router_pad)


def _expert_kernel(be_ref, nb_ref, xs_ref, gate_ref, w1_ref, w3_ref, w2_ref, o_ref, acc_ref):
    j, f = pl.program_id(0), pl.program_id(1)
    live = j < nb_ref[0]

    @pl.when(f == 0)
    def _():
        acc_ref[...] = jnp.zeros_like(acc_ref)

    @pl.when(live)
    def _():
        xs = xs_ref[...]
        a = _dot(xs, w1_ref[0])
        b = _dot(xs, w3_ref[0])
        hidden = (a * jax.nn.sigmoid(a) * b).astype(BF16)
        acc_ref[...] += _dot(hidden, w2_ref[0])

    @pl.when(f == pl.num_programs(1) - 1)
    def _():
        o_ref[...] = acc_ref[...] * gate_ref[...]


def moe_experts(block_e, n_live, xs, slot_gate, w1, w3, w2, *, bm, tf=896):
    n_slots, d = xs.shape
    ff = w1.shape[2]
    n_blocks = n_slots // bm
    grid_spec = pltpu.PrefetchScalarGridSpec(
        num_scalar_prefetch=2,
        grid=(n_blocks, ff // tf),
        in_specs=[pl.BlockSpec((bm, d), lambda j, f, be, nb: (j, 0)),
                  pl.BlockSpec((bm, 1), lambda j, f, be, nb: (j, 0)),
                  pl.BlockSpec((1, d, tf), lambda j, f, be, nb: (be[j], 0, f)),
                  pl.BlockSpec((1, d, tf), lambda j, f, be, nb: (be[j], 0, f)),
                  pl.BlockSpec((1, tf, d), lambda j, f, be, nb: (be[j], f, 0))],
        out_specs=pl.BlockSpec((bm, d), lambda j, f, be, nb: (j, 0)),
        scratch_shapes=[pltpu.VMEM((bm, d), F32)],
    )
    return pl.pallas_call(
        _expert_kernel,
        grid_spec=grid_spec,
        out_shape=jax.ShapeDtypeStruct((n_slots, d), F32),
        compiler_params=_cparams(("parallel", "arbitrary")),
        name="moe_experts",
    )(block_e, n_live, xs, slot_gate, w1, w3, w2)


def _combine_kernel(x_ref, a_ref, b_ref, gf_ref, o_ref, *, final_norm):
    y = x_ref[...] + (a_ref[...] + b_ref[...])
    if final_norm:
        y = _rms(y, gf_ref[...])
    o_ref[...] = y


def moe_combine(x, ya, yb, g_final, *, final_norm, tm=512):
    m, d = x.shape
    tm = min(tm, m)
    spec = pl.BlockSpec((tm, d), lambda i: (i, 0))
    return pl.pallas_call(
        functools.partial(_combine_kernel, final_norm=final_norm),
        grid=(m // tm,),
        in_specs=[spec, spec, spec, pl.BlockSpec((1, d), lambda i: (0, 0))],
        out_specs=spec,
        out_shape=jax.ShapeDtypeStruct((m, d), F32),
        compiler_params=_cparams(("parallel",)),
        name="moe_combine",
    )(x, ya, yb, g_final.reshape(1, d))


def _compact_kernel(ib_ref, ic_ref, first_ref, total_ref, tok_ref, hn_ref, o_ref, acc_ref, *, tc):
    i = pl.program_id(0)

    @pl.when(first_ref[i] == 1)
    def _():
        acc_ref[...] = jnp.zeros_like(acc_ref)

    @pl.when(i < total_ref[0])
    def _():
        local = tok_ref[...] - ic_ref[i] * tc
        lane = lax.broadcasted_iota(I32, (tok_ref.shape[0], tc), 1)
        onehot = jnp.where(local == lane, 1.0, 0.0).astype(BF16)
        acc_ref[...] += _dot(onehot, hn_ref[...])

    o_ref[...] = acc_ref[...].astype(o_ref.dtype)


def moe_compact(item_block, item_chunk, item_first, n_items, slot_tok, hn, *, bm, tc):
    n_slots = slot_tok.shape[0]
    n, d = hn.shape
    n_max = item_block.shape[0]
    grid_spec = pltpu.PrefetchScalarGridSpec(
        num_scalar_prefetch=4,
        grid=(n_max,),
        in_specs=[pl.BlockSpec((bm, 1), lambda i, ib, ic, fr, tot: (ib[i], 0)),
                  pl.BlockSpec((tc, d), lambda i, ib, ic, fr, tot: (ic[i], 0))],
        out_specs=pl.BlockSpec((bm, d), lambda i, ib, ic, fr, tot: (ib[i], 0)),
        scratch_shapes=[pltpu.VMEM((bm, d), F32)],
    )
    return pl.pallas_call(
        functools.partial(_compact_kernel, tc=tc),
        grid_spec=grid_spec,
        out_shape=jax.ShapeDtypeStruct((n_slots, d), BF16),
        compiler_params=_cparams(("arbitrary",)),
        name="moe_compact",
    )(item_block, item_chunk, item_first, n_items, slot_tok.reshape(n_slots, 1), hn)


def moe_ffn(x, g, router_pad, w1, w3, w2, g_final, *, final_norm):
    n, d = x.shape
    bm = 512 if n >= 4096 else 128
    tc = min(n, 512)
    hn, e12, g12 = moe_router(x, g, router_pad)
    e_flat = e12[:, :2].reshape(-1)
    g_flat = g12[:, :2].reshape(-1)
    n_assign = 2 * n
    order = jnp.argsort(e_flat, stable=True).astype(I32)
    rank = jnp.argsort(order).astype(I32)
    counts = jnp.sum(e_flat[:, None] == jnp.arange(N_EXPERTS, dtype=I32)[None, :], axis=0).astype(I32)
    padded = (counts + bm - 1) // bm * bm
    start = jnp.cumsum(counts) - counts
    pend = jnp.cumsum(padded)
    pstart = pend - padded
    n_blocks = -(-n_assign // bm) + N_EXPERTS
    n_slots = n_blocks * bm
    block_start = jnp.arange(n_blocks, dtype=I32) * bm
    block_e = jnp.minimum(jnp.sum(pend[None, :] <= block_start[:, None], axis=1), N_EXPERTS - 1).astype(I32)
    n_live = (pend[-1] // bm).astype(I32).reshape(1)
    slot_e = jnp.repeat(block_e, bm)
    slot_idx = jnp.arange(n_slots, dtype=I32) - pstart[slot_e]
    slot_live = slot_idx < counts[slot_e]
    slot_src = order[jnp.clip(start[slot_e] + slot_idx, 0, n_assign - 1)]
    slot_tok = jnp.where(slot_live, slot_src // 2, -1)
    slot_gate = jnp.where(slot_live, g_flat[slot_src], 0.0)
    assign_slot = pstart[e_flat] + rank - start[e_flat]
    tok_b = slot_tok.reshape(n_blocks, bm)
    c_lo = jnp.maximum(tok_b[:, 0], 0) // tc
    c_hi = jnp.maximum(jnp.max(tok_b, axis=1), 0) // tc
    n_items_b = c_hi - c_lo + 1
    item_end = jnp.cumsum(n_items_b)
    total = item_end[-1]
    n_items_max = n_blocks + N_EXPERTS * (n // tc)
    item_id = jnp.arange(n_items_max, dtype=I32)
    item = jnp.minimum(item_id, total - 1)
    item_block = jnp.sum(item_end[None, :] <= item[:, None], axis=1).astype(I32)
    item_off = item - (item_end - n_items_b)[item_block]
    item_chunk = (c_lo[item_block] + item_off).astype(I32)
    item_first = ((item_off == 0) & (item_id < total)).astype(I32)
    xs = moe_compact(item_block, item_chunk, item_first, total.astype(I32).reshape(1), slot_tok, hn, bm=bm, tc=tc)
    ys = moe_experts(block_e, n_live, xs, slot_gate.reshape(n_slots, 1), w1, w3, w2, bm=bm)
    pair = assign_slot.reshape(n, 2)
    return moe_combine(x, ys[pair[:, 0]], ys[pair[:, 1]], g_final, final_norm=final_norm)


def _swap_half_heads(x):
    n = x.shape[-1]
    lane = lax.broadcasted_iota(I32, x.shape, x.ndim - 1)
    fwd = pltpu.roll(x, n - HEAD_DIM // 2, x.ndim - 1)
    bwd = pltpu.roll(x, HEAD_DIM // 2, x.ndim - 1)
    return jnp.where(lane % HEAD_DIM < HEAD_DIM // 2, fwd, bwd)


def _rope(x, cos, sin):
    reps = x.shape[-1] // LANES
    if reps > 1:
        cos = jnp.concatenate([cos] * reps, axis=-1)
        sin = jnp.concatenate([sin] * reps, axis=-1)
    return x * cos + _swap_half_heads(x) * sin


def _mixer_kernel(*refs, tt, chunk, pos0, from_state, emit_cv):
    it = iter(refs)
    xa_ref, xb_ref = next(it), next(it)
    if from_state:
        convp_ref, poolp_ref = next(it), next(it)
    else:
        prev_ref = next(it)
    cos_ref, sin_ref = next(it), next(it)
    convw_ref, poolw_ref, pscale_ref, cnorm_ref, wm_ref, cbias_ref = (next(it) for _ in range(6))
    mix_ref, q_ref, k_ref, kb_ref, v_ref, vb_ref, qi_ref, ki_ref, ki2_ref, wi_ref, convs_ref, pools_ref = (
        next(it) for _ in range(12))
    cv_ref = next(it) if emit_cv else None
    ybuf, xbuf = next(it), next(it)

    t = pl.program_id(1)
    xa = xa_ref[0]
    a_in, a_b, a_c = xa[:, 0:256], xa[:, 256:512], xa[:, 512:768]
    b_in, c_u, c_v = xa[:, 768:1024], xa[:, 1024:1280], xa[:, 1280:1536]

    y = a_c * a_in
    if from_state:
        ybuf[PREV_ROWS - 2:PREV_ROWS, :] = convp_ref[0]
        xbuf[0:1, :] = jnp.zeros((1, W_GROUP), F32)
        xbuf[1:PREV_ROWS, :] = poolp_ref[0]
    else:
        first = (t == 0).astype(F32)
        prev = prev_ref[0] * (1.0 - first)
        ybuf[0:PREV_ROWS, :] = prev[:, 512:768] * prev[:, 0:256]
        xbuf[0:PREV_ROWS, :] = prev[:, 768:1024]
    ybuf[PREV_ROWS:PREV_ROWS + tt, :] = y
    xbuf[PREV_ROWS:PREV_ROWS + tt, :] = b_in

    cw = convw_ref[...]
    z = cw[0:1, :] * ybuf[PREV_ROWS - 2:PREV_ROWS - 2 + tt, :]
    z = z + cw[1:2, :] * ybuf[PREV_ROWS - 1:PREV_ROWS - 1 + tt, :]
    z = z + cw[2:3, :] * y
    ya = a_b * z
    convs_ref[0] = ybuf[PREV_ROWS + tt - 2:PREV_ROWS + tt, :]

    lane = lax.broadcasted_iota(I32, (tt, W_GROUP), 1)
    row = lax.broadcasted_iota(I32, (tt, W_GROUP), 0)
    pos = (pos0 + t * tt + row).astype(F32)
    run = b_in
    mean = jnp.zeros((tt, W_GROUP), F32)
    for j in range(1, POOL_WINDOWS[-1] + 1):
        if j > 1:
            run = run + xbuf[PREV_ROWS - (j - 1):PREV_ROWS - (j - 1) + tt, :]
        if j in POOL_WINDOWS:
            grp = POOL_WINDOWS.index(j)
            cnt = jnp.minimum(pos + 1.0, float(j))
            mean = jnp.where(lane // POOL_CH == grp, run / cnt, mean)
    dlt = (mean - b_in).astype(BF16)
    yb = _dot(dlt, poolw_ref[...]) * pscale_ref[...]
    pools_ref[0] = xbuf[PREV_ROWS + tt - POOL_BUF:PREV_ROWS + tt, :]

    vn = _rms(c_v, cnorm_ref[...])
    if emit_cv:
        cv_ref[0] = vn
    vnb = vn.astype(BF16)
    lane_c = lax.broadcasted_iota(I32, (chunk, W_GROUP), 1)
    parts = []
    for c in range(tt // chunk):
        vc = vnb[c * chunk:(c + 1) * chunk, :]
        s = cbias_ref[...]
        for hd in range(CHUNK_HEADS):
            sh = _dot(wm_ref[hd], vc)
            s = s + jnp.where(lane_c // (W_GROUP // CHUNK_HEADS) == hd, sh, 0.0)
        parts.append(s)
    s_all = parts[0] if len(parts) == 1 else jnp.concatenate(parts, axis=0)
    yc = c_u * s_all

    mix_ref[0] = jnp.concatenate([ya, yb, yc], axis=-1).astype(BF16)

    xb = xb_ref[0]
    cos, sin = cos_ref[...], sin_ref[...]
    q = _rope(xb[:, 0:256], cos, sin)
    k = _rope(xb[:, 256:512], cos, sin)
    qi = _rope(xb[:, 512:1024], cos, sin)
    v = xb[:, 1024:1280]
    ki = _rope(xb[:, 1280:1408], cos, sin)
    wi = xb[:, 1408:1536] * IDX_W_SCALE
    q_ref[0] = (q * Q_SCALE).astype(BF16)
    k_ref[0] = k
    kb_ref[0] = k.astype(BF16)
    v_ref[0] = v
    if from_state:
        vb_ref[0] = v.astype(BF16)
    else:
        vb_ref[0, 0] = v.T.astype(BF16)
    qi_ref[0] = qi.astype(BF16)
    ki_ref[0] = ki
    lane128 = lax.broadcasted_iota(I32, ki.shape, 1)
    ki2_ref[0] = jnp.where(lane128 < IDX_DIM, ki, pltpu.roll(ki, IDX_DIM, 1)).astype(BF16)
    wi_ref[0] = wi


def mixers(proj, conv_prev, pool_prev, cos, sin, lw, *, pos0, from_state, emit_cv):
    b, t_len, _ = proj.shape
    tt = min(t_len, 256)
    chunk = min(t_len, CHUNK)
    nt = t_len // tt
    in_specs = [pl.BlockSpec((1, tt, HALF), lambda i, t: (i, t, 0)),
                pl.BlockSpec((1, tt, HALF), lambda i, t: (i, t, 1))]
    args = [proj, proj]
    if from_state:
        in_specs += [pl.BlockSpec((1, CONV_W - 1, W_GROUP), lambda i, t: (i, 0, 0)),
                     pl.BlockSpec((1, POOL_BUF, W_GROUP), lambda i, t: (i, 0, 0))]
        args += [conv_prev, pool_prev]
    else:
        per = tt // PREV_ROWS
        in_specs += [pl.BlockSpec((1, PREV_ROWS, HALF), lambda i, t: (i, jnp.maximum(t * per - 1, 0), 0))]
        args += [proj]
    in_specs += [pl.BlockSpec((tt, LANES), lambda i, t: (t, 0)),
                 pl.BlockSpec((tt, LANES), lambda i, t: (t, 0))]
    args += [cos, sin]
    consts = [lw['conv_w'], lw['pool_w_bd'], lw['pool_scale'], lw['chunk_norm'], lw['chunk_wm'], lw['chunk_bias']]
    for c in consts:
        in_specs.append(pl.BlockSpec(c.shape, lambda i, t, nd=c.ndim: (0,) * nd))
    args += consts

    def tok(width, dtype):
        return (pl.BlockSpec((1, tt, width), lambda i, t: (i, t, 0)), jax.ShapeDtypeStruct((b, t_len, width), dtype))

    outs = [tok(768, BF16),
            tok(256, BF16),
            tok(256, F32), tok(256, BF16),
            tok(256, F32),
            tok(256, BF16) if from_state else
            (pl.BlockSpec((1, 1, W_GROUP, tt), lambda i, t: (i, t, 0, 0)),
             jax.ShapeDtypeStruct((b, nt, W_GROUP, tt), BF16)),
            tok(512, BF16),
            tok(LANES, F32),
            tok(LANES, BF16),
            tok(LANES, F32),
            (pl.BlockSpec((1, CONV_W - 1, W_GROUP), lambda i, t: (i, 0, 0)),
             jax.ShapeDtypeStruct((b, CONV_W - 1, W_GROUP), F32)),
            (pl.BlockSpec((1, POOL_BUF, W_GROUP), lambda i, t: (i, 0, 0)),
             jax.ShapeDtypeStruct((b, POOL_BUF, W_GROUP), F32))]
    if emit_cv:
        outs.append(tok(256, F32))
    return pl.pallas_call(
        functools.partial(_mixer_kernel, tt=tt, chunk=chunk, pos0=pos0, from_state=from_state, emit_cv=emit_cv),
        grid=(b, nt),
        in_specs=in_specs,
        out_specs=[o[0] for o in outs],
        out_shape=[o[1] for o in outs],
        scratch_shapes=[pltpu.VMEM((PREV_ROWS + tt, W_GROUP), F32), pltpu.VMEM((PREV_ROWS + tt, W_GROUP), F32)],
        compiler_params=_cparams(("parallel", "arbitrary")),
        name="mixers",
    )(*args)


def _ordered_f32(key):
    return pltpu.bitcast(key ^ ((key >> 31) & jnp.int32(0x7FFFFFFF)), F32)


def _count_rows(score_ref, n_tiles, pred):
    rows = score_ref.shape[1]
    lane = lax.broadcasted_iota(I32, (rows, LANES), 1)
    acc = jnp.zeros((rows, LANES), F32)
    for j in range(n_tiles):
        acc = acc + jnp.where(pred(score_ref[j], j * LANES + lane), 1.0, 0.0)
    return jnp.sum(acc, axis=1, keepdims=True)


def _count_cols(score_ref, n_groups, gsz, pred):
    cols = score_ref.shape[1]
    pos0 = lax.broadcasted_iota(I32, (gsz, cols), 0)

    def body(c, acc):
        blk = score_ref[pl.ds(pl.multiple_of(c * gsz, gsz), gsz), :]
        hit = jnp.where(pred(blk, c * gsz + pos0), 1.0, 0.0)
        return acc + jnp.sum(hit.reshape(gsz // SUBLANES, SUBLANES, cols), axis=0)

    acc = lax.fori_loop(0, n_groups, body, jnp.zeros((SUBLANES, cols), F32))
    return jnp.sum(acc, axis=0, keepdims=True)


def _kth_largest(count, topk, two_bits_per_trip):
    def enough(key):
        cand = _ordered_f32(key)
        return count(lambda s, pos: s >= cand) >= topk

    key = jnp.where(count(lambda s, pos: s >= 0.0) >= topk, 0, INT_MIN).astype(I32)
    if two_bits_per_trip:
        top = key | jnp.int32(1 << 30)
        key = jnp.where(enough(top), top, key)

        def two_bits(i, key):
            lo = 28 - 2 * i
            c1, c2, c3 = (key | (jnp.int32(v) << lo) for v in (1, 2, 3))
            return jnp.where(enough(c3), c3, jnp.where(enough(c2), c2, jnp.where(enough(c1), c1, key)))

        key = lax.fori_loop(0, 15, two_bits, key)
    else:
        def one_bit(i, key):
            cand = key | (jnp.int32(1) << (30 - i))
            return jnp.where(enough(cand), cand, key)

        key = lax.fori_loop(0, 31, one_bit, key)
    n_valid = count(lambda s, pos: s > -jnp.inf)
    return jnp.where(n_valid < topk, -jnp.inf, _ordered_f32(key))


def _tie_bound(count, thr, shape, topk, n_keys):
    need = topk - count(lambda s, pos: s > thr)
    n_eq = count(lambda s, pos: s == thr)
    idx_bits = n_keys.bit_length()
    excess = jnp.max(jnp.where((n_eq > need) & (thr > -jnp.inf), 1.0, 0.0))

    def tie_bound():
        def bound_bit(i, bound):
            cand = bound | (jnp.int32(1) << (idx_bits - 1 - i))
            hits = count(lambda s, pos: (s == thr) & (pos < cand))
            return jnp.where(hits <= need, cand, bound)

        return lax.fori_loop(0, idx_bits, bound_bit, jnp.zeros(shape, I32))

    return lax.cond(excess > 0.0, tie_bound, lambda: jnp.full(shape, 2 ** idx_bits - 1, I32))


def _dsa_prompt_kernel(q_ref, qi_ref, wi_ref, k_ref, vt_ref, ki2_ref, o_ref, score_ref, *scratch,
                       tq, kc1, kc, topk, n_keys):
    acc_refs, logit_refs = scratch[:ATT_HEADS], scratch[ATT_HEADS:]
    qb = pl.program_id(1)
    n_chunks = (qb * tq + tq + kc - 1) // kc
    neg_inf = jnp.float32(-jnp.inf)
    q_pos = qb * tq + lax.broadcasted_iota(I32, (kc, tq), 1)
    pos0 = lax.broadcasted_iota(I32, (kc, tq), 0)
    sub = lax.broadcasted_iota(I32, (LANES, tq), 0)

    q_t = q_ref[0].astype(F32).T
    qi_t = qi_ref[0].astype(F32).T
    w_t = wi_ref[0].T

    def head_tile(x_t, h, dim):
        tile = x_t[(h // 2) * LANES:(h // 2 + 1) * LANES]
        return jnp.where((sub // dim) == (h % 2), tile, 0.0).astype(BF16)

    qim = [head_tile(qi_t, h, IDX_DIM) for h in range(IDX_HEADS)]
    qhm = [head_tile(q_t, h, HEAD_DIM) for h in range(ATT_HEADS)]

    q_pos1 = qb * tq + lax.broadcasted_iota(I32, (kc1, tq), 1)
    pos1 = lax.broadcasted_iota(I32, (kc1, tq), 0)

    def score_chunk(c, carry):
        off = pl.multiple_of(c * kc1, kc1)
        ki2 = ki2_ref[0, pl.ds(off, kc1), :]
        acc = jnp.zeros((kc1, tq), F32)
        for h in range(IDX_HEADS):
            acc = acc + w_t[h:h + 1] * jnp.maximum(_dot(ki2, qim[h]), 0.0)
        score_ref[pl.ds(off, kc1), :] = jnp.where(c * kc1 + pos1 <= q_pos1, acc, neg_inf)
        return carry

    lax.fori_loop(0, (qb * tq + tq + kc1 - 1) // kc1, score_chunk, 0)

    count = functools.partial(_count_cols, score_ref, n_chunks, kc)
    thr = _kth_largest(count, topk, two_bits_per_trip=False)
    bound = _tie_bound(count, thr, (1, tq), topk, n_keys)

    for acc_ref in acc_refs:
        acc_ref[...] = jnp.zeros_like(acc_ref)

    logit_a, logit_b = logit_refs[:ATT_HEADS], logit_refs[ATT_HEADS:]
    last = n_chunks - 1

    def score_matmuls(c, dst):
        off = pl.multiple_of(c * kc, kc)
        for h in range(ATT_HEADS):
            dst[h][...] = _dot(k_ref[0, pl.ds(off, kc), (h // 2) * LANES:(h // 2 + 1) * LANES], qhm[h])

    def softmax_pv(c, live, src, m_run, l_run):
        pos = c * kc + jnp.where(live, 0, n_keys) + pos0
        score = score_ref[pl.ds(pl.multiple_of(c * kc, kc), kc), :]
        sel = ((score > thr) | ((score == thr) & (pos < bound))) & (pos <= q_pos)
        m_new, l_new, probs, alphas = [], [], [], []
        for h in range(ATT_HEADS):
            logit = jnp.where(sel, src[h][...], neg_inf)
            m_h = jnp.maximum(m_run[h], jnp.max(logit, axis=0, keepdims=True))
            m_safe = jnp.where(m_h == neg_inf, 0.0, m_h)
            p = jnp.exp2(logit - m_safe)
            alpha = jnp.exp2(m_run[h] - m_safe)
            l_new.append(alpha * l_run[h] + jnp.sum(p, axis=0, keepdims=True))
            m_new.append(m_h)
            probs.append(p.astype(BF16))
            alphas.append(alpha)
        for h in range(ATT_HEADS):
            pv = _dot(vt_ref[0, c, h * HEAD_DIM:(h + 1) * HEAD_DIM, :], probs[h])
            acc_refs[h][...] = acc_refs[h][...] * alphas[h] + pv
        return tuple(m_new), tuple(l_new)

    def attend_pair(i, carry):
        m_run, l_run = carry
        c0 = 2 * i
        c1 = jnp.minimum(c0 + 1, last)
        score_matmuls(c1, logit_b)
        m_run, l_run = softmax_pv(c0, True, logit_a, m_run, l_run)
        score_matmuls(jnp.minimum(c0 + 2, last), logit_a)
        return softmax_pv(c1, c0 + 1 <= last, logit_b, m_run, l_run)

    init = (tuple(jnp.full((1, tq), neg_inf, F32) for _ in range(ATT_HEADS)),
            tuple(jnp.zeros((1, tq), F32) for _ in range(ATT_HEADS)))
    score_matmuls(0, logit_a)
    _, l_fin = lax.fori_loop(0, (n_chunks + 1) // 2, attend_pair, init)
    out_t = jnp.concatenate([acc_refs[h][...] * (1.0 / l_fin[h]) for h in range(ATT_HEADS)], axis=0)
    o_ref[0] = out_t.T.astype(o_ref.dtype)


def dsa_prompt(q, qi, wi, k, vt, ki2):
    b, t_len, _ = q.shape
    kc = vt.shape[-1]
    tq = kc
    kc1 = math.gcd(t_len, 2 * kc)
    topk = min(TOPK_MAX, t_len // 4)
    return pl.pallas_call(
        functools.partial(_dsa_prompt_kernel, tq=tq, kc1=kc1, kc=kc, topk=topk, n_keys=t_len),
        grid=(b, t_len // tq),
        in_specs=[pl.BlockSpec((1, tq, W_GROUP), lambda i, j: (i, j, 0)),
                  pl.BlockSpec((1, tq, 2 * W_GROUP), lambda i, j: (i, j, 0)),
                  pl.BlockSpec((1, tq, LANES), lambda i, j: (i, j, 0)),
                  pl.BlockSpec((1, t_len, W_GROUP), lambda i, j: (i, 0, 0)),
                  pl.BlockSpec((1, t_len // kc, W_GROUP, kc), lambda i, j: (i, 0, 0, 0)),
                  pl.BlockSpec((1, t_len, LANES), lambda i, j: (i, 0, 0))],
        out_specs=pl.BlockSpec((1, tq, W_GROUP), lambda i, j: (i, j, 0)),
        out_shape=jax.ShapeDtypeStruct((b, t_len, W_GROUP), BF16),
        scratch_shapes=[pltpu.VMEM((t_len, tq), F32)] + [pltpu.VMEM((HEAD_DIM, tq), F32)] * ATT_HEADS + [pltpu.VMEM((kc, tq), F32)] * (2 * ATT_HEADS),
        compiler_params=_cparams(("parallel", "arbitrary")),
        name="dsa_prompt",
    )(q, qi, wi, k, vt, ki2)


def _dsa_sample_kernel(*refs, g, t, n_pages, page, topk):
    pt_ref, q_ref, qi_ref, wi_ref, kn_ref, vn_ref, kin_ref = refs[:7]
    n_pg = g * n_pages
    kt_refs, vt_refs, kit_refs = refs[7:7 + n_pg], refs[7 + n_pg:7 + 2 * n_pg], refs[7 + 2 * n_pg:7 + 3 * n_pg]
    o_ref, score_ref = refs[7 + 3 * n_pg], refs[8 + 3 * n_pg]
    del pt_ref
    past = n_pages * page
    s_len = past + LANES
    n_tiles = s_len // LANES
    lane = lax.broadcasted_iota(I32, (t, LANES), 1)
    lane_o = lax.broadcasted_iota(I32, (t, W_GROUP), 1)
    col = lax.broadcasted_iota(I32, (t, s_len), 1)
    causal = col <= past + lax.broadcasted_iota(I32, (t, s_len), 0)
    neg_inf = jnp.float32(-jnp.inf)

    def pad_rows(x):
        return jnp.concatenate([x.astype(F32), jnp.zeros((LANES - t, x.shape[1]), F32)], axis=0).astype(BF16)

    def pages(page_refs, b):
        return jnp.concatenate([page_refs[b * n_pages + p][0, 0] for p in range(n_pages)], axis=1).astype(BF16)

    all_dots = []
    for b in range(g):
        qi = qi_ref[b].astype(F32)
        lhs = jnp.concatenate(
            [jnp.where((lane // IDX_DIM) == (h % 2), qi[:, (h // 2) * LANES:(h // 2 + 1) * LANES], 0.0)
             for h in range(IDX_HEADS)], axis=0).astype(BF16)
        kit = pages(kit_refs, b)
        all_dots.append(jnp.concatenate([_dot(lhs, jnp.concatenate([kit, kit], axis=0)),
                                         _dot_nt(lhs, pad_rows(kin_ref[b]))], axis=1))
    for b in range(g):
        dots, wi = all_dots[b], wi_ref[b]
        acc = jnp.zeros((t, s_len), F32)
        for h in range(IDX_HEADS):
            acc = acc + wi[:, h:h + 1] * jnp.maximum(dots[h * t:(h + 1) * t], 0.0)
        score = jnp.where(causal, acc, neg_inf)
        for j in range(n_tiles):
            score_ref[j, b * t:(b + 1) * t, :] = score[:, j * LANES:(j + 1) * LANES]

    count = functools.partial(_count_rows, score_ref, n_tiles)
    thr = _kth_largest(count, topk, two_bits_per_trip=True)
    bound = _tie_bound(count, thr, (g * t, 1), topk, s_len)

    all_logits = []
    for b in range(g):
        q = q_ref[b].astype(F32)
        lhs = jnp.concatenate([jnp.where((lane_o // HEAD_DIM) == h, q, 0.0) for h in range(ATT_HEADS)],
                              axis=0).astype(BF16)
        all_logits.append(
            jnp.concatenate([_dot(lhs, pages(kt_refs, b)), _dot_nt(lhs, pad_rows(kn_ref[b]))], axis=1))
    all_probs, all_sums = [], []
    for b in range(g):
        rows = slice(b * t, (b + 1) * t)
        score = jnp.concatenate([score_ref[j, rows, :] for j in range(n_tiles)], axis=1)
        sel = ((score > thr[rows]) | ((score == thr[rows]) & (col < bound[rows]))) & causal
        logit = jnp.where(jnp.concatenate([sel] * ATT_HEADS, axis=0), all_logits[b], neg_inf)
        p = jnp.exp2(logit - jnp.max(logit, axis=1, keepdims=True))
        all_probs.append(p.astype(BF16))
        all_sums.append(jnp.sum(p, axis=1, keepdims=True))
    for b in range(g):
        pb = all_probs[b]
        res = _dot_nt(pb[:, :past], pages(vt_refs, b)) + _dot(pb[:, past:], pad_rows(vn_ref[b]))
        res = res / all_sums[b]
        out = jnp.zeros((t, W_GROUP), F32)
        for h in range(ATT_HEADS):
            out = jnp.where((lane_o // HEAD_DIM) == h, res[h * t:(h + 1) * t], out)
        o_ref[b] = out.astype(o_ref.dtype)


def dsa_sample(q, qi, wi, k_new, v_new, ki2_new, kt_pages, vt_pages, kit_pages, page_table, layer, *, g=4):
    b, t_len, _ = q.shape
    n_pages = page_table.shape[1]
    page = kt_pages.shape[-1]
    g = math.gcd(g, b)
    topk = min(TOPK_MAX, (n_pages * page + t_len) // 4)

    def spec(width):
        return pl.BlockSpec((g, t_len, width), lambda i, pt: (i, 0, 0))

    def page_specs(rows):
        return [pl.BlockSpec((1, 1, rows, page),
                             lambda i, pt, s=s, p=p: (pt[(i * g + s) * n_pages + p], layer, 0, 0))
                for s in range(g) for p in range(n_pages)]

    grid_spec = pltpu.PrefetchScalarGridSpec(
        num_scalar_prefetch=1,
        grid=(b // g,),
        in_specs=[spec(W_GROUP), spec(2 * W_GROUP), spec(LANES), spec(W_GROUP), spec(W_GROUP), spec(LANES)]
        + page_specs(W_GROUP) + page_specs(W_GROUP) + page_specs(IDX_DIM),
        out_specs=spec(W_GROUP),
        scratch_shapes=[pltpu.VMEM((n_pages + 1, g * t_len, LANES), F32)],
    )
    n_pg = g * n_pages
    return pl.pallas_call(
        functools.partial(_dsa_sample_kernel, g=g, t=t_len, n_pages=n_pages, page=page, topk=topk),
        grid_spec=grid_spec,
        out_shape=jax.ShapeDtypeStruct((b, t_len, W_GROUP), BF16),
        compiler_params=_cparams(("parallel",)),
        name="dsa_sample",
    )(page_table.reshape(-1), q, qi, wi, k_new, v_new, ki2_new,
      *([kt_pages] * n_pg), *([vt_pages] * n_pg), *([kit_pages] * n_pg))


def _mem_attn_kernel(q_ref, mk_ref, mv_ref, o_ref, *, g):
    for b in range(g):
        q = q_ref[b]
        outs = []
        for h in range(MEM_HEADS):
            sl = slice(h * MEM_HEAD_DIM, (h + 1) * MEM_HEAD_DIM)
            logit = _dot_nt(q[:, sl], mk_ref[b, 0, :, sl].astype(BF16)) * (MEM_HEAD_DIM ** -0.5)
            m = jnp.max(logit, axis=1, keepdims=True)
            p = jnp.exp(logit - m)
            p = p / jnp.sum(p, axis=1, keepdims=True)
            outs.append(_dot(p.astype(BF16), mv_ref[b, 0, :, sl].astype(BF16)))
        o_ref[b] = jnp.concatenate(outs, axis=-1).astype(o_ref.dtype)


def mem_attention(q, mk, mv, layer, *, tq=512, g=1):
    b, t_len, d = q.shape
    m_len = mk.shape[2]
    if t_len >= tq:
        g = 1
    else:
        tq, g = t_len, math.gcd(g, b)
    mem_spec = pl.BlockSpec((g, 1, m_len, d), lambda i, j: (i, layer, 0, 0))
    return pl.pallas_call(
        functools.partial(_mem_attn_kernel, g=g),
        grid=(b // g, t_len // tq),
        in_specs=[pl.BlockSpec((g, tq, d), lambda i, j: (i, j, 0)), mem_spec, mem_spec],
        out_specs=pl.BlockSpec((g, tq, d), lambda i, j: (i, j, 0)),
        out_shape=jax.ShapeDtypeStruct((b, t_len, d), BF16),
        compiler_params=_cparams(("parallel", "arbitrary")),
        name="mem_attention",
    )(q, mk, mv)


def _mem_head_copies(mk_hbm, mv_hbm, kbuf, vbuf, sem, seq, layer, slot):
    copies = []
    for h in range(MEM_HEADS):
        cols = pl.ds(h * MEM_HEAD_DIM, MEM_HEAD_DIM)
        copies.append(pltpu.make_async_copy(mk_hbm.at[seq, layer, :, h, :], kbuf.at[slot, :, cols], sem.at[slot, h]))
        copies.append(pltpu.make_async_copy(mv_hbm.at[seq, layer, :, h, :], vbuf.at[slot, :, cols],
                                            sem.at[slot, MEM_HEADS + h]))
    return copies


def _mem_attn_split_kernel(q_ref, mk_hbm, mv_hbm, o_ref, kbuf, vbuf, sem, *, layer, n_seq):
    i = pl.program_id(0)
    slot = i % 2

    @pl.when(i == 0)
    def _():
        for c in _mem_head_copies(mk_hbm, mv_hbm, kbuf, vbuf, sem, 0, layer, 0):
            c.start()

    @pl.when(i + 1 < n_seq)
    def _():
        for c in _mem_head_copies(mk_hbm, mv_hbm, kbuf, vbuf, sem, i + 1, layer, 1 - slot):
            c.start()

    for c in _mem_head_copies(mk_hbm, mv_hbm, kbuf, vbuf, sem, i, layer, slot):
        c.wait()

    q = q_ref[0]
    heads = [slice(h * MEM_HEAD_DIM, (h + 1) * MEM_HEAD_DIM) for h in range(MEM_HEADS)]
    logits = [_dot_nt(q[:, sl], kbuf[slot, :, sl].astype(BF16)) * (MEM_HEAD_DIM ** -0.5) for sl in heads]
    probs = []
    for logit in logits:
        p = jnp.exp(logit - jnp.max(logit, axis=1, keepdims=True))
        probs.append((p / jnp.sum(p, axis=1, keepdims=True)).astype(BF16))
    outs = [_dot(p, vbuf[slot, :, sl].astype(BF16)) for p, sl in zip(probs, heads)]
    o_ref[0] = jnp.concatenate(outs, axis=-1).astype(o_ref.dtype)


def mem_attention_split(q, mk, mv, layer):
    b, t_len, d = q.shape
    m_len = mk.shape[2]
    return pl.pallas_call(
        functools.partial(_mem_attn_split_kernel, layer=layer, n_seq=b),
        grid=(b,),
        in_specs=[pl.BlockSpec((1, t_len, d), lambda i: (i, 0, 0)),
                  pl.BlockSpec(memory_space=pl.ANY), pl.BlockSpec(memory_space=pl.ANY)],
        out_specs=pl.BlockSpec((1, t_len, d), lambda i: (i, 0, 0)),
        out_shape=jax.ShapeDtypeStruct((b, t_len, d), BF16),
        scratch_shapes=[pltpu.VMEM((2, m_len, d), F32), pltpu.VMEM((2, m_len, d), F32),
                        pltpu.SemaphoreType.DMA((2, 2 * MEM_HEADS))],
        compiler_params=_cparams(("arbitrary",)),
        name="mem_attention_split",
    )(q, mk, mv)


def _prep_layer(l, p):
    w = p['w_in'][l]
    cuts = [0]
    for s in (W_GROUP,) * 9 + (IDX_HEADS * IDX_DIM, IDX_DIM, IDX_HEADS):
        cuts.append(cuts[-1] + s)
    piece = lambda i: w[:, cuts[i]:cuts[i + 1]]
    zeros = lambda n: jnp.zeros((D_MODEL, n), w.dtype)
    w_in = jnp.concatenate(
        [piece(i) for i in range(6)] + [piece(6), piece(7), piece(9), piece(8), piece(10), zeros(LANES - IDX_DIM),
                                        piece(11), zeros(LANES - IDX_HEADS)], axis=1).astype(BF16)
    pool_w = p['pool_w'][l]
    pool_bd = jnp.zeros((W_GROUP, W_GROUP), F32)
    for g in range(len(POOL_WINDOWS)):
        pool_bd = pool_bd.at[g * POOL_CH:(g + 1) * POOL_CH, g * POOL_CH:(g + 1) * POOL_CH].set(pool_w[g])
    return {
        'norm_mix': p['norm_mix'][l], 'w_in': w_in,
        'conv_w': p['conv_w'][l], 'pool_w_bd': pool_bd.astype(BF16),
        'pool_scale': p['pool_scale'][l].reshape(1, W_GROUP), 'chunk_norm': p['chunk_norm'][l].reshape(1, W_GROUP),
        'chunk_ws': p['chunk_ws'][l], 'chunk_b': p['chunk_b'][l],
        'w_out_abc': p['w_out'][l][:3 * W_GROUP].astype(BF16), 'w_out_d': p['w_out'][l][3 * W_GROUP:].astype(BF16),
        'norm_mem': p['norm_mem'][l], 'wq_mem': p['wq_mem'][l].astype(BF16), 'wo_mem': p['wo_mem'][l].astype(BF16),
        'norm_ffn': p['norm_ffn'][l],
    }


def _chunk_consts(lw, chunk):
    mask = jnp.tril(jnp.ones((chunk, chunk), dtype=bool))
    wm = jnp.where(mask[None], lw['chunk_ws'][:, :chunk, :chunk], 0).astype(BF16)
    bias = jnp.repeat(lw['chunk_b'][:, :chunk].T, W_GROUP // CHUNK_HEADS, axis=1)
    return wm, bias


def _rope_tables(pos):
    half = HEAD_DIM // 2
    inv = ROPE_THETA ** (-jnp.arange(half, dtype=F32) / half)
    ang = pos.astype(F32)[:, None] * inv[None, :]
    cos, sin = jnp.cos(ang), jnp.sin(ang)
    cos_t = jnp.concatenate([cos, cos] * (LANES // HEAD_DIM), axis=1)
    sin_t = jnp.concatenate([-sin, sin] * (LANES // HEAD_DIM), axis=1)
    return cos_t, sin_t


def _run_trunk(x, pos0, conv_prev, pool_prev, past, mem_k, mem_v, layers, ffn_w, norm_final):
    b, t_len, d = x.shape
    n = b * t_len
    depth = len(layers)
    from_state = past is not None
    pos = pos0 + jnp.arange(t_len, dtype=I32)
    cos_t, sin_t = _rope_tables(pos)
    h = x.reshape(n, d)
    ks, vs, kis, convs, pools, cvs = [], [], [], [], [], []
    for l, lw in enumerate(layers):
        proj = norm_matmul(h, lw['norm_mix'], lw['w_in']).reshape(b, t_len, P_PAD)
        chunk = min(t_len, CHUNK)
        wm, cbias = _chunk_consts(lw, chunk)
        lw_m = dict(lw, chunk_wm=wm, chunk_bias=cbias)
        outs = mixers(proj, conv_prev[:, l] if from_state else None, pool_prev[:, l] if from_state else None,
                      cos_t, sin_t, lw_m, pos0=pos0, from_state=from_state, emit_cv=from_state)
        mix, q, k, kb, v, vb, qi, ki, ki2, wi, conv_s, pool_s = outs[:12]
        if from_state:
            cvs.append(outs[12])
            yd = dsa_sample(q, qi, wi, kb, vb, ki2, *past, l)
        else:
            yd = dsa_prompt(q, qi, wi, kb, vb, ki2)
        h = matmul_res([mix.reshape(n, 3 * W_GROUP), yd.reshape(n, W_GROUP)], [lw['w_out_abc'], lw['w_out_d']], h)
        qm = norm_matmul(h, lw['norm_mem'], lw['wq_mem'], out_dtype=BF16).reshape(b, t_len, d)
        om = mem_attention(qm, mem_k, mem_v, l) if mem_k.ndim == 4 else mem_attention_split(qm, mem_k, mem_v, l)
        h = matmul_res([om.reshape(n, d)], [lw['wo_mem']], h)
        last = l == depth - 1
        fw = ffn_w[l]
        if fw['kind'] == 'dense':
            h = ffn_dense(h, lw['norm_ffn'], fw['w1'], fw['w3'], fw['w2'], norm_final, final_norm=last)
        else:
            h = moe_ffn(h, lw['norm_ffn'], fw['router'], fw['w1'], fw['w3'], fw['w2'], norm_final, final_norm=last)
        ks.append(k.reshape(b, t_len, ATT_HEADS, HEAD_DIM))
        vs.append(v.reshape(b, t_len, ATT_HEADS, HEAD_DIM))
        kis.append(ki[:, :, :IDX_DIM])
        convs.append(conv_s)
        pools.append(pool_s)
    st = lambda xs: jnp.stack(xs, axis=1)
    y = h.reshape(b, t_len, d)
    return y, st(ks), st(vs), st(kis), st(convs), st(pools), (st(cvs) if cvs else None)


def kernel(x_prompt, x_sample, mem_prompt, cache_k, cache_v, cache_kidx, cache_mem_k, cache_mem_v, state_conv, state_pool, page_table, norm_mix, w_in, conv_w, pool_w, pool_scale, chunk_norm, chunk_ws, chunk_b, w_out, norm_mem, wq_mem, wk_mem, wv_mem, wo_mem, norm_ffn, w1_dense, w3_dense, w2_dense, router, w1_moe, w3_moe, w2_moe, norm_final):
    p = {
        'norm_mix': norm_mix, 'w_in': w_in, 'conv_w': conv_w, 'pool_w': pool_w, 'pool_scale': pool_scale,
        'chunk_norm': chunk_norm, 'chunk_ws': chunk_ws, 'chunk_b': chunk_b, 'w_out': w_out, 'norm_mem': norm_mem,
        'wq_mem': wq_mem, 'wo_mem': wo_mem, 'norm_ffn': norm_ffn,
    }
    depth = norm_mix.shape[0]
    layers = [_prep_layer(l, p) for l in range(depth)]
    ffn_w = []
    for l in range(depth):
        j = l // 2
        if l % 2 == 0:
            ffn_w.append({'kind': 'dense', 'w1': w1_dense[j].astype(BF16), 'w3': w3_dense[j].astype(BF16),
                          'w2': w2_dense[j].astype(BF16)})
        else:
            router_pad = jnp.concatenate([router[j], jnp.zeros((D_MODEL, LANES - N_EXPERTS), F32)], axis=1)
            ffn_w.append({'kind': 'moe', 'router': router_pad, 'w1': w1_moe[j].astype(BF16),
                          'w3': w3_moe[j].astype(BF16), 'w2': w2_moe[j].astype(BF16)})

    n_pages, page = page_table.shape[1], cache_k.shape[2]
    past_len = n_pages * page
    n_phys = cache_k.shape[0]
    kt_pages = jnp.transpose(cache_k, (0, 1, 3, 4, 2)).reshape(n_phys, depth, W_GROUP, page)
    vt_pages = jnp.transpose(cache_v, (0, 1, 3, 4, 2)).reshape(n_phys, depth, W_GROUP, page)
    kit_pages = jnp.transpose(cache_kidx, (0, 1, 3, 2))
    past = (kt_pages, vt_pages, kit_pages, page_table)
    y_s, k_s, v_s, ki_s, conv_s, pool_s, cv_s = _run_trunk(
        x_sample, past_len, state_conv, state_pool, past, cache_mem_k, cache_mem_v, layers, ffn_w, norm_final)

    bp, tp, d = x_prompt.shape
    n_mem = mem_prompt.shape[1]
    w_kv = jnp.concatenate([wk_mem[l] for l in range(depth)] + [wv_mem[l] for l in range(depth)], axis=1).astype(BF16)
    mem_kv = norm_matmul(mem_prompt.reshape(bp * n_mem, d), jnp.ones((d,), F32), w_kv, norm=False)
    mem_kv = mem_kv.reshape(bp, n_mem, 2, depth, d)
    mem_k_p = jnp.moveaxis(mem_kv[:, :, 0], 2, 1)
    mem_v_p = jnp.moveaxis(mem_kv[:, :, 1], 2, 1)
    y_p, k_p, v_p, ki_p, conv_p, pool_p, _ = _run_trunk(
        x_prompt, 0, None, None, None, mem_k_p, mem_v_p, layers, ffn_w, norm_final)

    mem_shape = (bp, depth, n_mem, MEM_HEADS, MEM_HEAD_DIM)
    return (y_p, y_s, k_p, v_p, ki_p, mem_k_p.reshape(mem_shape), mem_v_p.reshape(mem_shape), conv_p, pool_p,
            k_s, v_s, ki_s, conv_s, pool_s, cv_s)
```

```python
import functools
import math

import jax
import jax.numpy as jnp
from jax import lax
from jax.experimental import pallas as pl
from jax.experimental.pallas import tpu as pltpu

F32 = jnp.float32
BF16 = jnp.bfloat16
I32 = jnp.int32

D_MODEL = 1024
W_GROUP = 256
CONV_W = 3
POOL_WINDOWS = (2, 4, 8, 16)
POOL_CH = 64
POOL_BUF = 15
CHUNK = 128
CHUNK_HEADS = 4
ATT_HEADS = 4
HEAD_DIM = 64
IDX_HEADS = 8
IDX_DIM = 64
IDX_W_SCALE = (IDX_HEADS ** -0.5) * (IDX_DIM ** -0.5)
TOPK_MAX = 256
ROPE_THETA = 10000.0
MEM_HEADS = 4
MEM_HEAD_DIM = 256
D_FF = 3584
N_EXPERTS = 8
EPS = 1e-6

LANES = 128
SUBLANES = 8
PREV_ROWS = 16
P_PAD = 3072
HALF = P_PAD // 2
OFF_Q, OFF_K, OFF_QI, OFF_V, OFF_KI, OFF_WI = 1536, 1792, 2048, 2560, 2816, 2944
VMEM_LIMIT = 56 * 1024 * 1024

INT_MIN = -2 ** 31
Q_SCALE = (HEAD_DIM ** -0.5) * math.log2(math.e)


def _cparams(sem):
    return pltpu.CompilerParams(dimension_semantics=sem, vmem_limit_bytes=VMEM_LIMIT)


def _rms(x, g):
    ms = jnp.mean(x * x, axis=-1, keepdims=True)
    return x * lax.rsqrt(ms + EPS) * g


def _dot(a, b):
    return jnp.dot(a, b, preferred_element_type=F32)


def _dot_nt(a, b):
    return lax.dot_general(a, b, (((1,), (1,)), ((), ())), preferred_element_type=F32)


def _norm_matmul_kernel(x_ref, g_ref, w_ref, o_ref, xn_ref, *, norm):
    @pl.when(pl.program_id(1) == 0)
    def _():
        x = x_ref[...]
        if norm:
            x = _rms(x, g_ref[...])
        xn_ref[...] = x.astype(BF16)

    o_ref[...] = _dot(xn_ref[...], w_ref[...]).astype(o_ref.dtype)


def norm_matmul(x, g, w, *, norm=True, out_dtype=F32, tm=1024, tn=1536):
    m, k = x.shape
    n = w.shape[1]
    tm = min(tm, m)
    tn = tn if n % tn == 0 else math.gcd(n, 1024)
    return pl.pallas_call(
        functools.partial(_norm_matmul_kernel, norm=norm),
        grid=(m // tm, n // tn),
        in_specs=[pl.BlockSpec((tm, k), lambda i, j: (i, 0)),
                  pl.BlockSpec((1, k), lambda i, j: (0, 0)),
                  pl.BlockSpec((k, tn), lambda i, j: (0, j))],
        out_specs=pl.BlockSpec((tm, tn), lambda i, j: (i, j)),
        out_shape=jax.ShapeDtypeStruct((m, n), out_dtype),
        scratch_shapes=[pltpu.VMEM((tm, k), BF16)],
        compiler_params=_cparams(("parallel", "arbitrary")),
        name="norm_matmul",
    )(x, g.reshape(1, k), w)


def _matmul_res_kernel(*refs, n_in):
    a_refs, w_refs = refs[:n_in], refs[n_in:2 * n_in]
    r_ref, o_ref = refs[2 * n_in], refs[2 * n_in + 1]
    acc = r_ref[...]
    for a, w in zip(a_refs, w_refs):
        acc = acc + _dot(a[...].astype(BF16), w[...])
    o_ref[...] = acc


def matmul_res(a_list, w_list, res, *, tm=1024):
    m, n = res.shape
    tm = min(tm, m)
    n_in = len(a_list)
    in_specs = [pl.BlockSpec((tm, a.shape[1]), lambda i: (i, 0)) for a in a_list]
    in_specs += [pl.BlockSpec(w.shape, lambda i: (0, 0)) for w in w_list]
    in_specs += [pl.BlockSpec((tm, n), lambda i: (i, 0))]
    return pl.pallas_call(
        functools.partial(_matmul_res_kernel, n_in=n_in),
        grid=(m // tm,),
        in_specs=in_specs,
        out_specs=pl.BlockSpec((tm, n), lambda i: (i, 0)),
        out_shape=jax.ShapeDtypeStruct((m, n), F32),
        compiler_params=_cparams(("parallel",)),
        name="matmul_res",
    )(*a_list, *w_list, res)


def _ffn_kernel(x_ref, g_ref, w1_ref, w3_ref, w2_ref, gf_ref, o_ref, xn_ref, acc_ref, *, final_norm):
    f = pl.program_id(1)

    @pl.when(f == 0)
    def _():
        xn_ref[...] = _rms(x_ref[...], g_ref[...]).astype(BF16)
        acc_ref[...] = jnp.zeros_like(acc_ref)

    xn = xn_ref[...]
    a = _dot(xn, w1_ref[...])
    b = _dot(xn, w3_ref[...])
    hidden = (a * jax.nn.sigmoid(a) * b).astype(BF16)
    acc_ref[...] += _dot(hidden, w2_ref[...])

    @pl.when(f == pl.num_programs(1) - 1)
    def _():
        y = x_ref[...] + acc_ref[...]
        if final_norm:
            y = _rms(y, gf_ref[...])
        o_ref[...] = y


def ffn_dense(x, g, w1, w3, w2, g_final, *, final_norm, tm=1024, tf=896):
    m, d = x.shape
    ff = w1.shape[1]
    tm = min(tm, m)
    return pl.pallas_call(
        functools.partial(_ffn_kernel, final_norm=final_norm),
        grid=(m // tm, ff // tf),
        in_specs=[pl.BlockSpec((tm, d), lambda i, f: (i, 0)),
                  pl.BlockSpec((1, d), lambda i, f: (0, 0)),
                  pl.BlockSpec((d, tf), lambda i, f: (0, f)),
                  pl.BlockSpec((d, tf), lambda i, f: (0, f)),
                  pl.BlockSpec((tf, d), lambda i, f: (f, 0)),
                  pl.BlockSpec((1, d), lambda i, f: (0, 0))],
        out_specs=pl.BlockSpec((tm, d), lambda i, f: (i, 0)),
        out_shape=jax.ShapeDtypeStruct((m, d), F32),
        scratch_shapes=[pltpu.VMEM((tm, d), BF16), pltpu.VMEM((tm, d), F32)],
        compiler_params=_cparams(("parallel", "arbitrary")),
        name="ffn_dense",
    )(x, g.reshape(1, d), w1, w3, w2, g_final.reshape(1, d))


def _router_kernel(x_ref, g_ref, r_ref, hn_ref, e_ref, gate_ref):
    hn = _rms(x_ref[...], g_ref[...])
    hn_ref[...] = hn.astype(BF16)
    logits = jnp.dot(hn, r_ref[...], preferred_element_type=F32, precision=lax.Precision.HIGHEST)
    lane = lax.broadcasted_iota(I32, logits.shape, 1)
    neg = jnp.float32(-jnp.inf)
    l1 = jnp.where(lane < N_EXPERTS, logits, neg)
    m1 = jnp.max(l1, axis=1, keepdims=True)
    i1 = jnp.min(jnp.where(l1 == m1, lane, LANES), axis=1, keepdims=True)
    l2 = jnp.where(lane == i1, neg, l1)
    m2 = jnp.max(l2, axis=1, keepdims=True)
    i2 = jnp.min(jnp.where(l2 == m2, lane, LANES), axis=1, keepdims=True)
    t = jnp.exp(m2 - m1)
    denom = 1.0 + t
    e_ref[...] = jnp.where(lane == 0, i1, jnp.where(lane == 1, i2, 0))
    gate_ref[...] = jnp.where(lane == 0, 1.0 / denom, jnp.where(lane == 1, t / denom, 0.0))


def moe_router(x, g, router_pad, *, tm=512):
    m, d = x.shape
    tm = min(tm, m)
    return pl.pallas_call(
        _router_kernel,
        grid=(m // tm,),
        in_specs=[pl.BlockSpec((tm, d), lambda i: (i, 0)),
                  pl.BlockSpec((1, d), lambda i: (0, 0)),
                  pl.BlockSpec((d, LANES), lambda i: (0, 0))],
        out_specs=[pl.BlockSpec((tm, d), lambda i: (i, 0)),
                   pl.BlockSpec((tm, LANES), lambda i: (i, 0)),
                   pl.BlockSpec((tm, LANES), lambda i: (i, 0))],
        out_shape=[jax.ShapeDtypeStruct((m, d), BF16),
                   jax.ShapeDtypeStruct((m, LANES), I32),
                   jax.ShapeDtypeStruct((m, LANES), F32)],
        compiler_params=_cparams(("parallel",)),
        name="moe_router",
    )(x, g.reshape(1, d), router_pad)


def _expert_kernel(be_ref, nb_ref, xs_ref, gate_ref, w1_ref, w3_ref, w2_ref, o_ref, acc_ref):
    j, f = pl.program_id(0), pl.program_id(1)
    live = j < nb_ref[0]

    @pl.when(f == 0)
    def _():
        acc_ref[...] = jnp.zeros_like(acc_ref)

    @pl.when(live)
    def _():
        xs = xs_ref[...]
        a = _dot(xs, w1_ref[0])
        b = _dot(xs, w3_ref[0])
        hidden = (a * jax.nn.sigmoid(a) * b).astype(BF16)
        acc_ref[...] += _dot(hidden, w2_ref[0])

    @pl.when(f == pl.num_programs(1) - 1)
    def _():
        o_ref[...] = acc_ref[...] * gate_ref[...]


def moe_experts(block_e, n_live, xs, slot_gate, w1, w3, w2, *, bm, tf=896):
    n_slots, d = xs.shape
    ff = w1.shape[2]
    n_blocks = n_slots // bm
    grid_spec = pltpu.PrefetchScalarGridSpec(
        num_scalar_prefetch=2,
        grid=(n_blocks, ff // tf),
        in_specs=[pl.BlockSpec((bm, d), lambda j, f, be, nb: (j, 0)),
                  pl.BlockSpec((bm, 1), lambda j, f, be, nb: (j, 0)),
                  pl.BlockSpec((1, d, tf), lambda j, f, be, nb: (be[j], 0, f)),
                  pl.BlockSpec((1, d, tf), lambda j, f, be, nb: (be[j], 0, f)),
                  pl.BlockSpec((1, tf, d), lambda j, f, be, nb: (be[j], f, 0))],
        out_specs=pl.BlockSpec((bm, d), lambda j, f, be, nb: (j, 0)),
        scratch_shapes=[pltpu.VMEM((bm, d), F32)],
    )
    return pl.pallas_call(
        _expert_kernel,
        grid_spec=grid_spec,
        out_shape=jax.ShapeDtypeStruct((n_slots, d), F32),
        compiler_params=_cparams(("parallel", "arbitrary")),
        name="moe_experts",
    )(block_e, n_live, xs, slot_gate, w1, w3, w2)


def _combine_kernel(x_ref, a_ref, b_ref, gf_ref, o_ref, *, final_norm):
    y = x_ref[...] + (a_ref[...] + b_ref[...])
    if final_norm:
        y = _rms(y, gf_ref[...])
    o_ref[...] = y


def moe_combine(x, ya, yb, g_final, *, final_norm, tm=512):
    m, d = x.shape
    tm = min(tm, m)
    spec = pl.BlockSpec((tm, d), lambda i: (i, 0))
    return pl.pallas_call(
        functools.partial(_combine_kernel, final_norm=final_norm),
        grid=(m // tm,),
        in_specs=[spec, spec, spec, pl.BlockSpec((1, d), lambda i: (0, 0))],
        out_specs=spec,
        out_shape=jax.ShapeDtypeStruct((m, d), F32),
        compiler_params=_cparams(("parallel",)),
        name="moe_combine",
    )(x, ya, yb, g_final.reshape(1, d))


def _compact_kernel(ib_ref, ic_ref, first_ref, total_ref, tok_ref, hn_ref, o_ref, acc_ref, *, tc):
    i = pl.program_id(0)

    @pl.when(first_ref[i] == 1)
    def _():
        acc_ref[...] = jnp.zeros_like(acc_ref)

    @pl.when(i < total_ref[0])
    def _():
        local = tok_ref[...] - ic_ref[i] * tc
        lane = lax.broadcasted_iota(I32, (tok_ref.shape[0], tc), 1)
        onehot = jnp.where(local == lane, 1.0, 0.0).astype(BF16)
        acc_ref[...] += _dot(onehot, hn_ref[...])

    o_ref[...] = acc_ref[...].astype(o_ref.dtype)


def moe_compact(item_block, item_chunk, item_first, n_items, slot_tok, hn, *, bm, tc):
    n_slots = slot_tok.shape[0]
    n, d = hn.shape
    n_max = item_block.shape[0]
    grid_spec = pltpu.PrefetchScalarGridSpec(
        num_scalar_prefetch=4,
        grid=(n_max,),
        in_specs=[pl.BlockSpec((bm, 1), lambda i, ib, ic, fr, tot: (ib[i], 0)),
                  pl.BlockSpec((tc, d), lambda i, ib, ic, fr, tot: (ic[i], 0))],
        out_specs=pl.BlockSpec((bm, d), lambda i, ib, ic, fr, tot: (ib[i], 0)),
        scratch_shapes=[pltpu.VMEM((bm, d), F32)],
    )
    return pl.pallas_call(
        functools.partial(_compact_kernel, tc=tc),
        grid_spec=grid_spec,
        out_shape=jax.ShapeDtypeStruct((n_slots, d), BF16),
        compiler_params=_cparams(("arbitrary",)),
        name="moe_compact",
    )(item_block, item_chunk, item_first, n_items, slot_tok.reshape(n_slots, 1), hn)


def moe_ffn(x, g, router_pad, w1, w3, w2, g_final, *, final_norm):
    n, d = x.shape
    bm = 512 if n >= 4096 else 128
    tc = min(n, 512)
    hn, e12, g12 = moe_router(x, g, router_pad)
    e_flat = e12[:, :2].reshape(-1)
    g_flat = g12[:, :2].reshape(-1)
    n_assign = 2 * n
    order = jnp.argsort(e_flat, stable=True).astype(I32)
    rank = jnp.argsort(order).astype(I32)
    counts = jnp.sum(e_flat[:, None] == jnp.arange(N_EXPERTS, dtype=I32)[None, :], axis=0).astype(I32)
    padded = (counts + bm - 1) // bm * bm
    start = jnp.cumsum(counts) - counts
    pend = jnp.cumsum(padded)
    pstart = pend - padded
    n_blocks = -(-n_assign // bm) + N_EXPERTS
    n_slots = n_blocks * bm
    block_start = jnp.arange(n_blocks, dtype=I32) * bm
    block_e = jnp.minimum(jnp.sum(pend[None, :] <= block_start[:, None], axis=1), N_EXPERTS - 1).astype(I32)
    n_live = (pend[-1] // bm).astype(I32).reshape(1)
    slot_e = jnp.repeat(block_e, bm)
    slot_idx = jnp.arange(n_slots, dtype=I32) - pstart[slot_e]
    slot_live = slot_idx < counts[slot_e]
    slot_src = order[jnp.clip(start[slot_e] + slot_idx, 0, n_assign - 1)]
    slot_tok = jnp.where(slot_live, slot_src // 2, -1)
    slot_gate = jnp.where(slot_live, g_flat[slot_src], 0.0)
    assign_slot = pstart[e_flat] + rank - start[e_flat]
    tok_b = slot_tok.reshape(n_blocks, bm)
    c_lo = jnp.maximum(tok_b[:, 0], 0) // tc
    c_hi = jnp.maximum(jnp.max(tok_b, axis=1), 0) // tc
    n_items_b = c_hi - c_lo + 1
    item_end = jnp.cumsum(n_items_b)
    total = item_end[-1]
    n_items_max = n_blocks + N_EXPERTS * (n // tc)
    item_id = jnp.arange(n_items_max, dtype=I32)
    item = jnp.minimum(item_id, total - 1)
    item_block = jnp.sum(item_end[None, :] <= item[:, None], axis=1).astype(I32)
    item_off = item - (item_end - n_items_b)[item_block]
    item_chunk = (c_lo[item_block] + item_off).astype(I32)
    item_first = ((item_off == 0) & (item_id < total)).astype(I32)
    xs = moe_compact(item_block, item_chunk, item_first, total.astype(I32).reshape(1), slot_tok, hn, bm=bm, tc=tc)
    ys = moe_experts(block_e, n_live, xs, slot_gate.reshape(n_slots, 1), w1, w3, w2, bm=bm)
    pair = assign_slot.reshape(n, 2)
    return moe_combine(x, ys[pair[:, 0]], ys[pair[:, 1]], g_final, final_norm=final_norm)


def _swap_half_heads(x):
    n = x.shape[-1]
    lane = lax.broadcasted_iota(I32, x.shape, x.ndim - 1)
    fwd = pltpu.roll(x, n - HEAD_DIM // 2, x.ndim - 1)
    bwd = pltpu.roll(x, HEAD_DIM // 2, x.ndim - 1)
    return jnp.where(lane % HEAD_DIM < HEAD_DIM // 2, fwd, bwd)


def _rope(x, cos, sin):
    reps = x.shape[-1] // LANES
    if reps > 1:
        cos = jnp.concatenate([cos] * reps, axis=-1)
        sin = jnp.concatenate([sin] * reps, axis=-1)
    return x * cos + _swap_half_heads(x) * sin


def _mixer_kernel(*refs, tt, chunk, pos0, from_state, emit_cv):
    it = iter(refs)
    xa_ref, xb_ref = next(it), next(it)
    if from_state:
        convp_ref, poolp_ref = next(it), next(it)
    else:
        prev_ref = next(it)
    cos_ref, sin_ref = next(it), next(it)
    convw_ref, poolw_ref, pscale_ref, cnorm_ref, wm_ref, cbias_ref = (next(it) for _ in range(6))
    mix_ref, q_ref, k_ref, kb_ref, v_ref, vb_ref, qi_ref, ki_ref, ki2_ref, wi_ref, convs_ref, pools_ref = (
        next(it) for _ in range(12))
    cv_ref = next(it) if emit_cv else None
    ybuf, xbuf = next(it), next(it)

    t = pl.program_id(1)
    xa = xa_ref[0]
    a_in, a_b, a_c = xa[:, 0:256], xa[:, 256:512], xa[:, 512:768]
    b_in, c_u, c_v = xa[:, 768:1024], xa[:, 1024:1280], xa[:, 1280:1536]

    y = a_c * a_in
    if from_state:
        ybuf[PREV_ROWS - 2:PREV_ROWS, :] = convp_ref[0]
        xbuf[0:1, :] = jnp.zeros((1, W_GROUP), F32)
        xbuf[1:PREV_ROWS, :] = poolp_ref[0]
    else:
        first = (t == 0).astype(F32)
        prev = prev_ref[0] * (1.0 - first)
        ybuf[0:PREV_ROWS, :] = prev[:, 512:768] * prev[:, 0:256]
        xbuf[0:PREV_ROWS, :] = prev[:, 768:1024]
    ybuf[PREV_ROWS:PREV_ROWS + tt, :] = y
    xbuf[PREV_ROWS:PREV_ROWS + tt, :] = b_in

    cw = convw_ref[...]
    z = cw[0:1, :] * ybuf[PREV_ROWS - 2:PREV_ROWS - 2 + tt, :]
    z = z + cw[1:2, :] * ybuf[PREV_ROWS - 1:PREV_ROWS - 1 + tt, :]
    z = z + cw[2:3, :] * y
    ya = a_b * z
    convs_ref[0] = ybuf[PREV_ROWS + tt - 2:PREV_ROWS + tt, :]

    lane = lax.broadcasted_iota(I32, (tt, W_GROUP), 1)
    row = lax.broadcasted_iota(I32, (tt, W_GROUP), 0)
    pos = (pos0 + t * tt + row).astype(F32)
    run = b_in
    mean = jnp.zeros((tt, W_GROUP), F32)
    for j in range(1, POOL_WINDOWS[-1] + 1):
        if j > 1:
            run = run + xbuf[PREV_ROWS - (j - 1):PREV_ROWS - (j - 1) + tt, :]
        if j in POOL_WINDOWS:
            grp = POOL_WINDOWS.index(j)
            cnt = jnp.minimum(pos + 1.0, float(j))
            mean = jnp.where(lane // POOL_CH == grp, run / cnt, mean)
    dlt = (mean - b_in).astype(BF16)
    yb = _dot(dlt, poolw_ref[...]) * pscale_ref[...]
    pools_ref[0] = xbuf[PREV_ROWS + tt - POOL_BUF:PREV_ROWS + tt, :]

    vn = _rms(c_v, cnorm_ref[...])
    if emit_cv:
        cv_ref[0] = vn
    vnb = vn.astype(BF16)
    lane_c = lax.broadcasted_iota(I32, (chunk, W_GROUP), 1)
    parts = []
    for c in range(tt // chunk):
        vc = vnb[c * chunk:(c + 1) * chunk, :]
        s = cbias_ref[...]
        for hd in range(CHUNK_HEADS):
            sh = _dot(wm_ref[hd], vc)
            s = s + jnp.where(lane_c // (W_GROUP // CHUNK_HEADS) == hd, sh, 0.0)
        parts.append(s)
    s_all = parts[0] if len(parts) == 1 else jnp.concatenate(parts, axis=0)
    yc = c_u * s_all

    mix_ref[0] = jnp.concatenate([ya, yb, yc], axis=-1).astype(BF16)

    xb = xb_ref[0]
    cos, sin = cos_ref[...], sin_ref[...]
    q = _rope(xb[:, 0:256], cos, sin)
    k = _rope(xb[:, 256:512], cos, sin)
    qi = _rope(xb[:, 512:1024], cos, sin)
    v = xb[:, 1024:1280]
    ki = _rope(xb[:, 1280:1408], cos, sin)
    wi = xb[:, 1408:1536] * IDX_W_SCALE
    q_ref[0] = (q * Q_SCALE).astype(BF16)
    k_ref[0] = k
    kb_ref[0] = k.astype(BF16)
    v_ref[0] = v
    if from_state:
        vb_ref[0] = v.astype(BF16)
    else:
        vb_ref[0, 0] = v.T.astype(BF16)
    qi_ref[0] = qi.astype(BF16)
    ki_ref[0] = ki
    lane128 = lax.broadcasted_iota(I32, ki.shape, 1)
    ki2_ref[0] = jnp.where(lane128 < IDX_DIM, ki, pltpu.roll(ki, IDX_DIM, 1)).astype(BF16)
    wi_ref[0] = wi


def mixers(proj, conv_prev, pool_prev, cos, sin, lw, *, pos0, from_state, emit_cv):
    b, t_len, _ = proj.shape
    tt = min(t_len, 256)
    chunk = min(t_len, CHUNK)
    nt = t_len // tt
    in_specs = [pl.BlockSpec((1, tt, HALF), lambda i, t: (i, t, 0)),
                pl.BlockSpec((1, tt, HALF), lambda i, t: (i, t, 1))]
    args = [proj, proj]
    if from_state:
        in_specs += [pl.BlockSpec((1, CONV_W - 1, W_GROUP), lambda i, t: (i, 0, 0)),
                     pl.BlockSpec((1, POOL_BUF, W_GROUP), lambda i, t: (i, 0, 0))]
        args += [conv_prev, pool_prev]
    else:
        per = tt // PREV_ROWS
        in_specs += [pl.BlockSpec((1, PREV_ROWS, HALF), lambda i, t: (i, jnp.maximum(t * per - 1, 0), 0))]
        args += [proj]
    in_specs += [pl.BlockSpec((tt, LANES), lambda i, t: (t, 0)),
                 pl.BlockSpec((tt, LANES), lambda i, t: (t, 0))]
    args += [cos, sin]
    consts = [lw['conv_w'], lw['pool_w_bd'], lw['pool_scale'], lw['chunk_norm'], lw['chunk_wm'], lw['chunk_bias']]
    for c in consts:
        in_specs.append(pl.BlockSpec(c.shape, lambda i, t, nd=c.ndim: (0,) * nd))
    args += consts

    def tok(width, dtype):
        return (pl.BlockSpec((1, tt, width), lambda i, t: (i, t, 0)), jax.ShapeDtypeStruct((b, t_len, width), dtype))

    outs = [tok(768, BF16),
            tok(256, BF16),
            tok(256, F32), tok(256, BF16),
            tok(256, F32),
            tok(256, BF16) if from_state else
            (pl.BlockSpec((1, 1, W_GROUP, tt), lambda i, t: (i, t, 0, 0)),
             jax.ShapeDtypeStruct((b, nt, W_GROUP, tt), BF16)),
            tok(512, BF16),
            tok(LANES, F32),
            tok(LANES, BF16),
            tok(LANES, F32),
            (pl.BlockSpec((1, CONV_W - 1, W_GROUP), lambda i, t: (i, 0, 0)),
             jax.ShapeDtypeStruct((b, CONV_W - 1, W_GROUP), F32)),
            (pl.BlockSpec((1, POOL_BUF, W_GROUP), lambda i, t: (i, 0, 0)),
             jax.ShapeDtypeStruct((b, POOL_BUF, W_GROUP), F32))]
    if emit_cv:
        outs.append(tok(256, F32))
    return pl.pallas_call(
        functools.partial(_mixer_kernel, tt=tt, chunk=chunk, pos0=pos0, from_state=from_state, emit_cv=emit_cv),
        grid=(b, nt),
        in_specs=in_specs,
        out_specs=[o[0] for o in outs],
        out_shape=[o[1] for o in outs],
        scratch_shapes=[pltpu.VMEM((PREV_ROWS + tt, W_GROUP), F32), pltpu.VMEM((PREV_ROWS + tt, W_GROUP), F32)],
        compiler_params=_cparams(("parallel", "arbitrary")),
        name="mixers",
    )(*args)


def _ordered_f32(key):
    return pltpu.bitcast(key ^ ((key >> 31) & jnp.int32(0x7FFFFFFF)), F32)


def _count_rows(score_ref, n_tiles, pred):
    rows = score_ref.shape[1]
    lane = lax.broadcasted_iota(I32, (rows, LANES), 1)
    acc = jnp.zeros((rows, LANES), F32)
    for j in range(n_tiles):
        acc = acc + jnp.where(pred(score_ref[j], j * LANES + lane), 1.0, 0.0)
    return jnp.sum(acc, axis=1, keepdims=True)


def _count_cols(score_ref, n_groups, gsz, pred):
    cols = score_ref.shape[1]
    pos0 = lax.broadcasted_iota(I32, (gsz, cols), 0)
    last = n_groups - 1

    def group_hits(c):
        blk = score_ref[pl.ds(pl.multiple_of(c * gsz, gsz), gsz), :]
        hit = jnp.where(pred(blk, c * gsz + pos0), 1.0, 0.0)
        return jnp.sum(hit.reshape(gsz // SUBLANES, SUBLANES, cols), axis=0)

    def body(i, acc):
        second = jnp.where(2 * i + 1 <= last, 1.0, 0.0)
        return acc + group_hits(2 * i) + second * group_hits(jnp.minimum(2 * i + 1, last))

    acc = lax.fori_loop(0, (n_groups + 1) // 2, body, jnp.zeros((SUBLANES, cols), F32))
    return jnp.sum(acc, axis=0, keepdims=True)


def _kth_largest(count, topk, two_bits_per_trip):
    def enough(key):
        cand = _ordered_f32(key)
        return count(lambda s, pos: s >= cand) >= topk

    key = jnp.where(count(lambda s, pos: s >= 0.0) >= topk, 0, INT_MIN).astype(I32)
    if two_bits_per_trip:
        top = key | jnp.int32(1 << 30)
        key = jnp.where(enough(top), top, key)

        def two_bits(i, key):
            lo = 28 - 2 * i
            c1, c2, c3 = (key | (jnp.int32(v) << lo) for v in (1, 2, 3))
            return jnp.where(enough(c3), c3, jnp.where(enough(c2), c2, jnp.where(enough(c1), c1, key)))

        key = lax.fori_loop(0, 15, two_bits, key)
    else:
        def one_bit(i, key):
            cand = key | (jnp.int32(1) << (30 - i))
            return jnp.where(enough(cand), cand, key)

        key = lax.fori_loop(0, 31, one_bit, key)
    n_valid = count(lambda s, pos: s > -jnp.inf)
    return jnp.where(n_valid < topk, -jnp.inf, _ordered_f32(key))


def _tie_bound(count, thr, shape, topk, n_keys):
    need = topk - count(lambda s, pos: s > thr)
    n_eq = count(lambda s, pos: s == thr)
    idx_bits = n_keys.bit_length()
    excess = jnp.max(jnp.where((n_eq > need) & (thr > -jnp.inf), 1.0, 0.0))

    def tie_bound():
        def bound_bit(i, bound):
            cand = bound | (jnp.int32(1) << (idx_bits - 1 - i))
            hits = count(lambda s, pos: (s == thr) & (pos < cand))
            return jnp.where(hits <= need, cand, bound)

        return lax.fori_loop(0, idx_bits, bound_bit, jnp.zeros(shape, I32))

    return lax.cond(excess > 0.0, tie_bound, lambda: jnp.full(shape, 2 ** idx_bits - 1, I32))


def _dsa_prompt_kernel(q_ref, qi_ref, wi_ref, k_ref, vt_ref, ki2_ref, o_ref, score_ref, *scratch,
                       tq, kc1, kc, topk, n_keys):
    acc_refs, logit_refs = scratch[:ATT_HEADS], scratch[ATT_HEADS:]
    qb = pl.program_id(1)
    n_chunks = (qb * tq + tq + kc - 1) // kc
    neg_inf = jnp.float32(-jnp.inf)
    q_pos = qb * tq + lax.broadcasted_iota(I32, (kc, tq), 1)
    pos0 = lax.broadcasted_iota(I32, (kc, tq), 0)
    sub = lax.broadcasted_iota(I32, (LANES, tq), 0)

    q_t = q_ref[0].astype(F32).T
    qi_t = qi_ref[0].astype(F32).T
    w_t = wi_ref[0].T

    def head_tile(x_t, h, dim):
        tile = x_t[(h // 2) * LANES:(h // 2 + 1) * LANES]
        return jnp.where((sub // dim) == (h % 2), tile, 0.0).astype(BF16)

    qim = [head_tile(qi_t, h, IDX_DIM) for h in range(IDX_HEADS)]
    qhm = [head_tile(q_t, h, HEAD_DIM) for h in range(ATT_HEADS)]

    q_pos1 = qb * tq + lax.broadcasted_iota(I32, (kc1, tq), 1)
    pos1 = lax.broadcasted_iota(I32, (kc1, tq), 0)

    def score_chunk(c, carry):
        off = pl.multiple_of(c * kc1, kc1)
        ki2 = ki2_ref[0, pl.ds(off, kc1), :]
        acc = jnp.zeros((kc1, tq), F32)
        for h in range(IDX_HEADS):
            acc = acc + w_t[h:h + 1] * jnp.maximum(_dot(ki2, qim[h]), 0.0)
        score_ref[pl.ds(off, kc1), :] = jnp.where(c * kc1 + pos1 <= q_pos1, acc, neg_inf)
        return carry

    lax.fori_loop(0, (qb * tq + tq + kc1 - 1) // kc1, score_chunk, 0)

    count = functools.partial(_count_cols, score_ref, n_chunks, kc)
    thr = _kth_largest(count, topk, two_bits_per_trip=False)
    bound = _tie_bound(count, thr, (1, tq), topk, n_keys)

    for acc_ref in acc_refs:
        acc_ref[...] = jnp.zeros_like(acc_ref)

    logit_a, logit_b = logit_refs[:ATT_HEADS], logit_refs[ATT_HEADS:]
    last = n_chunks - 1

    def score_matmuls(c, dst):
        off = pl.multiple_of(c * kc, kc)
        for h in range(ATT_HEADS):
            dst[h][...] = _dot(k_ref[0, pl.ds(off, kc), (h // 2) * LANES:(h // 2 + 1) * LANES], qhm[h])

    def softmax_pv(c, live, src, m_run, l_run):
        pos = c * kc + jnp.where(live, 0, n_keys) + pos0
        score = score_ref[pl.ds(pl.multiple_of(c * kc, kc), kc), :]
        sel = ((score > thr) | ((score == thr) & (pos < bound))) & (pos <= q_pos)
        m_new, l_new, probs, alphas = [], [], [], []
        for h in range(ATT_HEADS):
            logit = jnp.where(sel, src[h][...], neg_inf)
            m_h = jnp.maximum(m_run[h], jnp.max(logit, axis=0, keepdims=True))
            m_safe = jnp.where(m_h == neg_inf, 0.0, m_h)
            p = jnp.exp2(logit - m_safe)
            alpha = jnp.exp2(m_run[h] - m_safe)
            l_new.append(alpha * l_run[h] + jnp.sum(p, axis=0, keepdims=True))
            m_new.append(m_h)
            probs.append(p.astype(BF16))
            alphas.append(alpha)
        for h in range(ATT_HEADS):
            pv = _dot(vt_ref[0, c, h * HEAD_DIM:(h + 1) * HEAD_DIM, :], probs[h])
            acc_refs[h][...] = acc_refs[h][...] * alphas[h] + pv
        return tuple(m_new), tuple(l_new)

    def attend_pair(i, carry):
        m_run, l_run = carry
        c0 = 2 * i
        c1 = jnp.minimum(c0 + 1, last)
        score_matmuls(c1, logit_b)
        m_run, l_run = softmax_pv(c0, True, logit_a, m_run, l_run)
        score_matmuls(jnp.minimum(c0 + 2, last), logit_a)
        return softmax_pv(c1, c0 + 1 <= last, logit_b, m_run, l_run)

    init = (tuple(jnp.full((1, tq), neg_inf, F32) for _ in range(ATT_HEADS)),
            tuple(jnp.zeros((1, tq), F32) for _ in range(ATT_HEADS)))
    score_matmuls(0, logit_a)
    _, l_fin = lax.fori_loop(0, (n_chunks + 1) // 2, attend_pair, init)
    out_t = jnp.concatenate([acc_refs[h][...] * (1.0 / l_fin[h]) for h in range(ATT_HEADS)], axis=0)
    o_ref[0] = out_t.T.astype(o_ref.dtype)


def dsa_prompt(q, qi, wi, k, vt, ki2):
    b, t_len, _ = q.shape
    kc = vt.shape[-1]
    tq = kc
    kc1 = math.gcd(t_len, 2 * kc)
    topk = min(TOPK_MAX, t_len // 4)
    return pl.pallas_call(
        functools.partial(_dsa_prompt_kernel, tq=tq, kc1=kc1, kc=kc, topk=topk, n_keys=t_len),
        grid=(b, t_len // tq),
        in_specs=[pl.BlockSpec((1, tq, W_GROUP), lambda i, j: (i, j, 0)),
                  pl.BlockSpec((1, tq, 2 * W_GROUP), lambda i, j: (i, j, 0)),
                  pl.BlockSpec((1, tq, LANES), lambda i, j: (i, j, 0)),
                  pl.BlockSpec((1, t_len, W_GROUP), lambda i, j: (i, 0, 0)),
                  pl.BlockSpec((1, t_len // kc, W_GROUP, kc), lambda i, j: (i, 0, 0, 0)),
                  pl.BlockSpec((1, t_len, LANES), lambda i, j: (i, 0, 0))],
        out_specs=pl.BlockSpec((1, tq, W_GROUP), lambda i, j: (i, j, 0)),
        out_shape=jax.ShapeDtypeStruct((b, t_len, W_GROUP), BF16),
        scratch_shapes=[pltpu.VMEM((t_len, tq), F32)] + [pltpu.VMEM((HEAD_DIM, tq), F32)] * ATT_HEADS + [pltpu.VMEM((kc, tq), F32)] * (2 * ATT_HEADS),
        compiler_params=_cparams(("parallel", "arbitrary")),
        name="dsa_prompt",
    )(q, qi, wi, k, vt, ki2)


def _dsa_sample_kernel(*refs, g, t, n_pages, page, topk):
    pt_ref, q_ref, qi_ref, wi_ref, kn_ref, vn_ref, kin_ref = refs[:7]
    n_pg = g * n_pages
    kt_refs, vt_refs, kit_refs = refs[7:7 + n_pg], refs[7 + n_pg:7 + 2 * n_pg], refs[7 + 2 * n_pg:7 + 3 * n_pg]
    o_ref, score_ref = refs[7 + 3 * n_pg], refs[8 + 3 * n_pg]
    del pt_ref
    past = n_pages * page
    s_len = past + LANES
    n_tiles = s_len // LANES
    lane = lax.broadcasted_iota(I32, (t, LANES), 1)
    lane_o = lax.broadcasted_iota(I32, (t, W_GROUP), 1)
    col = lax.broadcasted_iota(I32, (t, s_len), 1)
    causal = col <= past + lax.broadcasted_iota(I32, (t, s_len), 0)
    neg_inf = jnp.float32(-jnp.inf)

    def pad_rows(x):
        return jnp.concatenate([x.astype(F32), jnp.zeros((LANES - t, x.shape[1]), F32)], axis=0).astype(BF16)

    def pages(page_refs, b):
        return jnp.concatenate([page_refs[b * n_pages + p][0, 0] for p in range(n_pages)], axis=1).astype(BF16)

    all_dots = []
    for b in range(g):
        qi = qi_ref[b].astype(F32)
        lhs = jnp.concatenate(
            [jnp.where((lane // IDX_DIM) == (h % 2), qi[:, (h // 2) * LANES:(h // 2 + 1) * LANES], 0.0)
             for h in range(IDX_HEADS)], axis=0).astype(BF16)
        kit = pages(kit_refs, b)
        all_dots.append(jnp.concatenate([_dot(lhs, jnp.concatenate([kit, kit], axis=0)),
                                         _dot_nt(lhs, pad_rows(kin_ref[b]))], axis=1))
    for b in range(g):
        dots, wi = all_dots[b], wi_ref[b]
        acc = jnp.zeros((t, s_len), F32)
        for h in range(IDX_HEADS):
            acc = acc + wi[:, h:h + 1] * jnp.maximum(dots[h * t:(h + 1) * t], 0.0)
        score = jnp.where(causal, acc, neg_inf)
        for j in range(n_tiles):
            score_ref[j, b * t:(b + 1) * t, :] = score[:, j * LANES:(j + 1) * LANES]

    count = functools.partial(_count_rows, score_ref, n_tiles)
    thr = _kth_largest(count, topk, two_bits_per_trip=True)
    bound = _tie_bound(count, thr, (g * t, 1), topk, s_len)

    all_logits = []
    for b in range(g):
        q = q_ref[b].astype(F32)
        lhs = jnp.concatenate([jnp.where((lane_o // HEAD_DIM) == h, q, 0.0) for h in range(ATT_HEADS)],
                              axis=0).astype(BF16)
        all_logits.append(
            jnp.concatenate([_dot(lhs, pages(kt_refs, b)), _dot_nt(lhs, pad_rows(kn_ref[b]))], axis=1))
    all_probs, all_sums = [], []
    for b in range(g):
        rows = slice(b * t, (b + 1) * t)
        score = jnp.concatenate([score_ref[j, rows, :] for j in range(n_tiles)], axis=1)
        sel = ((score > thr[rows]) | ((score == thr[rows]) & (col < bound[rows]))) & causal
        logit = jnp.where(jnp.concatenate([sel] * ATT_HEADS, axis=0), all_logits[b], neg_inf)
        p = jnp.exp2(logit - jnp.max(logit, axis=1, keepdims=True))
        all_probs.append(p.astype(BF16))
        all_sums.append(jnp.sum(p, axis=1, keepdims=True))
    for b in range(g):
        pb = all_probs[b]
        res = _dot_nt(pb[:, :past], pages(vt_refs, b)) + _dot(pb[:, past:], pad_rows(vn_ref[b]))
        res = res / all_sums[b]
        out = jnp.zeros((t, W_GROUP), F32)
        for h in range(ATT_HEADS):
            out = jnp.where((lane_o // HEAD_DIM) == h, res[h * t:(h + 1) * t], out)
        o_ref[b] = out.astype(o_ref.dtype)


def dsa_sample(q, qi, wi, k_new, v_new, ki2_new, kt_pages, vt_pages, kit_pages, page_table, layer, *, g=4):
    b, t_len, _ = q.shape
    n_pages = page_table.shape[1]
    page = kt_pages.shape[-1]
    g = math.gcd(g, b)
    topk = min(TOPK_MAX, (n_pages * page + t_len) // 4)

    def spec(width):
        return pl.BlockSpec((g, t_len, width), lambda i, pt: (i, 0, 0))

    def page_specs(rows):
        return [pl.BlockSpec((1, 1, rows, page),
                             lambda i, pt, s=s, p=p: (pt[(i * g + s) * n_pages + p], layer, 0, 0))
                for s in range(g) for p in range(n_pages)]

    grid_spec = pltpu.PrefetchScalarGridSpec(
        num_scalar_prefetch=1,
        grid=(b // g,),
        in_specs=[spec(W_GROUP), spec(2 * W_GROUP), spec(LANES), spec(W_GROUP), spec(W_GROUP), spec(LANES)]
        + page_specs(W_GROUP) + page_specs(W_GROUP) + page_specs(IDX_DIM),
        out_specs=spec(W_GROUP),
        scratch_shapes=[pltpu.VMEM((n_pages + 1, g * t_len, LANES), F32)],
    )
    n_pg = g * n_pages
    return pl.pallas_call(
        functools.partial(_dsa_sample_kernel, g=g, t=t_len, n_pages=n_pages, page=page, topk=topk),
        grid_spec=grid_spec,
        out_shape=jax.ShapeDtypeStruct((b, t_len, W_GROUP), BF16),
        compiler_params=_cparams(("parallel",)),
        name="dsa_sample",
    )(page_table.reshape(-1), q, qi, wi, k_new, v_new, ki2_new,
      *([kt_pages] * n_pg), *([vt_pages] * n_pg), *([kit_pages] * n_pg))


def _mem_attn_kernel(q_ref, mk_ref, mv_ref, o_ref, *, g):
    for b in range(g):
        q = q_ref[b]
        outs = []
        for h in range(MEM_HEADS):
            sl = slice(h * MEM_HEAD_DIM, (h + 1) * MEM_HEAD_DIM)
            logit = _dot_nt(q[:, sl], mk_ref[b, 0, :, sl].astype(BF16)) * (MEM_HEAD_DIM ** -0.5)
            m = jnp.max(logit, axis=1, keepdims=True)
            p = jnp.exp(logit - m)
            p = p / jnp.sum(p, axis=1, keepdims=True)
            outs.append(_dot(p.astype(BF16), mv_ref[b, 0, :, sl].astype(BF16)))
        o_ref[b] = jnp.concatenate(outs, axis=-1).astype(o_ref.dtype)


def mem_attention(q, mk, mv, layer, *, tq=512, g=1):
    b, t_len, d = q.shape
    m_len = mk.shape[2]
    if t_len >= tq:
        g = 1
    else:
        tq, g = t_len, math.gcd(g, b)
    mem_spec = pl.BlockSpec((g, 1, m_len, d), lambda i, j: (i, layer, 0, 0))
    return pl.pallas_call(
        functools.partial(_mem_attn_kernel, g=g),
        grid=(b // g, t_len // tq),
        in_specs=[pl.BlockSpec((g, tq, d), lambda i, j: (i, j, 0)), mem_spec, mem_spec],
        out_specs=pl.BlockSpec((g, tq, d), lambda i, j: (i, j, 0)),
        out_shape=jax.ShapeDtypeStruct((b, t_len, d), BF16),
        compiler_params=_cparams(("parallel", "arbitrary")),
        name="mem_attention",
    )(q, mk, mv)


def _mem_head_copies(mk_hbm, mv_hbm, kbuf, vbuf, sem, seq, layer, slot):
    copies = []
    for h in range(MEM_HEADS):
        cols = pl.ds(h * MEM_HEAD_DIM, MEM_HEAD_DIM)
        copies.append(pltpu.make_async_copy(mk_hbm.at[seq, layer, :, h, :], kbuf.at[slot, :, cols], sem.at[slot, h]))
        copies.append(pltpu.make_async_copy(mv_hbm.at[seq, layer, :, h, :], vbuf.at[slot, :, cols],
                                            sem.at[slot, MEM_HEADS + h]))
    return copies


def _mem_attn_split_kernel(q_ref, mk_hbm, mv_hbm, o_ref, kbuf, vbuf, sem, *, layer, n_seq):
    i = pl.program_id(0)
    slot = i % 2

    @pl.when(i == 0)
    def _():
        for c in _mem_head_copies(mk_hbm, mv_hbm, kbuf, vbuf, sem, 0, layer, 0):
            c.start()

    @pl.when(i + 1 < n_seq)
    def _():
        for c in _mem_head_copies(mk_hbm, mv_hbm, kbuf, vbuf, sem, i + 1, layer, 1 - slot):
            c.start()

    for c in _mem_head_copies(mk_hbm, mv_hbm, kbuf, vbuf, sem, i, layer, slot):
        c.wait()

    q = q_ref[0]
    heads = [slice(h * MEM_HEAD_DIM, (h + 1) * MEM_HEAD_DIM) for h in range(MEM_HEADS)]
    logits = [_dot_nt(q[:, sl], kbuf[slot, :, sl].astype(BF16)) * (MEM_HEAD_DIM ** -0.5) for sl in heads]
    probs = []
    for logit in logits:
        p = jnp.exp(logit - jnp.max(logit, axis=1, keepdims=True))
        probs.append((p / jnp.sum(p, axis=1, keepdims=True)).astype(BF16))
    outs = [_dot(p, vbuf[slot, :, sl].astype(BF16)) for p, sl in zip(probs, heads)]
    o_ref[0] = jnp.concatenate(outs, axis=-1).astype(o_ref.dtype)


def mem_attention_split(q, mk, mv, layer):
    b, t_len, d = q.shape
    m_len = mk.shape[2]
    return pl.pallas_call(
        functools.partial(_mem_attn_split_kernel, layer=layer, n_seq=b),
        grid=(b,),
        in_specs=[pl.BlockSpec((1, t_len, d), lambda i: (i, 0, 0)),
                  pl.BlockSpec(memory_space=pl.ANY), pl.BlockSpec(memory_space=pl.ANY)],
        out_specs=pl.BlockSpec((1, t_len, d), lambda i: (i, 0, 0)),
        out_shape=jax.ShapeDtypeStruct((b, t_len, d), BF16),
        scratch_shapes=[pltpu.VMEM((2, m_len, d), F32), pltpu.VMEM((2, m_len, d), F32),
                        pltpu.SemaphoreType.DMA((2, 2 * MEM_HEADS))],
        compiler_params=_cparams(("arbitrary",)),
        name="mem_attention_split",
    )(q, mk, mv)


def _prep_layer(l, p):
    w = p['w_in'][l]
    cuts = [0]
    for s in (W_GROUP,) * 9 + (IDX_HEADS * IDX_DIM, IDX_DIM, IDX_HEADS):
        cuts.append(cuts[-1] + s)
    piece = lambda i: w[:, cuts[i]:cuts[i + 1]]
    zeros = lambda n: jnp.zeros((D_MODEL, n), w.dtype)
    w_in = jnp.concatenate(
        [piece(i) for i in range(6)] + [piece(6), piece(7), piece(9), piece(8), piece(10), zeros(LANES - IDX_DIM),
                                        piece(11), zeros(LANES - IDX_HEADS)], axis=1).astype(BF16)
    pool_w = p['pool_w'][l]
    pool_bd = jnp.zeros((W_GROUP, W_GROUP), F32)
    for g in range(len(POOL_WINDOWS)):
        pool_bd = pool_bd.at[g * POOL_CH:(g + 1) * POOL_CH, g * POOL_CH:(g + 1) * POOL_CH].set(pool_w[g])
    return {
        'norm_mix': p['norm_mix'][l], 'w_in': w_in,
        'conv_w': p['conv_w'][l], 'pool_w_bd': pool_bd.astype(BF16),
        'pool_scale': p['pool_scale'][l].reshape(1, W_GROUP), 'chunk_norm': p['chunk_norm'][l].reshape(1, W_GROUP),
        'chunk_ws': p['chunk_ws'][l], 'chunk_b': p['chunk_b'][l],
        'w_out_abc': p['w_out'][l][:3 * W_GROUP].astype(BF16), 'w_out_d': p['w_out'][l][3 * W_GROUP:].astype(BF16),
        'norm_mem': p['norm_mem'][l], 'wq_mem': p['wq_mem'][l].astype(BF16), 'wo_mem': p['wo_mem'][l].astype(BF16),
        'norm_ffn': p['norm_ffn'][l],
    }


def _chunk_consts(lw, chunk):
    mask = jnp.tril(jnp.ones((chunk, chunk), dtype=bool))
    wm = jnp.where(mask[None], lw['chunk_ws'][:, :chunk, :chunk], 0).astype(BF16)
    bias = jnp.repeat(lw['chunk_b'][:, :chunk].T, W_GROUP // CHUNK_HEADS, axis=1)
    return wm, bias


def _rope_tables(pos):
    half = HEAD_DIM // 2
    inv = ROPE_THETA ** (-jnp.arange(half, dtype=F32) / half)
    ang = pos.astype(F32)[:, None] * inv[None, :]
    cos, sin = jnp.cos(ang), jnp.sin(ang)
    cos_t = jnp.concatenate([cos, cos] * (LANES // HEAD_DIM), axis=1)
    sin_t = jnp.concatenate([-sin, sin] * (LANES // HEAD_DIM), axis=1)
    return cos_t, sin_t


def _run_trunk(x, pos0, conv_prev, pool_prev, past, mem_k, mem_v, layers, ffn_w, norm_final):
    b, t_len, d = x.shape
    n = b * t_len
    depth = len(layers)
    from_state = past is not None
    pos = pos0 + jnp.arange(t_len, dtype=I32)
    cos_t, sin_t = _rope_tables(pos)
    h = x.reshape(n, d)
    ks, vs, kis, convs, pools, cvs = [], [], [], [], [], []
    for l, lw in enumerate(layers):
        proj = norm_matmul(h, lw['norm_mix'], lw['w_in']).reshape(b, t_len, P_PAD)
        chunk = min(t_len, CHUNK)
        wm, cbias = _chunk_consts(lw, chunk)
        lw_m = dict(lw, chunk_wm=wm, chunk_bias=cbias)
        outs = mixers(proj, conv_prev[:, l] if from_state else None, pool_prev[:, l] if from_state else None,
                      cos_t, sin_t, lw_m, pos0=pos0, from_state=from_state, emit_cv=from_state)
        mix, q, k, kb, v, vb, qi, ki, ki2, wi, conv_s, pool_s = outs[:12]
        if from_state:
            cvs.append(outs[12])
            yd = dsa_sample(q, qi, wi, kb, vb, ki2, *past, l)
        else:
            yd = dsa_prompt(q, qi, wi, kb, vb, ki2)
        h = matmul_res([mix.reshape(n, 3 * W_GROUP), yd.reshape(n, W_GROUP)], [lw['w_out_abc'], lw['w_out_d']], h)
        qm = norm_matmul(h, lw['norm_mem'], lw['wq_mem'], out_dtype=BF16).reshape(b, t_len, d)
        om = mem_attention(qm, mem_k, mem_v, l) if mem_k.ndim == 4 else mem_attention_split(qm, mem_k, mem_v, l)
        h = matmul_res([om.reshape(n, d)], [lw['wo_mem']], h)
        last = l == depth - 1
        fw = ffn_w[l]
        if fw['kind'] == 'dense':
            h = ffn_dense(h, lw['norm_ffn'], fw['w1'], fw['w3'], fw['w2'], norm_final, final_norm=last)
        else:
            h = moe_ffn(h, lw['norm_ffn'], fw['router'], fw['w1'], fw['w3'], fw['w2'], norm_final, final_norm=last)
        ks.append(k.reshape(b, t_len, ATT_HEADS, HEAD_DIM))
        vs.append(v.reshape(b, t_len, ATT_HEADS, HEAD_DIM))
        kis.append(ki[:, :, :IDX_DIM])
        convs.append(conv_s)
        pools.append(pool_s)
    st = lambda xs: jnp.stack(xs, axis=1)
    y = h.reshape(b, t_len, d)
    return y, st(ks), st(vs), st(kis), st(convs), st(pools), (st(cvs) if cvs else None)


def kernel(x_prompt, x_sample, mem_prompt, cache_k, cache_v, cache_kidx, cache_mem_k, cache_mem_v, state_conv, state_pool, page_table, norm_mix, w_in, conv_w, pool_w, pool_scale, chunk_norm, chunk_ws, chunk_b, w_out, norm_mem, wq_mem, wk_mem, wv_mem, wo_mem, norm_ffn, w1_dense, w3_dense, w2_dense, router, w1_moe, w3_moe, w2_moe, norm_final):
    p = {
        'norm_mix': norm_mix, 'w_in': w_in, 'conv_w': conv_w, 'pool_w': pool_w, 'pool_scale': pool_scale,
        'chunk_norm': chunk_norm, 'chunk_ws': chunk_ws, 'chunk_b': chunk_b, 'w_out': w_out, 'norm_mem': norm_mem,
        'wq_mem': wq_mem, 'wo_mem': wo_mem, 'norm_ffn': norm_ffn,
    }
    depth = norm_mix.shape[0]
    layers = [_prep_layer(l, p) for l in range(depth)]
    ffn_w = []
    for l in range(depth):
        j = l // 2
        if l % 2 == 0:
            ffn_w.append({'kind': 'dense', 'w1': w1_dense[j].astype(BF16), 'w3': w3_dense[j].astype(BF16),
                          'w2': w2_dense[j].astype(BF16)})
        else:
            router_pad = jnp.concatenate([router[j], jnp.zeros((D_MODEL, LANES - N_EXPERTS), F32)], axis=1)
            ffn_w.append({'kind': 'moe', 'router': router_pad, 'w1': w1_moe[j].astype(BF16),
                          'w3': w3_moe[j].astype(BF16), 'w2': w2_moe[j].astype(BF16)})

    n_pages, page = page_table.shape[1], cache_k.shape[2]
    past_len = n_pages * page
    n_phys = cache_k.shape[0]
    kt_pages = jnp.transpose(cache_k, (0, 1, 3, 4, 2)).reshape(n_phys, depth, W_GROUP, page)
    vt_pages = jnp.transpose(cache_v, (0, 1, 3, 4, 2)).reshape(n_phys, depth, W_GROUP, page)
    kit_pages = jnp.transpose(cache_kidx, (0, 1, 3, 2))
    past = (kt_pages, vt_pages, kit_pages, page_table)
    y_s, k_s, v_s, ki_s, conv_s, pool_s, cv_s = _run_trunk(
        x_sample, past_len, state_conv, state_pool, past, cache_mem_k, cache_mem_v, layers, ffn_w, norm_final)

    bp, tp, d = x_prompt.shape
    n_mem = mem_prompt.shape[1]
    w_kv = jnp.concatenate([wk_mem[l] for l in range(depth)] + [wv_mem[l] for l in range(depth)], axis=1).astype(BF16)
    mem_kv = norm_matmul(mem_prompt.reshape(bp * n_mem, d), jnp.ones((d,), F32), w_kv, norm=False)
    mem_kv = mem_kv.reshape(bp, n_mem, 2, depth, d)
    mem_k_p = jnp.moveaxis(mem_kv[:, :, 0], 2, 1)
    mem_v_p = jnp.moveaxis(mem_kv[:, :, 1], 2, 1)
    y_p, k_p, v_p, ki_p, conv_p, pool_p, _ = _run_trunk(
        x_prompt, 0, None, None, None, mem_k_p, mem_v_p, layers, ffn_w, norm_final)

    mem_shape = (bp, depth, n_mem, MEM_HEADS, MEM_HEAD_DIM)
    return (y_p, y_s, k_p, v_p, ki_p, mem_k_p.reshape(mem_shape), mem_v_p.reshape(mem_shape), conv_p, pool_p,
            k_s, v_s, ki_s, conv_s, pool_s, cv_s)
```
